```python
import jax, jax.numpy as jnp
from jax import lax
import numpy as np


D_MODEL = 1024
BATCH = 4
SEQ = 4096
DEPTH = 2
DEC_BATCH = 128
DEC_SEQ = 8
PAST_LEN = 2048
PAGE_SIZE = 128

N_BRANCH = 4
MIX_W = D_MODEL // 4
SB_HEADS = 4
SB_HD = MIX_W // SB_HEADS
SB_BLOCK = 128
SB_BIAS_INIT = -7.0
CONV_W = 3
RW_HD = 64
RW_HEADS = MIX_W // RW_HD
RW_DECAY_LORA = 32
RW_A_LORA = 32
RW_G_LORA = 64
RW_COLS = 3 * MIX_W + RW_DECAY_LORA + RW_A_LORA + RW_G_LORA
RW_GN_EPS = 64e-5
CHUNK = 128
SGU_GROUPS = 4
SGU_GD = MIX_W // SGU_GROUPS
N_MEM = 256
X_HEADS = 4
X_HD = D_MODEL // X_HEADS
D_FF = 4 * D_MODEL
ALPHA = (2 * DEPTH) ** 0.25
BETA = (8 * DEPTH) ** -0.25
LN_EPS = 1e-5
OFF_B = 3 * MIX_W
OFF_C = 6 * MIX_W
OFF_D = OFF_C + RW_COLS
IN_COLS = OFF_D + 2 * MIX_W

kernel_name = 'stickbreak_conv_rwkv7_sgu_hybrid_step'


def _ln(x, g, b, eps=LN_EPS):
    xf = x.astype(jnp.float32)
    mu = jnp.mean(xf, -1, keepdims=True)
    var = jnp.mean(jnp.square(xf - mu), -1, keepdims=True)
    return ((xf - mu) * lax.rsqrt(var + eps) * g + b).astype(x.dtype)


def _sb_scores(q, q_pos, k, v, k_pos, bias):
    z = jnp.einsum('nqhe,nkhe->nhqk', q, k, preferred_element_type=jnp.float32) * (SB_HD ** -0.5)
    z = z + bias.astype(jnp.float32)[None, :, None, None]
    causal = k_pos[None, :] < q_pos[:, None]
    log_stop = jnp.where(causal, jax.nn.log_sigmoid(-z), 0.0)
    later = lax.cumsum(log_stop, axis=3, reverse=True) - log_stop
    wts = jnp.where(causal, jnp.exp(jax.nn.log_sigmoid(z) + later), 0.0)
    return jnp.einsum('nhqk,nkhe->nqhe', wts.astype(v.dtype), v)


def _sb_prompt(q, k, v, bias):
    n, L = q.shape[0], q.shape[1]
    nb = L // SB_BLOCK
    pos = jnp.arange(L, dtype=jnp.int32)
    qb = q.reshape(n, nb, SB_BLOCK, SB_HEADS, SB_HD).swapaxes(0, 1)
    pb = pos.reshape(nb, SB_BLOCK)
    out = lax.map(lambda a: _sb_scores(a[0], a[1], k, v, pos, bias), (qb, pb))
    return out.swapaxes(0, 1).reshape(n, L, SB_HEADS, SB_HD)


def _short_conv(z, prev, w):
    L = z.shape[1]
    zp = jnp.concatenate([prev, z], 1)
    y = zp[:, 0:L] * w[0]
    for j in range(1, CONV_W):
        y = y + zp[:, j:j + L] * w[j]
    return y, zp[:, -(CONV_W - 1):]


def _rwkv7(p, shift_prev, S0, mu, w0, w2, a0, a2, g2, kk_s, ka_s, rk, gn_g, gn_b):
    n, L = p.shape[0], p.shape[1]
    f32 = jnp.float32
    p_prev = jnp.concatenate([shift_prev[:, None], p[:, :-1]], 1)
    xs = p + (p_prev - p) * mu
    r = xs[..., 0:MIX_W]
    k = xs[..., MIX_W:2 * MIX_W]
    v = xs[..., 2 * MIX_W:3 * MIX_W]
    o = 3 * MIX_W
    wl = xs[..., o:o + RW_DECAY_LORA]
    o = o + RW_DECAY_LORA
    al = xs[..., o:o + RW_A_LORA]
    gl = xs[..., o + RW_A_LORA:]
    w_log = -jax.nn.softplus(-(w0 + jnp.tanh(wl) @ w2).astype(f32)) - 0.5
    decay = jnp.exp(-jnp.exp(w_log))
    a = jax.nn.sigmoid((a0 + al @ a2).astype(f32))
    g = (jax.nn.sigmoid(gl) @ g2).astype(f32)
    hs = lambda t: t.reshape(n, L, RW_HEADS, RW_HD)
    kk = hs((k * kk_s).astype(f32))
    kk = kk / jnp.maximum(jnp.sqrt(jnp.sum(kk * kk, -1, keepdims=True)), 1e-12)
    k_eff = hs(k.astype(f32) * (1.0 + (a - 1.0) * ka_s))
    r_h = hs(r.astype(f32))
    v_h = hs(v.astype(f32))
    seqs = tuple(t.swapaxes(0, 1) for t in (r_h, hs(decay), k_eff, v_h, -kk, kk * hs(a)))

    def step(S, inp):
        r_t, w_t, k_t, v_t, a_t, b_t = inp
        sa = jnp.einsum('nhij,nhj->nhi', S, a_t)
        S = S * w_t[:, :, None, :] + sa[..., None] * b_t[:, :, None, :] + v_t[..., None] * k_t[:, :, None, :]
        return S, jnp.einsum('nhij,nhj->nhi', S, r_t)

    S_fin, y = lax.scan(step, S0.astype(f32), seqs)
    y = y.swapaxes(0, 1)
    m = jnp.mean(y, -1, keepdims=True)
    var = jnp.mean(jnp.square(y - m), -1, keepdims=True)
    y = ((y - m) * lax.rsqrt(var + RW_GN_EPS)).reshape(n, L, MIX_W) * gn_g + gn_b
    bonus = (jnp.sum(r_h * k_eff * rk, -1, keepdims=True) * v_h).reshape(n, L, MIX_W)
    y = (y + bonus) * g
    return y.astype(p.dtype), p[:, -1], S_fin.astype(S0.dtype)


def _sgu(z, ln_g, ln_b, ws, bias, prompt):
    n, L = z.shape[0], z.shape[1]
    z = jax.nn.gelu(z)
    u, v = z[..., 0:MIX_W], z[..., MIX_W:]
    v = _ln(v, ln_g, ln_b)
    ws_c = ws * jnp.tril(jnp.ones((CHUNK, CHUNK), ws.dtype))
    if prompt:
        vc = v.reshape(n, L // CHUNK, CHUNK, SGU_GROUPS, SGU_GD)
        mixed = jnp.einsum('gts,ncsge->nctge', ws_c, vc) + bias.T[None, None, :, :, None]
    else:
        vc = v.reshape(n, L, SGU_GROUPS, SGU_GD)
        mixed = jnp.einsum('gts,nsge->ntge', ws_c[:, :L, :L], vc) + bias[:, :L].T[None, :, :, None]
    return u * mixed.reshape(n, L, MIX_W), v


def _layer(x, lp, mem_k, mem_v, past_k, past_v, conv_prev, shift_prev, wkv_prev, prompt):
    n, L = x.shape[0], x.shape[1]
    h = x @ lp['w_in']
    q = h[..., 0:MIX_W].reshape(n, L, SB_HEADS, SB_HD)
    k = h[..., MIX_W:2 * MIX_W].reshape(n, L, SB_HEADS, SB_HD)
    v = h[..., 2 * MIX_W:3 * MIX_W].reshape(n, L, SB_HEADS, SB_HD)
    if prompt:
        ya = _sb_prompt(q, k, v, lp['sb_bias'])
    else:
        past_len = past_k.shape[1]
        k_all = jnp.concatenate([past_k, k], 1)
        v_all = jnp.concatenate([past_v, v], 1)
        q_pos = past_len + jnp.arange(L, dtype=jnp.int32)
        k_pos = jnp.arange(past_len + L, dtype=jnp.int32)
        ya = _sb_scores(q, q_pos, k_all, v_all, k_pos, lp['sb_bias'])
    gb = h[..., OFF_B:OFF_B + MIX_W]
    gc = h[..., OFF_B + MIX_W:OFF_B + 2 * MIX_W]
    hb = h[..., OFF_B + 2 * MIX_W:OFF_C]
    conv_out, conv_new = _short_conv(gc * hb, conv_prev, lp['conv_w'])
    yb = gb * conv_out
    yc, shift_new, wkv_new = _rwkv7(h[..., OFF_C:OFF_D], shift_prev, wkv_prev, lp['rw_mu'], lp['rw_w0'],
                                    lp['rw_w2'], lp['rw_a0'], lp['rw_a2'], lp['rw_g2'], lp['rw_kk'],
                                    lp['rw_ka'], lp['rw_rk'], lp['rw_gn_g'], lp['rw_gn_b'])
    yd, sgu_v = _sgu(h[..., OFF_D:], lp['sgu_ln_g'], lp['sgu_ln_b'], lp['sgu_ws'], lp['sgu_b'], prompt)
    br = jnp.stack([ya.reshape(n, L, MIX_W), yb, yc, yd], axis=2)
    proj = jnp.einsum('nlic,icd->nlid', br, lp['w_branch'])
    gates = jax.nn.sigmoid(x @ lp['w_gate'] + lp['b_gate']).reshape(n, L, N_BRANCH, D_MODEL)
    mix = jnp.sum(gates * proj, axis=2) @ lp['w_o']
    x = _ln(ALPHA * x + mix, lp['ln1_g'], lp['ln1_b'])
    qm = (x @ lp['w_mq']).reshape(n, L, X_HEADS, X_HD)
    s = jnp.einsum('nqhe,nkhe->nhqk', qm, mem_k, preferred_element_type=jnp.float32) * (X_HD ** -0.5)
    pr = jax.nn.softmax(s, axis=-1)
    att = jnp.einsum('nhqk,nkhe->nqhe', pr.astype(mem_v.dtype), mem_v).reshape(n, L, D_MODEL) @ lp['w_mo']
    x = _ln(ALPHA * x + att, lp['ln2_g'], lp['ln2_b'])
    ff = jnp.square(jax.nn.relu(x @ lp['w_up'])) @ lp['w_down']
    x = _ln(ALPHA * x + ff, lp['ln3_g'], lp['ln3_b'])
    return x, k, v, conv_new, shift_new, wkv_new, sgu_v


def setup_inputs(seed: int = 0) -> dict:
    key = jax.random.key(seed)
    keys = iter(jax.random.split(key, 64))
    nrm = lambda shape, s=1.0: jax.random.normal(next(keys), shape, jnp.float32) * s
    n_pages = PAST_LEN // PAGE_SIZE
    n_used = DEC_BATCH * n_pages
    n_phys = (5 * n_used) // 4
    perm = jax.random.permutation(next(keys), n_phys)
    page_table = perm[:n_used].reshape(DEC_BATCH, n_pages).astype(jnp.int32)
    Dd = DEPTH
    return {
        'x_prompt': nrm((BATCH, SEQ, D_MODEL)),
        'x_sample': nrm((DEC_BATCH, DEC_SEQ, D_MODEL)),
        'mem_prompt': nrm((BATCH, N_MEM, D_MODEL)),
        'cache_k': nrm((Dd, n_phys, PAGE_SIZE, SB_HEADS, SB_HD)),
        'cache_v': nrm((Dd, n_phys, PAGE_SIZE, SB_HEADS, SB_HD)),
        'page_table': page_table,
        'cache_mem_k': nrm((Dd, DEC_BATCH, N_MEM, X_HEADS, X_HD)),
        'cache_mem_v': nrm((Dd, DEC_BATCH, N_MEM, X_HEADS, X_HD)),
        'state_conv': nrm((Dd, DEC_BATCH, CONV_W - 1, MIX_W)),
        'state_wkv': nrm((Dd, DEC_BATCH, RW_HEADS, RW_HD, RW_HD), 0.3),
        'state_shift': nrm((Dd, DEC_BATCH, RW_COLS)),
        'w_in': nrm((Dd, D_MODEL, IN_COLS), D_MODEL ** -0.5),
        'sb_bias': SB_BIAS_INIT + nrm((Dd, SB_HEADS), 0.1),
        'w_gate': nrm((Dd, D_MODEL, N_BRANCH * D_MODEL), D_MODEL ** -0.5),
        'b_gate': nrm((Dd, N_BRANCH * D_MODEL), 0.01),
        'w_branch': nrm((Dd, N_BRANCH, MIX_W, D_MODEL), MIX_W ** -0.5),
        'w_o': nrm((Dd, D_MODEL, D_MODEL), BETA * D_MODEL ** -0.5),
        'conv_w': nrm((Dd, CONV_W, MIX_W), CONV_W ** -0.5),
        'rw_mu': jax.random.uniform(next(keys), (Dd, RW_COLS), jnp.float32, 0.0, 1.0),
        'rw_w0': jax.random.uniform(next(keys), (Dd, MIX_W), jnp.float32, -6.0, 1.0),
        'rw_w2': nrm((Dd, RW_DECAY_LORA, MIX_W), 0.1 * RW_DECAY_LORA ** -0.5),
        'rw_a0': nrm((Dd, MIX_W), 0.1),
        'rw_a2': nrm((Dd, RW_A_LORA, MIX_W), RW_A_LORA ** -0.5),
        'rw_g2': nrm((Dd, RW_G_LORA, MIX_W), RW_G_LORA ** -0.5),
        'rw_kk': 0.85 + nrm((Dd, MIX_W), 0.05),
        'rw_ka': 1.0 + nrm((Dd, MIX_W), 0.05),
        'rw_rk': nrm((Dd, RW_HEADS, RW_HD), 0.1),
        'rw_gn_g': 1.0 + nrm((Dd, MIX_W), 0.05),
        'rw_gn_b': nrm((Dd, MIX_W), 0.05),
        'sgu_ln_g': 1.0 + nrm((Dd, MIX_W), 0.05),
        'sgu_ln_b': nrm((Dd, MIX_W), 0.05),
        'sgu_ws': nrm((Dd, SGU_GROUPS, CHUNK, CHUNK), CHUNK ** -0.5),
        'sgu_b': 1.0 + nrm((Dd, SGU_GROUPS, CHUNK), 0.05),
        'w_mq': nrm((Dd, D_MODEL, D_MODEL), D_MODEL ** -0.5),
        'w_mk': nrm((Dd, D_MODEL, D_MODEL), D_MODEL ** -0.5),
        'w_mv': nrm((Dd, D_MODEL, D_MODEL), D_MODEL ** -0.5),
        'w_mo': nrm((Dd, D_MODEL, D_MODEL), BETA * D_MODEL ** -0.5),
        'w_up': nrm((Dd, D_MODEL, D_FF), D_MODEL ** -0.5),
        'w_down': nrm((Dd, D_FF, D_MODEL), BETA * D_FF ** -0.5),
        'ln1_g': 1.0 + nrm((Dd, D_MODEL), 0.05),
        'ln1_b': nrm((Dd, D_MODEL), 0.05),
        'ln2_g': 1.0 + nrm((Dd, D_MODEL), 0.05),
        'ln2_b': nrm((Dd, D_MODEL), 0.05),
        'ln3_g': 1.0 + nrm((Dd, D_MODEL), 0.05),
        'ln3_b': nrm((Dd, D_MODEL), 0.05),
    }


def reference(x_prompt, x_sample, mem_prompt, cache_k, cache_v, page_table, cache_mem_k, cache_mem_v,
              state_conv, state_wkv, state_shift, w_in, sb_bias, w_gate, b_gate, w_branch, w_o, conv_w,
              rw_mu, rw_w0, rw_w2, rw_a0, rw_a2, rw_g2, rw_kk, rw_ka, rw_rk, rw_gn_g, rw_gn_b,
              sgu_ln_g, sgu_ln_b, sgu_ws, sgu_b, w_mq, w_mk, w_mv, w_mo, w_up, w_down,
              ln1_g, ln1_b, ln2_g, ln2_b, ln3_g, ln3_b):
    n_p = x_prompt.shape[0]
    n_s = x_sample.shape[0]
    dt = x_prompt.dtype
    conv0 = jnp.zeros((n_p, CONV_W - 1, MIX_W), dt)
    shift0 = jnp.zeros((n_p, RW_COLS), dt)
    wkv0 = jnp.zeros((n_p, RW_HEADS, RW_HD, RW_HD), dt)
    xp = x_prompt
    xs = x_sample
    p_k, p_v, p_mk, p_mv, p_conv, p_wkv, p_shift = [], [], [], [], [], [], []
    s_k, s_v, s_conv, s_wkv, s_shift, s_chunk = [], [], [], [], [], []
    for l in range(DEPTH):
        lp = {'w_in': w_in[l], 'sb_bias': sb_bias[l], 'w_gate': w_gate[l], 'b_gate': b_gate[l],
              'w_branch': w_branch[l],
              'w_o': w_o[l], 'conv_w': conv_w[l], 'rw_mu': rw_mu[l], 'rw_w0': rw_w0[l], 'rw_w2': rw_w2[l],
              'rw_a0': rw_a0[l], 'rw_a2': rw_a2[l], 'rw_g2': rw_g2[l], 'rw_kk': rw_kk[l], 'rw_ka': rw_ka[l],
              'rw_rk': rw_rk[l], 'rw_gn_g': rw_gn_g[l], 'rw_gn_b': rw_gn_b[l], 'sgu_ln_g': sgu_ln_g[l],
              'sgu_ln_b': sgu_ln_b[l], 'sgu_ws': sgu_ws[l], 'sgu_b': sgu_b[l], 'w_mq': w_mq[l],
              'w_mo': w_mo[l], 'w_up': w_up[l], 'w_down': w_down[l], 'ln1_g': ln1_g[l], 'ln1_b': ln1_b[l],
              'ln2_g': ln2_g[l], 'ln2_b': ln2_b[l], 'ln3_g': ln3_g[l], 'ln3_b': ln3_b[l]}
        mk = (mem_prompt @ w_mk[l]).reshape(n_p, -1, X_HEADS, X_HD)
        mv = (mem_prompt @ w_mv[l]).reshape(n_p, -1, X_HEADS, X_HD)
        xp, k_new, v_new, c_new, sh_new, st_new, _ = _layer(xp, lp, mk, mv, None, None, conv0, shift0, wkv0, True)
        p_k.append(k_new)
        p_v.append(v_new)
        p_mk.append(mk)
        p_mv.append(mv)
        p_conv.append(c_new)
        p_shift.append(sh_new)
        p_wkv.append(st_new)
        past_k = cache_k[l][page_table].reshape(n_s, -1, SB_HEADS, SB_HD)
        past_v = cache_v[l][page_table].reshape(n_s, -1, SB_HEADS, SB_HD)
        xs, k_new, v_new, c_new, sh_new, st_new, cv_new = _layer(
            xs, lp, cache_mem_k[l], cache_mem_v[l], past_k, past_v,
            state_conv[l], state_shift[l], state_wkv[l], False)
        s_k.append(k_new)
        s_v.append(v_new)
        s_conv.append(c_new)
        s_shift.append(sh_new)
        s_wkv.append(st_new)
        s_chunk.append(cv_new)
    return (xp, xs,
            jnp.stack(p_k), jnp.stack(p_v), jnp.stack(p_mk), jnp.stack(p_mv),
            jnp.stack(p_conv), jnp.stack(p_wkv), jnp.stack(p_shift),
            jnp.stack(s_k), jnp.stack(s_v), jnp.stack(s_conv), jnp.stack(s_wkv), jnp.stack(s_shift),
            jnp.stack(s_chunk))
```

```python
import functools
import math

import jax
import jax.numpy as jnp
from jax import lax
from jax.experimental import pallas as pl
from jax.experimental.pallas import tpu as pltpu

F32 = jnp.float32
BF16 = jnp.bfloat16

D_MODEL = 1024
DEPTH = 2
MIX_W = 256
HEAD_W = 64
N_HEADS = MIX_W // HEAD_W
RW_COLS = 896
PAGE_SIZE = 128
N_MEM = 256
X_HEADS = 4
X_HD = D_MODEL // X_HEADS
D_FF = 4 * D_MODEL
ALPHA = (2 * DEPTH) ** 0.25
LN_EPS = 1e-5
RW_GN_EPS = 64e-5
SGU_CHUNK = 128
RW_ROWS = 64
VMEM_LIMIT = 56 * 1024 * 1024


def _cparams(*sem):
    return pltpu.CompilerParams(dimension_semantics=sem, vmem_limit_bytes=VMEM_LIMIT)


def _const_spec(shape):
    nd = len(shape)
    return pl.BlockSpec(shape, lambda *_: (0,) * nd, pipeline_mode=pl.Buffered(1))


def _bdot(a, b):
    return jnp.dot(a.astype(BF16), b.astype(BF16), preferred_element_type=F32)


def _bdot_nt(a, b):
    return lax.dot_general(a.astype(BF16), b.astype(BF16), (((1,), (1,)), ((), ())),
                           preferred_element_type=F32)


def _bdot_tn(a, b):
    return lax.dot_general(a.astype(BF16), b.astype(BF16), (((0,), (0,)), ((), ())),
                           preferred_element_type=F32)


def _split2(x):
    hi = x.astype(BF16)
    lo = (x - hi.astype(F32)).astype(BF16)
    return hi, lo


def _split3(x):
    hi = x.astype(BF16)
    r1 = x - hi.astype(F32)
    mid = r1.astype(BF16)
    lo = (r1 - mid.astype(F32)).astype(BF16)
    return hi, mid, lo


def _sel_dot_l(sel_b, x):
    return sum(jnp.dot(sel_b, part, preferred_element_type=F32) for part in _split3(x))


def _sel_dot_r(x, sel_b):
    return sum(jnp.dot(part, sel_b, preferred_element_type=F32) for part in _split3(x))


def _dot3(a, b):
    ah, al = _split2(a)
    bh, bl = _split2(b)
    d = lambda x, y: jnp.dot(x, y, preferred_element_type=F32)
    return d(ah, bh) + d(ah, bl) + d(al, bh)


def _sigmoid(x):
    return 1.0 / (1.0 + jnp.exp(-x))


def _softplus(x):
    return jnp.maximum(x, 0.0) + jnp.log(1.0 + jnp.exp(-jnp.abs(x)))


def _gelu_tanh(x):
    return 0.5 * x * (1.0 + jnp.tanh(0.7978845608028654 * (x + 0.044715 * (x * x * x))))


def _ln_rows(x, g, b, eps=LN_EPS):
    mu = jnp.mean(x, axis=-1, keepdims=True)
    xc = x - mu
    var = jnp.mean(xc * xc, axis=-1, keepdims=True)
    return xc * lax.rsqrt(var + eps) * g + b


def _iota(shape, dim):
    return lax.broadcasted_iota(jnp.int32, shape, dim)


def _head_block_mask(rows, row_shift):
    return (_iota((rows, MIX_W), 0) >> row_shift) == (_iota((rows, MIX_W), 1) >> 6)


def _stack_heads(x, mask):
    t = jnp.concatenate([x] * N_HEADS, axis=0)
    return jnp.where(mask, t, jnp.zeros_like(t))


def _unstack_heads(x, rows):
    return x[0:rows] + x[rows:2 * rows] + x[2 * rows:3 * rows] + x[3 * rows:4 * rows]


def _inproj_body(x_ref, wq_ref, wc_ref, wr_ref, ws_ref, oq_ref, oc_ref, or_ref, os_ref):
    xb = x_ref[...].astype(BF16)
    oq_ref[...] = jnp.dot(xb, wq_ref[...], preferred_element_type=F32)
    oc_ref[...] = jnp.dot(xb, wc_ref[...], preferred_element_type=F32)
    or_ref[...] = jnp.dot(xb, wr_ref[...], preferred_element_type=F32)
    os_ref[...] = jnp.dot(xb, ws_ref[...], preferred_element_type=F32)


def _inproj(x, wq, wc, wr, ws, tm):
    t = x.shape[0]
    widths = (wq.shape[1], wc.shape[1], wr.shape[1], ws.shape[1])
    return pl.pallas_call(
        _inproj_body,
        grid=(t // tm,),
        in_specs=[pl.BlockSpec((tm, D_MODEL), lambda i: (i, 0))] + [_const_spec(w.shape) for w in (wq, wc, wr, ws)],
        out_specs=[pl.BlockSpec((tm, w), lambda i: (i, 0)) for w in widths],
        out_shape=[jax.ShapeDtypeStruct((t, w), F32) for w in widths],
        compiler_params=_cparams("parallel"),
        name="inproj",
    )(x, wq, wc, wr, ws)


def _sb_block(qs, kblk, vblk, bias, carry, acc, mask):
    tk = kblk.shape[0]
    z = lax.dot_general(qs, kblk, (((1,), (1,)), ((), ())), preferred_element_type=F32) + bias
    log_stop = -_softplus(z)
    if mask is not None:
        log_stop = jnp.where(mask, log_stop, 0.0)
    r_i = _iota((tk, 2 * tk), 0)
    c_i = _iota((tk, 2 * tk), 1)
    suffix = ((r_i >= c_i) | (c_i >= tk)).astype(BF16)
    hi, lo = _split2(log_stop)
    cs = jnp.dot(hi, suffix, preferred_element_type=F32) + jnp.dot(lo, suffix, preferred_element_type=F32)
    w = jnp.exp(z + cs[:, :tk] + carry)
    if mask is not None:
        w = jnp.where(mask, w, 0.0)
    acc = acc + jnp.dot(w.astype(BF16), vblk, preferred_element_type=F32)
    carry = carry + cs[:, tk:]
    return carry, acc


def _sb_prompt_body(q_ref, k_ref, v_ref, bias_ref, o_ref, kb, vb, carry_ref, acc_ref, *, tq):
    qi = pl.program_id(1)
    rows = N_HEADS * tq

    @pl.when(qi == 0)
    def _():
        kb[...] = k_ref[...].astype(BF16)
        vb[...] = v_ref[...].astype(BF16)

    hmask = _head_block_mask(rows, int(math.log2(tq)))
    qs = _stack_heads((q_ref[...] * (HEAD_W ** -0.5)).astype(BF16), hmask)
    bias = bias_ref[...]
    t_idx = _iota((rows, tq), 0) & (tq - 1)
    causal = _iota((rows, tq), 1) < t_idx

    start = pl.multiple_of(qi * tq, tq)
    carry, acc = _sb_block(qs, kb[pl.ds(start, tq), :], vb[pl.ds(start, tq), :], bias,
                           jnp.zeros((rows, tq), F32), jnp.zeros((rows, MIX_W), F32), causal)
    carry_ref[...] = carry
    acc_ref[...] = acc

    def step(jj, _):
        s = pl.multiple_of((qi - 1 - jj) * tq, tq)
        c, a = _sb_block(qs, kb[pl.ds(s, tq), :], vb[pl.ds(s, tq), :], bias,
                         carry_ref[...], acc_ref[...], None)
        carry_ref[...] = c
        acc_ref[...] = a
        return 0

    lax.fori_loop(0, qi, step, 0)
    acc = jnp.where(hmask, acc_ref[...], 0.0)
    o_ref[...] = _unstack_heads(acc, tq)


def _sb_prompt(h_qkv, bias_rows, n, seq, tq):
    nq = seq // tq
    rows = N_HEADS * tq
    return pl.pallas_call(
        functools.partial(_sb_prompt_body, tq=tq),
        grid=(n, nq),
        in_specs=[
            pl.BlockSpec((tq, MIX_W), lambda b, i: (b * nq + i, 0)),
            pl.BlockSpec((seq, MIX_W), lambda b, i: (b, 1)),
            pl.BlockSpec((seq, MIX_W), lambda b, i: (b, 2)),
            _const_spec((rows, tq)),
        ],
        out_specs=pl.BlockSpec((tq, MIX_W), lambda b, i: (b * nq + i, 0)),
        out_shape=jax.ShapeDtypeStruct((n * seq, MIX_W), F32),
        scratch_shapes=[pltpu.VMEM((seq, MIX_W), BF16), pltpu.VMEM((seq, MIX_W), BF16),
                        pltpu.VMEM((rows, tq), F32), pltpu.VMEM((rows, MIX_W), F32)],
        compiler_params=_cparams("parallel", "arbitrary"),
        name="sb_prompt",
    )(h_qkv, h_qkv, h_qkv, bias_rows)


def _sb_sample_body(pt_ref, q_ref, kn_ref, vn_ref, kp_ref, vp_ref, bias_ref, o_ref, qs_ref, carry_ref, acc_ref,
                    *, tq, n_pages):
    jj = pl.program_id(1)
    rows = N_HEADS * tq
    hmask = _head_block_mask(rows, int(math.log2(tq)))

    @pl.when(jj == 0)
    def _():
        qs = _stack_heads(q_ref[...] * (HEAD_W ** -0.5), hmask).astype(BF16)
        qs_ref[...] = qs
        t_idx = _iota((rows, PAGE_SIZE), 0) & (tq - 1)
        causal = _iota((rows, PAGE_SIZE), 1) < t_idx
        carry, acc = _sb_block(qs, kn_ref[...].astype(BF16), vn_ref[...].astype(BF16), bias_ref[...],
                               jnp.zeros((rows, PAGE_SIZE), F32), jnp.zeros((rows, MIX_W), F32), causal)
        carry_ref[...] = carry
        acc_ref[...] = acc

    @pl.when(jj > 0)
    def _():
        carry, acc = _sb_block(qs_ref[...], kp_ref[...].astype(BF16), vp_ref[...].astype(BF16), bias_ref[...],
                               carry_ref[...], acc_ref[...], None)
        carry_ref[...] = carry
        acc_ref[...] = acc

    @pl.when(jj == n_pages)
    def _():
        acc = jnp.where(hmask, acc_ref[...], 0.0)
        o_ref[...] = _unstack_heads(acc, tq)


def _sb_sample(h_qkv, k_new_pad, v_new_pad, cache_k, cache_v, page_table, bias_rows, nb, tq):
    n_pages = page_table.shape[1]
    rows = N_HEADS * tq

    def page_map(b, j, pt):
        return (pt[b, jnp.where(j == 0, n_pages - 1, n_pages - j)], 0, 0)

    grid_spec = pltpu.PrefetchScalarGridSpec(
        num_scalar_prefetch=1,
        grid=(nb, n_pages + 1),
        in_specs=[
            pl.BlockSpec((tq, MIX_W), lambda b, j, pt: (b, 0)),
            pl.BlockSpec((None, PAGE_SIZE, MIX_W), lambda b, j, pt: (b, 0, 0)),
            pl.BlockSpec((None, PAGE_SIZE, MIX_W), lambda b, j, pt: (b, 0, 0)),
            pl.BlockSpec((None, PAGE_SIZE, MIX_W), page_map),
            pl.BlockSpec((None, PAGE_SIZE, MIX_W), page_map),
            pl.BlockSpec((rows, PAGE_SIZE), lambda b, j, pt: (0, 0)),
        ],
        out_specs=pl.BlockSpec((tq, MIX_W), lambda b, j, pt: (b, 0)),
        scratch_shapes=[pltpu.VMEM((rows, MIX_W), BF16), pltpu.VMEM((rows, PAGE_SIZE), F32),
                        pltpu.VMEM((rows, MIX_W), F32)],
    )
    return pl.pallas_call(
        functools.partial(_sb_sample_body, tq=tq, n_pages=n_pages),
        grid_spec=grid_spec,
        out_shape=jax.ShapeDtypeStruct((nb * tq, MIX_W), F32),
        compiler_params=_cparams("parallel", "arbitrary"),
        name="sb_sample",
    )(page_table, h_qkv, k_new_pad, v_new_pad, cache_k, cache_v, bias_rows)


def _rwkv_problem(p, p_prev, s_list, mu, par, w2p, a2p, g2p, masks, *, nb, c):
    g_rows = RW_ROWS
    ones_bd, bd, m_strict, m_incl, eye4, l_tri, l_all, eye_s = masks
    w0, a0, kk_s, ka_s, rk, gn_g, gn_b = (par[i:i + 1, :] for i in range(7))

    xs = p + (p_prev - p) * mu
    r = xs[:, 0:256]
    k = xs[:, 256:512]
    v = xs[:, 512:768]
    lora = xs[:, 768:896]
    u = w0 + _bdot(jnp.tanh(lora), w2p)
    logw = (-math.exp(-0.5)) * _sigmoid(u)
    asig = _sigmoid(a0 + _bdot(lora, a2p))
    gate = _bdot(_sigmoid(lora), g2p)
    kk = k * kk_s
    kk = kk / jnp.maximum(jnp.sqrt(_sel_dot_r(kk * kk, ones_bd)), 1e-12)
    k_eff = k * (1.0 + (asig - 1.0) * ka_s)
    bonus = _sel_dot_r(r * k_eff * rk, ones_bd) * v
    a = -kk
    b = kk * asig

    cum = _sel_dot_l(l_tri, logw)
    cum_all = _sel_dot_l(l_all, logw)
    rt = r * jnp.exp(cum)
    at = a * jnp.exp(cum - logw)
    ginv = jnp.exp(-cum)
    bt = b * ginv
    kt = k_eff * ginv
    e_tail = jnp.exp(cum_all - cum)
    bg = b * e_tail
    kg = k_eff * e_tail
    g_end = jnp.exp(cum_all)

    tile = lambda x: jnp.concatenate([x] * N_HEADS, axis=0)
    stack = lambda x: _stack_heads(x, bd)
    at_b, rt_b, bt_b, kt_b, v_b = (x.astype(BF16) for x in (at, rt, bt, kt, v))
    lhs = jnp.concatenate([tile(at_b), tile(rt_b)], axis=0)
    rhs = jnp.concatenate([stack(bt_b), stack(kt_b)], axis=0)
    a_raw = lax.dot_general(lhs, rhs, (((1,), (1,)), ((), ())), preferred_element_type=F32)
    r4 = N_HEADS * g_rows
    a_ab = jnp.where(m_strict, a_raw[:r4, :r4], 0.0)
    a_ak = jnp.where(m_strict, a_raw[:r4, r4:], 0.0)
    a_rb = jnp.where(m_incl, a_raw[r4:, :r4], 0.0)
    a_rk = jnp.where(m_incl, a_raw[r4:, r4:], 0.0)

    inv = eye4 + a_ab
    apow = a_ab
    span = 1
    while 2 * span < c:
        apow = _bdot(apow, apow)
        inv = inv + _bdot(inv, apow)
        span *= 2

    sv = stack(v_b)
    w1 = _bdot(a_ak, sv)
    uu = _bdot(inv, jnp.concatenate([stack(at_b), w1.astype(BF16)], axis=1))
    ua_s = uu[:, :MIX_W]
    uv_s = uu[:, MIX_W:]
    a_rb_b = a_rb.astype(BF16)
    qe_s = stack(rt_b).astype(F32) + _bdot(a_rb_b, ua_s)
    y0_s = _bdot(a_rb_b, uv_s) + _bdot(a_rk, sv)
    qe = _unstack_heads(qe_s, g_rows)
    y0 = _unstack_heads(y0_s, g_rows)
    ua = _unstack_heads(ua_s, g_rows)
    uv = _unstack_heads(uv_s, g_rows)

    bd_sq = bd[:MIX_W, :]
    row_seq = _iota((g_rows, MIX_W), 0) >> int(math.log2(c))
    ys = []
    s_new = []
    for s in range(nb):
        s_old = s_list[s]
        ys.append(_bdot_nt(qe[s * c:(s + 1) * c], s_old) + y0[s * c:(s + 1) * c])
        if nb == 1:
            bg_s, kg_s = bg, kg
        else:
            bg_s = jnp.where(row_seq == s, bg, 0.0)
            kg_s = jnp.where(row_seq == s, kg, 0.0)
        m_c = jnp.where(bd_sq, _bdot_tn(ua, bg_s), 0.0) + jnp.where(eye_s, g_end[s * c:s * c + 1, :], 0.0)
        n_c = jnp.where(bd_sq, _bdot_tn(uv, bg_s) + _bdot_tn(v, kg_s), 0.0)
        s_new.append(_dot3(s_old, m_c) + n_c)
    y = ys[0] if nb == 1 else jnp.concatenate(ys, axis=0)

    mean = _sel_dot_r(y, ones_bd) * (1.0 / HEAD_W)
    d = y - mean
    var = _sel_dot_r(d * d, ones_bd) * (1.0 / HEAD_W)
    yn = d * lax.rsqrt(var + RW_GN_EPS) * gn_g + gn_b
    return (yn + bonus) * gate, s_new


def _rwkv_body(p_ref, pe_ref, s0_ref, mu_ref, par_ref, w2_ref, a2_ref, g2_ref, y_ref, so_ref, s_scr, plast_scr,
               *, n_prob, nb, c):
    ci = pl.program_id(1)
    g_rows = RW_ROWS
    r4 = N_HEADS * g_rows
    log_c = int(math.log2(c))

    @pl.when(ci == 0)
    def _():
        s_scr[...] = s0_ref[...]
        if nb == 1:
            plast_scr[...] = pe_ref[...]

    lane_h = _iota((MIX_W, MIX_W), 1) >> 6
    row_h = _iota((MIX_W, MIX_W), 0) >> 6
    ones_bd = (lane_h == row_h).astype(BF16)
    bd = _head_block_mask(r4, 6)
    iq = _iota((r4, r4), 0)
    jq = _iota((r4, r4), 1)
    same = (iq >> log_c) == (jq >> log_c)
    m_strict = same & (jq < iq)
    m_incl = same & (jq <= iq)
    eye4 = (iq == jq).astype(F32)
    ig = _iota((g_rows, g_rows), 0)
    jg = _iota((g_rows, g_rows), 1)
    same_g = (ig >> log_c) == (jg >> log_c)
    l_tri = (same_g & (jg <= ig)).astype(BF16)
    l_all = same_g.astype(BF16)
    eye_s = _iota((MIX_W, MIX_W), 0) == _iota((MIX_W, MIX_W), 1)
    masks = (ones_bd, bd, m_strict, m_incl, eye4, l_tri, l_all, eye_s)

    mu = mu_ref[...]
    par = par_ref[...]
    w2p, a2p, g2p = w2_ref[...], a2_ref[...], g2_ref[...]
    row = _iota((g_rows, RW_COLS), 0)
    for pi in range(n_prob):
        p = p_ref[pi]
        rolled = pltpu.roll(p, 1, 0)
        if nb == 1:
            p_prev = jnp.where(row == 0, plast_scr[pi][7:8, :], rolled)
            plast_scr[pi] = p[g_rows - 8:g_rows, :]
        else:
            p_prev = jnp.where((row & (c - 1)) == 0, pe_ref[pi], rolled)
        s_list = [s_scr[pi * nb + s] for s in range(nb)]
        y, s_new = _rwkv_problem(p, p_prev, s_list, mu, par, w2p, a2p, g2p, masks, nb=nb, c=c)
        y_ref[pi] = y
        for s in range(nb):
            s_scr[pi * nb + s] = s_new[s]
    so_ref[...] = s_scr[...]


def _rwkv(p3, pe, s0, mu, par, w2p, a2p, g2p, n_prob, nb, c):
    n_grp, lt, _ = p3.shape
    pe_rows = pe.shape[1]
    n_state = n_prob * nb
    return pl.pallas_call(
        functools.partial(_rwkv_body, n_prob=n_prob, nb=nb, c=c),
        grid=(n_grp // n_prob, lt // RW_ROWS),
        in_specs=[
            pl.BlockSpec((n_prob, RW_ROWS, RW_COLS), lambda i, j: (i, j, 0)),
            pl.BlockSpec((n_prob, pe_rows, RW_COLS), lambda i, j: (i, 0, 0)),
            pl.BlockSpec((n_state, MIX_W, MIX_W), lambda i, j: (i, 0, 0)),
            _const_spec(mu.shape), _const_spec(par.shape),
            _const_spec(w2p.shape), _const_spec(a2p.shape), _const_spec(g2p.shape),
        ],
        out_specs=[
            pl.BlockSpec((n_prob, RW_ROWS, MIX_W), lambda i, j: (i, j, 0)),
            pl.BlockSpec((n_state, MIX_W, MIX_W), lambda i, j: (i, 0, 0)),
        ],
        out_shape=[jax.ShapeDtypeStruct((n_grp, lt, MIX_W), F32),
                   jax.ShapeDtypeStruct((n_grp * nb, MIX_W, MIX_W), F32)],
        scratch_shapes=[pltpu.VMEM((n_state, MIX_W, MIX_W), F32), pltpu.VMEM((n_prob, 8, RW_COLS), F32)],
        compiler_params=_cparams("parallel", "arbitrary"),
        name="rwkv7",
    )(p3, pe, s0, mu, par, w2p, a2p, g2p)


def _merge_body(x_ref, ya_ref, yc_ref, hc_ref, cp_ref, e0_ref, e1_ref, hs_ref, cw_ref, sln_ref, wm_ref, sb_ref,
                wg_ref, bg_ref, wb_ref, wo_ref, ln_ref, o_ref, z_ref, sv_ref, *, tm, seg, tiles_per_seq):
    i = pl.program_id(0)
    x = x_ref[...]
    hc = hc_ref[...]
    gb = hc[:, 0:MIX_W]
    z = hc[:, MIX_W:2 * MIX_W] * hc[:, 2 * MIX_W:3 * MIX_W]
    row = _iota((tm, MIX_W), 0)
    z1 = pltpu.roll(z, 1, 0)
    z2 = pltpu.roll(z, 2, 0)
    if seg >= tm:
        cp = cp_ref[...]
        zp = cp[:, MIX_W:2 * MIX_W] * cp[:, 2 * MIX_W:3 * MIX_W]
        zp = jnp.where(i % tiles_per_seq == 0, jnp.zeros_like(zp), zp)
        e1 = zp[7:8, :]
        e0 = zp[6:7, :]
        pos = row
    else:
        e1 = e1_ref[...]
        e0 = e0_ref[...]
        pos = row & (seg - 1)
    z1 = jnp.where(pos == 0, e1, z1)
    z2 = jnp.where(pos == 0, e0, jnp.where(pos == 1, e1, z2))
    cw = cw_ref[...]
    yb = gb * (z2 * cw[0:1, :] + z1 * cw[1:2, :] + z * cw[2:3, :])
    z_ref[...] = z[tm - z_ref.shape[0]:, :]

    hs = _gelu_tanh(hs_ref[...])
    u = hs[:, 0:MIX_W]
    sln = sln_ref[...]
    sv = _ln_rows(hs[:, MIX_W:], sln[0:1, :], sln[1:2, :])
    sv_ref[...] = sv
    t_i = _iota((SGU_CHUNK, N_HEADS * SGU_CHUNK), 0)
    s_i = _iota((SGU_CHUNK, N_HEADS * SGU_CHUNK), 1) & (SGU_CHUNK - 1)
    log_seg = int(math.log2(min(seg, SGU_CHUNK)))
    keep = (s_i <= t_i) & ((s_i >> log_seg) == (t_i >> log_seg))
    wm = jnp.where(keep, wm_ref[...], 0.0).astype(BF16)
    gmask = (_iota((N_HEADS * SGU_CHUNK, MIX_W), 0) >> 7) == (_iota((N_HEADS * SGU_CHUNK, MIX_W), 1) >> 6)
    sbias = sb_ref[...]
    svb = sv.astype(BF16)
    yd_parts = []
    for ck in range(tm // SGU_CHUNK):
        v_c = svb[ck * SGU_CHUNK:(ck + 1) * SGU_CHUNK]
        mixed = jnp.dot(wm, _stack_heads(v_c, gmask), preferred_element_type=F32) + sbias
        yd_parts.append(u[ck * SGU_CHUNK:(ck + 1) * SGU_CHUNK] * mixed)
    yd = jnp.concatenate(yd_parts, axis=0)

    xb = x.astype(BF16)
    branches = (ya_ref[...], yb, yc_ref[...], yd)
    mix = jnp.zeros((tm, D_MODEL), F32)
    for br in range(4):
        gate = _sigmoid(jnp.dot(xb, wg_ref[:, br * D_MODEL:(br + 1) * D_MODEL], preferred_element_type=F32)
                        + bg_ref[:, br * D_MODEL:(br + 1) * D_MODEL])
        proj = jnp.dot(branches[br].astype(BF16), wb_ref[br], preferred_element_type=F32)
        mix = mix + gate * proj
    ln = ln_ref[...]
    o_ref[...] = _ln_rows(ALPHA * x + jnp.dot(mix.astype(BF16), wo_ref[...], preferred_element_type=F32),
                          ln[0:1, :], ln[1:2, :])


def _merge(x, ya, yc, h_conv, e0, e1, h_sgu, conv_w, sgu_ln, wm, sgu_bias, wg, bgate, wb, wo, ln1, tm, seg):
    t = x.shape[0]
    prompt = seg >= tm
    tiles_per_seq = max(seg // tm, 1)
    z_rows = 8 if prompt else tm
    n_seq = t // seg if prompt else 0
    if prompt:
        cp_spec = pl.BlockSpec((8, 3 * MIX_W), lambda i: (jnp.maximum(i * (tm // 8) - 1, 0), 0))
        e_spec = _const_spec(e0.shape)
        z_spec = pl.BlockSpec((8, MIX_W), lambda i: (i // tiles_per_seq, 0))
        z_shape = jax.ShapeDtypeStruct((n_seq * 8, MIX_W), F32)
    else:
        cp_spec = pl.BlockSpec((8, 3 * MIX_W), lambda i: (0, 0))
        e_spec = pl.BlockSpec((tm, MIX_W), lambda i: (i, 0))
        z_spec = pl.BlockSpec((tm, MIX_W), lambda i: (i, 0))
        z_shape = jax.ShapeDtypeStruct((t, MIX_W), F32)
    row = lambda w: pl.BlockSpec((tm, w), lambda i: (i, 0))
    return pl.pallas_call(
        functools.partial(_merge_body, tm=tm, seg=seg, tiles_per_seq=tiles_per_seq),
        grid=(t // tm,),
        in_specs=[row(D_MODEL), row(MIX_W), row(MIX_W), row(3 * MIX_W), cp_spec, e_spec, e_spec, row(2 * MIX_W),
                  _const_spec(conv_w.shape), _const_spec(sgu_ln.shape), _const_spec(wm.shape),
                  _const_spec(sgu_bias.shape), _const_spec(wg.shape), _const_spec(bgate.shape),
                  _const_spec(wb.shape), _const_spec(wo.shape), _const_spec(ln1.shape)],
        out_specs=[row(D_MODEL), z_spec, row(MIX_W)],
        out_shape=[jax.ShapeDtypeStruct((t, D_MODEL), F32), z_shape, jax.ShapeDtypeStruct((t, MIX_W), F32)],
        compiler_params=_cparams("arbitrary"),
        name="merge",
    )(x, ya, yc, h_conv, h_conv, e0, e1, h_sgu, conv_w, sgu_ln, wm, sgu_bias, wg, bgate, wb, wo, ln1)


def _matmul2_body(x_ref, w1_ref, w2_ref, o1_ref, o2_ref):
    xb = x_ref[...].astype(BF16)
    o1_ref[...] = jnp.dot(xb, w1_ref[...], preferred_element_type=F32).astype(o1_ref.dtype)
    o2_ref[...] = jnp.dot(xb, w2_ref[...], preferred_element_type=F32).astype(o2_ref.dtype)


def _matmul2(x, w1, w2, out_dtype, tm):
    t, kdim = x.shape
    n1, n2 = w1.shape[1], w2.shape[1]
    return pl.pallas_call(
        _matmul2_body,
        grid=(t // tm,),
        in_specs=[pl.BlockSpec((tm, kdim), lambda i: (i, 0)), _const_spec(w1.shape), _const_spec(w2.shape)],
        out_specs=[pl.BlockSpec((tm, n1), lambda i: (i, 0)), pl.BlockSpec((tm, n2), lambda i: (i, 0))],
        out_shape=[jax.ShapeDtypeStruct((t, n1), out_dtype), jax.ShapeDtypeStruct((t, n2), out_dtype)],
        compiler_params=_cparams("parallel"),
        name="matmul2",
    )(x, w1, w2)


def _matmul_body(x_ref, w_ref, o_ref):
    o_ref[...] = jnp.dot(x_ref[...].astype(BF16), w_ref[...], preferred_element_type=F32).astype(o_ref.dtype)


def _matmul(x, w, out_dtype, tm):
    t, kdim = x.shape
    n = w.shape[1]
    return pl.pallas_call(
        _matmul_body,
        grid=(t // tm,),
        in_specs=[pl.BlockSpec((tm, kdim), lambda i: (i, 0)), _const_spec(w.shape)],
        out_specs=pl.BlockSpec((tm, n), lambda i: (i, 0)),
        out_shape=jax.ShapeDtypeStruct((t, n), out_dtype),
        compiler_params=_cparams("parallel"),
        name="matmul",
    )(x, w)


def _xattn_body(q_ref, k_ref, v_ref, o_ref, *, tq, slots):
    for s in range(slots):
        q = q_ref[s * tq:(s + 1) * tq, :]
        kb = k_ref[s].astype(BF16)
        vb = v_ref[s].astype(BF16)
        for h in range(X_HEADS):
            cols = slice(h * X_HD, (h + 1) * X_HD)
            sc = lax.dot_general(q[:, cols].astype(BF16), kb[:, cols], (((1,), (1,)), ((), ())),
                                 preferred_element_type=F32) * (X_HD ** -0.5)
            m = jnp.max(sc, axis=-1, keepdims=True)
            e = jnp.exp(sc - m)
            pr = e / jnp.sum(e, axis=-1, keepdims=True)
            o = jnp.dot(pr.astype(BF16), vb[:, cols], preferred_element_type=F32)
            o_ref[s * tq:(s + 1) * tq, cols] = o.astype(o_ref.dtype)


def _xattn(q, mem_k, mem_v, tq, slots, tiles_per_mem):
    t = q.shape[0]
    rows = tq * slots
    return pl.pallas_call(
        functools.partial(_xattn_body, tq=tq, slots=slots),
        grid=(t // rows,),
        in_specs=[pl.BlockSpec((rows, D_MODEL), lambda i: (i, 0)),
                  pl.BlockSpec((slots, N_MEM, D_MODEL), lambda i: (i // tiles_per_mem, 0, 0)),
                  pl.BlockSpec((slots, N_MEM, D_MODEL), lambda i: (i // tiles_per_mem, 0, 0))],
        out_specs=pl.BlockSpec((rows, D_MODEL), lambda i: (i, 0)),
        out_shape=jax.ShapeDtypeStruct((t, D_MODEL), q.dtype),
        compiler_params=_cparams("parallel"),
        name="xattn",
    )(q, mem_k, mem_v)


def _proj_ln_body(y_ref, w_ref, x_ref, ln_ref, o_ref):
    ln = ln_ref[...]
    acc = jnp.dot(y_ref[...].astype(BF16), w_ref[...], preferred_element_type=F32)
    o_ref[...] = _ln_rows(ALPHA * x_ref[...] + acc, ln[0:1, :], ln[1:2, :])


def _proj_ln(y, w, x, ln, tm):
    t = x.shape[0]
    return pl.pallas_call(
        _proj_ln_body,
        grid=(t // tm,),
        in_specs=[pl.BlockSpec((tm, D_MODEL), lambda i: (i, 0)), _const_spec(w.shape),
                  pl.BlockSpec((tm, D_MODEL), lambda i: (i, 0)), _const_spec(ln.shape)],
        out_specs=pl.BlockSpec((tm, D_MODEL), lambda i: (i, 0)),
        out_shape=jax.ShapeDtypeStruct((t, D_MODEL), F32),
        compiler_params=_cparams("parallel"),
        name="proj_ln",
    )(y, w, x, ln)


def _mlp_body(x_ref, wu_ref, wd_ref, ln_ref, o_ref):
    x = x_ref[...]
    xb = x.astype(BF16)
    acc = jnp.zeros(x.shape, F32)
    for j in range(D_FF // D_MODEL):
        cols = slice(j * D_MODEL, (j + 1) * D_MODEL)
        hid = jnp.maximum(jnp.dot(xb, wu_ref[:, cols], preferred_element_type=F32), 0.0)
        acc = acc + jnp.dot((hid * hid).astype(BF16), wd_ref[cols, :], preferred_element_type=F32)
    ln = ln_ref[...]
    o_ref[...] = _ln_rows(ALPHA * x + acc, ln[0:1, :], ln[1:2, :])


def _mlp(x, wu, wd, ln, tm):
    t = x.shape[0]
    return pl.pallas_call(
        _mlp_body,
        grid=(t // tm,),
        in_specs=[pl.BlockSpec((tm, D_MODEL), lambda i: (i, 0)), _const_spec(wu.shape), _const_spec(wd.shape),
                  _const_spec(ln.shape)],
        out_specs=pl.BlockSpec((tm, D_MODEL), lambda i: (i, 0)),
        out_shape=jax.ShapeDtypeStruct((t, D_MODEL), F32),
        compiler_params=_cparams("parallel"),
        name="mlp",
    )(x, wu, wd, ln)


def _block_diag_states(s):
    n = s.shape[0]
    eye = jnp.eye(N_HEADS, dtype=s.dtype)
    return (s[:, :, :, None, :] * eye[None, :, None, :, None]).reshape(n, MIX_W, MIX_W)


def _diag_states(s):
    n = s.shape[0]
    s5 = s.reshape(n, N_HEADS, HEAD_W, N_HEADS, HEAD_W)
    return jnp.stack([s5[:, h, :, h, :] for h in range(N_HEADS)], axis=1)


def _layer_params(l, w_in, sb_bias, w_gate, b_gate, w_branch, w_o, conv_w, rw_mu, rw_w0, rw_w2, rw_a0, rw_a2, rw_g2,
                  rw_kk, rw_ka, rw_rk, rw_gn_g, rw_gn_b, sgu_ln_g, sgu_ln_b, sgu_ws, sgu_b, w_mq, w_mk, w_mv, w_mo,
                  w_up, w_down, ln1_g, ln1_b, ln2_g, ln2_b, ln3_g, ln3_b):
    wi = w_in[l].astype(BF16)
    off_b, off_c, off_d = 3 * MIX_W, 6 * MIX_W, 6 * MIX_W + RW_COLS
    zpad = lambda w, r0: jnp.zeros((128, MIX_W), F32).at[r0:r0 + w.shape[0]].set(w).astype(BF16)
    par = jnp.zeros((8, MIX_W), F32)
    for i, vec in enumerate((rw_w0[l], rw_a0[l], rw_kk[l], rw_ka[l], rw_rk[l].reshape(MIX_W), rw_gn_g[l], rw_gn_b[l])):
        par = par.at[i].set(vec)
    return dict(
        w_qkv=wi[:, :off_b], w_conv=wi[:, off_b:off_c], w_rw=wi[:, off_c:off_d], w_sgu=wi[:, off_d:],
        sb_bias=sb_bias[l],
        w_gate=w_gate[l].astype(BF16), b_gate=b_gate[l].reshape(1, -1), w_branch=w_branch[l].astype(BF16),
        w_o=w_o[l].astype(BF16), conv_w=jnp.zeros((8, MIX_W), F32).at[:3].set(conv_w[l]),
        rw_mu=rw_mu[l].reshape(1, RW_COLS), rw_par=par,
        rw_w2=zpad(rw_w2[l], 0), rw_a2=zpad(rw_a2[l], 32), rw_g2=zpad(rw_g2[l], 64),
        sgu_ln=jnp.stack([sgu_ln_g[l], sgu_ln_b[l]]), sgu_ws=sgu_ws[l], sgu_b=sgu_b[l],
        w_mq=w_mq[l].astype(BF16), w_mk=w_mk[l].astype(BF16), w_mv=w_mv[l].astype(BF16), w_mo=w_mo[l].astype(BF16),
        w_up=w_up[l].astype(BF16), w_down=w_down[l].astype(BF16),
        ln1=jnp.stack([ln1_g[l], ln1_b[l]]), ln2=jnp.stack([ln2_g[l], ln2_b[l]]), ln3=jnp.stack([ln3_g[l], ln3_b[l]]),
    )


def _sgu_tables(lp, seg):
    ws = lp['sgu_ws']
    sb = lp['sgu_b']
    if seg < SGU_CHUNK:
        reps = SGU_CHUNK // seg
        ws = jnp.tile(ws[:, :seg, :seg], (1, reps, reps))
        sb = jnp.tile(sb[:, :seg], (1, reps))
    wm = jnp.transpose(ws, (1, 0, 2)).reshape(SGU_CHUNK, N_HEADS * SGU_CHUNK)
    bias = jnp.repeat(sb.T, HEAD_W, axis=1)
    return wm, bias


def _sb_bias_rows(bias, tq, width):
    return jnp.broadcast_to(jnp.repeat(bias, tq)[:, None], (N_HEADS * tq, width)).astype(F32)


def _tail(x, lp, mem_k, mem_v, tm, xattn_tq, xattn_slots, tiles_per_mem):
    qm = _matmul(x, lp['w_mq'], BF16 if xattn_tq % 16 == 0 else F32, tm)
    att = _xattn(qm, mem_k, mem_v, xattn_tq, xattn_slots, tiles_per_mem)
    x = _proj_ln(att, lp['w_mo'], x, lp['ln2'], tm)
    return _mlp(x, lp['w_up'], lp['w_down'], lp['ln3'], tm)


def _layer_prompt(x, lp, mem_k, mem_v, n, seq):
    t = n * seq
    h_qkv, h_conv, h_rw, h_sgu = _inproj(x, lp['w_qkv'], lp['w_conv'], lp['w_rw'], lp['w_sgu'], 512)
    tq = 128
    ya = _sb_prompt(h_qkv, _sb_bias_rows(lp['sb_bias'], tq, tq), n, seq, tq)
    pe = jnp.zeros((n, 8, RW_COLS), F32)
    s0 = jnp.zeros((n, MIX_W, MIX_W), F32)
    yc, s_fin = _rwkv(h_rw.reshape(n, seq, RW_COLS), pe, s0, lp['rw_mu'], lp['rw_par'], lp['rw_w2'], lp['rw_a2'],
                      lp['rw_g2'], n_prob=n, nb=1, c=RW_ROWS)
    wm, sgu_bias = _sgu_tables(lp, SGU_CHUNK)
    zero_e = jnp.zeros((8, MIX_W), F32)
    x1, z_tail, _ = _merge(x, ya, yc.reshape(t, MIX_W), h_conv, zero_e, zero_e, h_sgu, lp['conv_w'], lp['sgu_ln'],
                           wm, sgu_bias, lp['w_gate'], lp['b_gate'], lp['w_branch'], lp['w_o'], lp['ln1'],
                           tm=256, seg=seq)
    x3 = _tail(x1, lp, mem_k, mem_v, 512, 512, 1, seq // 512)
    k_new = h_qkv[:, MIX_W:2 * MIX_W].reshape(n, seq, N_HEADS, HEAD_W)
    v_new = h_qkv[:, 2 * MIX_W:].reshape(n, seq, N_HEADS, HEAD_W)
    conv_new = z_tail.reshape(n, 8, MIX_W)[:, 6:8]
    shift_new = h_rw.reshape(n, seq, RW_COLS)[:, -1]
    return x3, k_new, v_new, conv_new, shift_new, _diag_states(s_fin)


def _layer_sample(x, lp, mem_k, mem_v, cache_k, cache_v, page_table, state_conv, state_shift, state_wkv, nb, seq):
    t = nb * seq
    h_qkv, h_conv, h_rw, h_sgu = _inproj(x, lp['w_qkv'], lp['w_conv'], lp['w_rw'], lp['w_sgu'], 512)
    k_new = h_qkv[:, MIX_W:2 * MIX_W].reshape(nb, seq, MIX_W)
    v_new = h_qkv[:, 2 * MIX_W:].reshape(nb, seq, MIX_W)
    pad = ((0, 0), (0, PAGE_SIZE - seq), (0, 0))
    ya = _sb_sample(h_qkv, jnp.pad(k_new, pad), jnp.pad(v_new, pad), cache_k, cache_v, page_table,
                    _sb_bias_rows(lp['sb_bias'], seq, PAGE_SIZE), nb, seq)
    per = RW_ROWS // seq
    pe = jnp.repeat(state_shift, seq, axis=0).reshape(nb // per, RW_ROWS, RW_COLS)
    yc, s_fin = _rwkv(h_rw.reshape(nb // per, RW_ROWS, RW_COLS), pe, _block_diag_states(state_wkv), lp['rw_mu'],
                      lp['rw_par'], lp['rw_w2'], lp['rw_a2'], lp['rw_g2'], n_prob=2, nb=per, c=seq)
    wm, sgu_bias = _sgu_tables(lp, seq)
    e0 = jnp.repeat(state_conv[:, 0], seq, axis=0)
    e1 = jnp.repeat(state_conv[:, 1], seq, axis=0)
    x1, z_all, sgu_v = _merge(x, ya, yc.reshape(t, MIX_W), h_conv, e0, e1, h_sgu, lp['conv_w'], lp['sgu_ln'], wm,
                              sgu_bias, lp['w_gate'], lp['b_gate'], lp['w_branch'], lp['w_o'], lp['ln1'],
                              tm=256, seg=seq)
    x3 = _tail(x1, lp, mem_k, mem_v, 512, seq, 4, 1)
    conv_new = z_all.reshape(nb, seq, MIX_W)[:, seq - 2:]
    shift_new = h_rw.reshape(nb, seq, RW_COLS)[:, -1]
    return (x3, k_new.reshape(nb, seq, N_HEADS, HEAD_W), v_new.reshape(nb, seq, N_HEADS, HEAD_W), conv_new,
            shift_new, _diag_states(s_fin), sgu_v.reshape(nb, seq, MIX_W))


def kernel(x_prompt, x_sample, mem_prompt, cache_k, cache_v, page_table, cache_mem_k, cache_mem_v, state_conv,
           state_wkv, state_shift, w_in, sb_bias, w_gate, b_gate, w_branch, w_o, conv_w, rw_mu, rw_w0, rw_w2, rw_a0,
           rw_a2, rw_g2, rw_kk, rw_ka, rw_rk, rw_gn_g, rw_gn_b, sgu_ln_g, sgu_ln_b, sgu_ws, sgu_b, w_mq, w_mk, w_mv,
           w_mo, w_up, w_down, ln1_g, ln1_b, ln2_g, ln2_b, ln3_g, ln3_b):
    n_p, seq_p, _ = x_prompt.shape
    n_s, seq_s, _ = x_sample.shape
    n_phys = cache_k.shape[1]
    xp = x_prompt.reshape(n_p * seq_p, D_MODEL)
    xs = x_sample.reshape(n_s * seq_s, D_MODEL)
    mem2d = mem_prompt.reshape(n_p * N_MEM, D_MODEL)
    outs = [[] for _ in range(13)]
    for l in range(DEPTH):
        lp = _layer_params(l, w_in, sb_bias, w_gate, b_gate, w_branch, w_o, conv_w, rw_mu, rw_w0, rw_w2, rw_a0,
                           rw_a2, rw_g2, rw_kk, rw_ka, rw_rk, rw_gn_g, rw_gn_b, sgu_ln_g, sgu_ln_b, sgu_ws, sgu_b,
                           w_mq, w_mk, w_mv, w_mo, w_up, w_down, ln1_g, ln1_b, ln2_g, ln2_b, ln3_g, ln3_b)
        mk, mv = _matmul2(mem2d, lp['w_mk'], lp['w_mv'], F32, 512)
        mk3 = mk.reshape(n_p, N_MEM, D_MODEL)
        mv3 = mv.reshape(n_p, N_MEM, D_MODEL)
        xp, pk, pv, pc, psh, pst = _layer_prompt(xp, lp, mk3, mv3, n_p, seq_p)
        xs, sk, sv, sc, ssh, sst, scv = _layer_sample(
            xs, lp, cache_mem_k[l].reshape(n_s, N_MEM, D_MODEL), cache_mem_v[l].reshape(n_s, N_MEM, D_MODEL),
            cache_k[l].reshape(n_phys, PAGE_SIZE, MIX_W), cache_v[l].reshape(n_phys, PAGE_SIZE, MIX_W), page_table,
            state_conv[l], state_shift[l], state_wkv[l], n_s, seq_s)
        for lst, val in zip(outs, (pk, pv, mk3.reshape(n_p, N_MEM, X_HEADS, X_HD),
                                   mv3.reshape(n_p, N_MEM, X_HEADS, X_HD), pc, pst, psh, sk, sv, sc, sst, ssh, scv)):
            lst.append(val)
    return (xp.reshape(n_p, seq_p, D_MODEL), xs.reshape(n_s, seq_s, D_MODEL)) + tuple(jnp.stack(o) for o in outs)
```

```python
import functools
import math

import jax
import jax.numpy as jnp
from jax import lax
from jax.experimental import pallas as pl
from jax.experimental.pallas import tpu as pltpu

F32 = jnp.float32
BF16 = jnp.bfloat16

D_MODEL = 1024
DEPTH = 2
MIX_W = 256
HEAD_W = 64
N_HEADS = MIX_W // HEAD_W
RW_COLS = 896
PAGE_SIZE = 128
N_MEM = 256
X_HEADS = 4
X_HD = D_MODEL // X_HEADS
D_FF = 4 * D_MODEL
ALPHA = (2 * DEPTH) ** 0.25
LN_EPS = 1e-5
RW_GN_EPS = 64e-5
SGU_CHUNK = 128
RW_ROWS = 64
VMEM_LIMIT = 56 * 1024 * 1024


def _cparams(*sem):
    return pltpu.CompilerParams(dimension_semantics=sem, vmem_limit_bytes=VMEM_LIMIT)


def _const_spec(shape):
    nd = len(shape)
    return pl.BlockSpec(shape, lambda *_: (0,) * nd, pipeline_mode=pl.Buffered(1))


def _bdot(a, b):
    return jnp.dot(a.astype(BF16), b.astype(BF16), preferred_element_type=F32)


def _bdot_nt(a, b):
    return lax.dot_general(a.astype(BF16), b.astype(BF16), (((1,), (1,)), ((), ())),
                           preferred_element_type=F32)


def _bdot_tn(a, b):
    return lax.dot_general(a.astype(BF16), b.astype(BF16), (((0,), (0,)), ((), ())),
                           preferred_element_type=F32)


def _split2(x):
    hi = x.astype(BF16)
    lo = (x - hi.astype(F32)).astype(BF16)
    return hi, lo


def _split3(x):
    hi = x.astype(BF16)
    r1 = x - hi.astype(F32)
    mid = r1.astype(BF16)
    lo = (r1 - mid.astype(F32)).astype(BF16)
    return hi, mid, lo


def _sel_dot_l(sel_b, x):
    return sum(jnp.dot(sel_b, part, preferred_element_type=F32) for part in _split3(x))


def _sel_dot_r(x, sel_b):
    return sum(jnp.dot(part, sel_b, preferred_element_type=F32) for part in _split3(x))


def _dot3(a, b):
    ah, al = _split2(a)
    bh, bl = _split2(b)
    d = lambda x, y: jnp.dot(x, y, preferred_element_type=F32)
    return d(ah, bh) + d(ah, bl) + d(al, bh)


def _sigmoid(x):
    return 1.0 / (1.0 + jnp.exp(-x))


def _softplus(x):
    return jnp.maximum(x, 0.0) + jnp.log(1.0 + jnp.exp(-jnp.abs(x)))


def _gelu_tanh(x):
    return 0.5 * x * (1.0 + jnp.tanh(0.7978845608028654 * (x + 0.044715 * (x * x * x))))


def _ln_rows(x, g, b, eps=LN_EPS):
    mu = jnp.mean(x, axis=-1, keepdims=True)
    xc = x - mu
    var = jnp.mean(xc * xc, axis=-1, keepdims=True)
    return xc * lax.rsqrt(var + eps) * g + b


def _iota(shape, dim):
    return lax.broadcasted_iota(jnp.int32, shape, dim)


def _head_block_mask(rows, row_shift):
    return (_iota((rows, MIX_W), 0) >> row_shift) == (_iota((rows, MIX_W), 1) >> 6)


def _stack_heads(x, mask):
    t = jnp.concatenate([x] * N_HEADS, axis=0)
    return jnp.where(mask, t, jnp.zeros_like(t))


def _unstack_heads(x, rows):
    return x[0:rows] + x[rows:2 * rows] + x[2 * rows:3 * rows] + x[3 * rows:4 * rows]


def _inproj_body(x_ref, wq_ref, wc_ref, wr_ref, ws_ref, oq_ref, oc_ref, or_ref, os_ref):
    xb = x_ref[...].astype(BF16)
    oq_ref[...] = jnp.dot(xb, wq_ref[...], preferred_element_type=F32)
    oc_ref[...] = jnp.dot(xb, wc_ref[...], preferred_element_type=F32)
    or_ref[...] = jnp.dot(xb, wr_ref[...], preferred_element_type=F32)
    os_ref[...] = jnp.dot(xb, ws_ref[...], preferred_element_type=F32)


def _inproj(x, wq, wc, wr, ws, tm):
    t = x.shape[0]
    widths = (wq.shape[1], wc.shape[1], wr.shape[1], ws.shape[1])
    return pl.pallas_call(
        _inproj_body,
        grid=(t // tm,),
        in_specs=[pl.BlockSpec((tm, D_MODEL), lambda i: (i, 0))] + [_const_spec(w.shape) for w in (wq, wc, wr, ws)],
        out_specs=[pl.BlockSpec((tm, w), lambda i: (i, 0)) for w in widths],
        out_shape=[jax.ShapeDtypeStruct((t, w), F32) for w in widths],
        compiler_params=_cparams("parallel"),
        name="inproj",
    )(x, wq, wc, wr, ws)


def _suffix_ones(tk):
    return (_iota((tk, tk), 0) >= _iota((tk, tk), 1)).astype(BF16)


def _sb_weights(z, carry, suffix, mask):
    tk = z.shape[1]
    log_stop = -_softplus(z)
    if mask is not None:
        log_stop = jnp.where(mask, log_stop, 0.0)
    hi, lo = _split2(log_stop)
    cs = jnp.dot(hi, suffix, preferred_element_type=F32) + jnp.dot(lo, suffix, preferred_element_type=F32)
    later = carry if tk == 128 else jnp.concatenate([carry] * (tk // 128), axis=1)
    w = jnp.exp(z + cs + later)
    if mask is not None:
        w = jnp.where(mask, w, 0.0)
    total = jnp.sum(log_stop, axis=1, keepdims=True)
    return w, carry + jnp.broadcast_to(total, carry.shape)


def _sb_prompt_body(q_ref, k_ref, v_ref, bias_ref, o_ref, kb, vb, carry_ref, acc_ref, *, tq):
    qi = pl.program_id(1)
    rows = N_HEADS * tq

    @pl.when(qi == 0)
    def _():
        kb[...] = k_ref[...].astype(BF16)
        vb[...] = v_ref[...].astype(BF16)

    hmask = _head_block_mask(rows, int(math.log2(tq)))
    qs = _stack_heads((q_ref[...] * (HEAD_W ** -0.5)).astype(BF16), hmask)
    suffix = _suffix_ones(tq)

    def block(start, carry, acc, mask):
        z = lax.dot_general(qs, kb[pl.ds(start, tq), :], (((1,), (1,)), ((), ())),
                            preferred_element_type=F32) + bias_ref[...]
        w, carry = _sb_weights(z, carry, suffix, mask)
        acc = acc + jnp.dot(w.astype(BF16), vb[pl.ds(start, tq), :], preferred_element_type=F32)
        return carry, acc

    t_idx = _iota((rows, tq), 0) & (tq - 1)
    causal = _iota((rows, tq), 1) < t_idx
    carry, acc = block(pl.multiple_of(qi * tq, tq), jnp.zeros((rows, 128), F32),
                       jnp.zeros((rows, MIX_W), F32), causal)
    carry_ref[...] = carry
    acc_ref[...] = acc

    def step(jj, _):
        c, a = block(pl.multiple_of((qi - 1 - jj) * tq, tq), carry_ref[...], acc_ref[...], None)
        carry_ref[...] = c
        acc_ref[...] = a
        return 0

    lax.fori_loop(0, qi, step, 0)
    acc = jnp.where(hmask, acc_ref[...], 0.0)
    o_ref[...] = _unstack_heads(acc, tq)


def _sb_prompt(h_qkv, bias_rows, n, seq, tq):
    nq = seq // tq
    rows = N_HEADS * tq
    return pl.pallas_call(
        functools.partial(_sb_prompt_body, tq=tq),
        grid=(n, nq),
        in_specs=[
            pl.BlockSpec((tq, MIX_W), lambda b, i: (b * nq + i, 0)),
            pl.BlockSpec((seq, MIX_W), lambda b, i: (b, 1)),
            pl.BlockSpec((seq, MIX_W), lambda b, i: (b, 2)),
            _const_spec((rows, tq)),
        ],
        out_specs=pl.BlockSpec((tq, MIX_W), lambda b, i: (b * nq + i, 0)),
        out_shape=jax.ShapeDtypeStruct((n * seq, MIX_W), F32),
        scratch_shapes=[pltpu.VMEM((seq, MIX_W), BF16), pltpu.VMEM((seq, MIX_W), BF16),
                        pltpu.VMEM((rows, 128), F32), pltpu.VMEM((rows, MIX_W), F32)],
        compiler_params=_cparams("parallel", "arbitrary"),
        name="sb_prompt",
    )(h_qkv, h_qkv, h_qkv, bias_rows)


SB_PAGE_GROUP = 4
SB_GROUP_SLOTS = 3


def _sb_sample_body(pt_ref, q_ref, kn_ref, vn_ref, ck_ref, cv_ref, bias_ref, o_ref, kbuf, vbuf, sem,
                    *, layer, tq, n_pages):
    b = pl.program_id(0)
    groups_per_seq = n_pages // SB_PAGE_GROUP
    n_groups = pl.num_programs(0) * groups_per_seq
    rows = N_HEADS * tq

    def group_copies(g):
        seq_i = g // groups_per_seq
        first = (g - seq_i * groups_per_seq) * SB_PAGE_GROUP
        slot = lax.rem(g, SB_GROUP_SLOTS)
        out = []
        for p in range(SB_PAGE_GROUP):
            page = pt_ref[seq_i, n_pages - 1 - (first + p)]
            out.append(pltpu.make_async_copy(ck_ref.at[layer, page], kbuf.at[slot, p], sem.at[slot]))
            out.append(pltpu.make_async_copy(cv_ref.at[layer, page], vbuf.at[slot, p], sem.at[slot]))
        return slot, out

    @pl.when(b == 0)
    def _():
        for g in range(SB_GROUP_SLOTS - 1):
            for cp in group_copies(jnp.int32(g))[1]:
                cp.start()

    hmask = _head_block_mask(rows, int(math.log2(tq)))
    qs = _stack_heads(q_ref[...] * (HEAD_W ** -0.5), hmask).astype(BF16)
    bias = bias_ref[...]
    suffix = _suffix_ones(PAGE_SIZE)

    pad = jnp.zeros((PAGE_SIZE - tq, MIX_W), F32)
    k_new = jnp.concatenate([kn_ref[...], pad], axis=0).astype(BF16)
    v_new = jnp.concatenate([vn_ref[...], pad], axis=0).astype(BF16)
    t_idx = _iota((rows, PAGE_SIZE), 0) & (tq - 1)
    causal = _iota((rows, PAGE_SIZE), 1) < t_idx
    z = lax.dot_general(qs, k_new, (((1,), (1,)), ((), ())), preferred_element_type=F32) + bias
    w, carry = _sb_weights(z, jnp.zeros((rows, 128), F32), suffix, causal)
    acc = jnp.dot(w.astype(BF16), v_new, preferred_element_type=F32)

    for gi in range(groups_per_seq):
        g = b * groups_per_seq + gi
        ahead = g + (SB_GROUP_SLOTS - 1)

        @pl.when(ahead < n_groups)
        def _():
            for cp in group_copies(ahead)[1]:
                cp.start()

        slot, copies = group_copies(g)
        for cp in copies:
            cp.wait()
        for p in range(SB_PAGE_GROUP):
            k_t = kbuf[slot, p].astype(BF16)
            v_t = vbuf[slot, p].astype(BF16)
            z = jnp.dot(qs, k_t, preferred_element_type=F32) + bias
            w, carry = _sb_weights(z, carry, suffix, None)
            acc = acc + lax.dot_general(w.astype(BF16), v_t, (((1,), (1,)), ((), ())),
                                        preferred_element_type=F32)

    o_ref[...] = _unstack_heads(jnp.where(hmask, acc, 0.0), tq)


def _sb_sample(h_qkv, cache_kt, cache_vt, page_table, bias_rows, layer, nb, tq):
    n_pages = page_table.shape[1]
    rows = N_HEADS * tq
    buf = (SB_GROUP_SLOTS, SB_PAGE_GROUP, MIX_W, PAGE_SIZE)
    grid_spec = pltpu.PrefetchScalarGridSpec(
        num_scalar_prefetch=1,
        grid=(nb,),
        in_specs=[
            pl.BlockSpec((tq, MIX_W), lambda b, pt: (b, 0)),
            pl.BlockSpec((tq, MIX_W), lambda b, pt: (b, 1)),
            pl.BlockSpec((tq, MIX_W), lambda b, pt: (b, 2)),
            pl.BlockSpec(memory_space=pl.ANY),
            pl.BlockSpec(memory_space=pl.ANY),
            pl.BlockSpec((rows, PAGE_SIZE), lambda b, pt: (0, 0)),
        ],
        out_specs=pl.BlockSpec((tq, MIX_W), lambda b, pt: (b, 0)),
        scratch_shapes=[pltpu.VMEM(buf, F32), pltpu.VMEM(buf, F32), pltpu.SemaphoreType.DMA((SB_GROUP_SLOTS,))],
    )
    return pl.pallas_call(
        functools.partial(_sb_sample_body, layer=layer, tq=tq, n_pages=n_pages),
        grid_spec=grid_spec,
        out_shape=jax.ShapeDtypeStruct((nb * tq, MIX_W), F32),
        compiler_params=_cparams("arbitrary"),
        name="sb_sample",
    )(page_table, h_qkv, h_qkv, h_qkv, cache_kt, cache_vt, bias_rows)


def _rwkv_problem(p, p_prev, s_list, mu, par, w2p, a2p, g2p, masks, *, nb, c):
    g_rows = RW_ROWS
    ones_bd, bd, m_strict, m_incl, eye4, l_tri, l_all, eye_s = masks
    w0, a0, kk_s, ka_s, rk, gn_g, gn_b = (par[i:i + 1, :] for i in range(7))

    xs = p + (p_prev - p) * mu
    r = xs[:, 0:256]
    k = xs[:, 256:512]
    v = xs[:, 512:768]
    lora = xs[:, 768:896]
    u = w0 + _bdot(jnp.tanh(lora), w2p)
    logw = (-math.exp(-0.5)) * _sigmoid(u)
    asig = _sigmoid(a0 + _bdot(lora, a2p))
    gate = _bdot(_sigmoid(lora), g2p)
    kk = k * kk_s
    kk = kk / jnp.maximum(jnp.sqrt(_sel_dot_r(kk * kk, ones_bd)), 1e-12)
    k_eff = k * (1.0 + (asig - 1.0) * ka_s)
    bonus = _sel_dot_r(r * k_eff * rk, ones_bd) * v
    a = -kk
    b = kk * asig

    cum = _sel_dot_l(l_tri, logw)
    cum_all = _sel_dot_l(l_all, logw)
    rt = r * jnp.exp(cum)
    at = a * jnp.exp(cum - logw)
    ginv = jnp.exp(-cum)
    bt = b * ginv
    kt = k_eff * ginv
    e_tail = jnp.exp(cum_all - cum)
    bg = b * e_tail
    kg = k_eff * e_tail
    g_end = jnp.exp(cum_all)

    tile = lambda x: jnp.concatenate([x] * N_HEADS, axis=0)
    stack = lambda x: _stack_heads(x, bd)
    at_b, rt_b, bt_b, kt_b, v_b = (x.astype(BF16) for x in (at, rt, bt, kt, v))
    lhs = jnp.concatenate([tile(at_b), tile(rt_b)], axis=0)
    rhs = jnp.concatenate([stack(bt_b), stack(kt_b)], axis=0)
    a_raw = lax.dot_general(lhs, rhs, (((1,), (1,)), ((), ())), preferred_element_type=F32)
    r4 = N_HEADS * g_rows
    a_ab = jnp.where(m_strict, a_raw[:r4, :r4], 0.0)
    a_ak = jnp.where(m_strict, a_raw[:r4, r4:], 0.0)
    a_rb = jnp.where(m_incl, a_raw[r4:, :r4], 0.0)
    a_rk = jnp.where(m_incl, a_raw[r4:, r4:], 0.0)

    inv = eye4 + a_ab
    apow = a_ab
    span = 1
    while 2 * span < c:
        apow = _bdot(apow, apow)
        inv = inv + _bdot(inv, apow)
        span *= 2

    sv = stack(v_b)
    w1 = _bdot(a_ak, sv)
    uu = _bdot(inv, jnp.concatenate([stack(at_b), w1.astype(BF16)], axis=1))
    ua_s = uu[:, :MIX_W]
    uv_s = uu[:, MIX_W:]
    a_rb_b = a_rb.astype(BF16)
    qe_s = stack(rt_b).astype(F32) + _bdot(a_rb_b, ua_s)
    y0_s = _bdot(a_rb_b, uv_s) + _bdot(a_rk, sv)
    qe = _unstack_heads(qe_s, g_rows)
    y0 = _unstack_heads(y0_s, g_rows)
    ua = _unstack_heads(ua_s, g_rows)
    uv = _unstack_heads(uv_s, g_rows)

    bd_sq = bd[:MIX_W, :]
    row_seq = _iota((g_rows, MIX_W), 0) >> int(math.log2(c))
    ys = []
    s_new = []
    for s in range(nb):
        s_old = s_list[s]
        ys.append(_bdot_nt(qe[s * c:(s + 1) * c], s_old) + y0[s * c:(s + 1) * c])
        if nb == 1:
            bg_s, kg_s = bg, kg
        else:
            bg_s = jnp.where(row_seq == s, bg, 0.0)
            kg_s = jnp.where(row_seq == s, kg, 0.0)
        m_c = jnp.where(bd_sq, _bdot_tn(ua, bg_s), 0.0) + jnp.where(eye_s, g_end[s * c:s * c + 1, :], 0.0)
        n_c = jnp.where(bd_sq, _bdot_tn(uv, bg_s) + _bdot_tn(v, kg_s), 0.0)
        s_new.append(_dot3(s_old, m_c) + n_c)
    y = ys[0] if nb == 1 else jnp.concatenate(ys, axis=0)

    mean = _sel_dot_r(y, ones_bd) * (1.0 / HEAD_W)
    d = y - mean
    var = _sel_dot_r(d * d, ones_bd) * (1.0 / HEAD_W)
    yn = d * lax.rsqrt(var + RW_GN_EPS) * gn_g + gn_b
    return (yn + bonus) * gate, s_new


def _rwkv_body(p_ref, pe_ref, s0_ref, mu_ref, par_ref, w2_ref, a2_ref, g2_ref, y_ref, so_ref, s_scr, plast_scr,
               *, n_prob, nb, c):
    ci = pl.program_id(1)
    g_rows = RW_ROWS
    r4 = N_HEADS * g_rows
    log_c = int(math.log2(c))

    @pl.when(ci == 0)
    def _():
        s_scr[...] = s0_ref[...]
        if nb == 1:
            plast_scr[...] = pe_ref[...]

    lane_h = _iota((MIX_W, MIX_W), 1) >> 6
    row_h = _iota((MIX_W, MIX_W), 0) >> 6
    ones_bd = (lane_h == row_h).astype(BF16)
    bd = _head_block_mask(r4, 6)
    iq = _iota((r4, r4), 0)
    jq = _iota((r4, r4), 1)
    same = (iq >> log_c) == (jq >> log_c)
    m_strict = same & (jq < iq)
    m_incl = same & (jq <= iq)
    eye4 = (iq == jq).astype(F32)
    ig = _iota((g_rows, g_rows), 0)
    jg = _iota((g_rows, g_rows), 1)
    same_g = (ig >> log_c) == (jg >> log_c)
    l_tri = (same_g & (jg <= ig)).astype(BF16)
    l_all = same_g.astype(BF16)
    eye_s = _iota((MIX_W, MIX_W), 0) == _iota((MIX_W, MIX_W), 1)
    masks = (ones_bd, bd, m_strict, m_incl, eye4, l_tri, l_all, eye_s)

    mu = mu_ref[...]
    par = par_ref[...]
    w2p, a2p, g2p = w2_ref[...], a2_ref[...], g2_ref[...]
    row = _iota((g_rows, RW_COLS), 0)
    for pi in range(n_prob):
        p = p_ref[pi]
        rolled = pltpu.roll(p, 1, 0)
        if nb == 1:
            p_prev = jnp.where(row == 0, plast_scr[pi][7:8, :], rolled)
            plast_scr[pi] = p[g_rows - 8:g_rows, :]
        else:
            p_prev = jnp.where((row & (c - 1)) == 0, pe_ref[pi], rolled)
        s_list = [s_scr[pi * nb + s] for s in range(nb)]
        y, s_new = _rwkv_problem(p, p_prev, s_list, mu, par, w2p, a2p, g2p, masks, nb=nb, c=c)
        y_ref[pi] = y
        for s in range(nb):
            s_scr[pi * nb + s] = s_new[s]
    so_ref[...] = s_scr[...]


def _rwkv(p3, pe, s0, mu, par, w2p, a2p, g2p, n_prob, nb, c):
    n_grp, lt, _ = p3.shape
    pe_rows = pe.shape[1]
    n_state = n_prob * nb
    return pl.pallas_call(
        functools.partial(_rwkv_body, n_prob=n_prob, nb=nb, c=c),
        grid=(n_grp // n_prob, lt // RW_ROWS),
        in_specs=[
            pl.BlockSpec((n_prob, RW_ROWS, RW_COLS), lambda i, j: (i, j, 0)),
            pl.BlockSpec((n_prob, pe_rows, RW_COLS), lambda i, j: (i, 0, 0)),
            pl.BlockSpec((n_state, MIX_W, MIX_W), lambda i, j: (i, 0, 0)),
            _const_spec(mu.shape), _const_spec(par.shape),
            _const_spec(w2p.shape), _const_spec(a2p.shape), _const_spec(g2p.shape),
        ],
        out_specs=[
            pl.BlockSpec((n_prob, RW_ROWS, MIX_W), lambda i, j: (i, j, 0)),
            pl.BlockSpec((n_state, MIX_W, MIX_W), lambda i, j: (i, 0, 0)),
        ],
        out_shape=[jax.ShapeDtypeStruct((n_grp, lt, MIX_W), F32),
                   jax.ShapeDtypeStruct((n_grp * nb, MIX_W, MIX_W), F32)],
        scratch_shapes=[pltpu.VMEM((n_state, MIX_W, MIX_W), F32), pltpu.VMEM((n_prob, 8, RW_COLS), F32)],
        compiler_params=_cparams("parallel", "arbitrary"),
        name="rwkv7",
    )(p3, pe, s0, mu, par, w2p, a2p, g2p)


def _merge_body(x_ref, ya_ref, yc_ref, hc_ref, cp_ref, e0_ref, e1_ref, hs_ref, cw_ref, sln_ref, wm_ref, sb_ref,
                wg_ref, bg_ref, wb_ref, wo_ref, ln_ref, o_ref, z_ref, sv_ref, *, tm, seg, tiles_per_seq):
    i = pl.program_id(0)
    x = x_ref[...]
    hc = hc_ref[...]
    gb = hc[:, 0:MIX_W]
    z = hc[:, MIX_W:2 * MIX_W] * hc[:, 2 * MIX_W:3 * MIX_W]
    row = _iota((tm, MIX_W), 0)
    z1 = pltpu.roll(z, 1, 0)
    z2 = pltpu.roll(z, 2, 0)
    if seg >= tm:
        cp = cp_ref[...]
        zp = cp[:, MIX_W:2 * MIX_W] * cp[:, 2 * MIX_W:3 * MIX_W]
        zp = jnp.where(i % tiles_per_seq == 0, jnp.zeros_like(zp), zp)
        e1 = zp[7:8, :]
        e0 = zp[6:7, :]
        pos = row
    else:
        e1 = e1_ref[...]
        e0 = e0_ref[...]
        pos = row & (seg - 1)
    z1 = jnp.where(pos == 0, e1, z1)
    z2 = jnp.where(pos == 0, e0, jnp.where(pos == 1, e1, z2))
    cw = cw_ref[...]
    yb = gb * (z2 * cw[0:1, :] + z1 * cw[1:2, :] + z * cw[2:3, :])
    z_ref[...] = z[tm - z_ref.shape[0]:, :]

    hs = _gelu_tanh(hs_ref[...])
    u = hs[:, 0:MIX_W]
    sln = sln_ref[...]
    sv = _ln_rows(hs[:, MIX_W:], sln[0:1, :], sln[1:2, :])
    sv_ref[...] = sv
    t_i = _iota((SGU_CHUNK, N_HEADS * SGU_CHUNK), 0)
    s_i = _iota((SGU_CHUNK, N_HEADS * SGU_CHUNK), 1) & (SGU_CHUNK - 1)
    log_seg = int(math.log2(min(seg, SGU_CHUNK)))
    keep = (s_i <= t_i) & ((s_i >> log_seg) == (t_i >> log_seg))
    wm = jnp.where(keep, wm_ref[...], 0.0).astype(BF16)
    gmask = (_iota((N_HEADS * SGU_CHUNK, MIX_W), 0) >> 7) == (_iota((N_HEADS * SGU_CHUNK, MIX_W), 1) >> 6)
    sbias = sb_ref[...]
    svb = sv.astype(BF16)
    yd_parts = []
    for ck in range(tm // SGU_CHUNK):
        v_c = svb[ck * SGU_CHUNK:(ck + 1) * SGU_CHUNK]
        mixed = jnp.dot(wm, _stack_heads(v_c, gmask), preferred_element_type=F32) + sbias
        yd_parts.append(u[ck * SGU_CHUNK:(ck + 1) * SGU_CHUNK] * mixed)
    yd = jnp.concatenate(yd_parts, axis=0)

    xb = x.astype(BF16)
    branches = (ya_ref[...], yb, yc_ref[...], yd)
    mix = jnp.zeros((tm, D_MODEL), F32)
    for br in range(4):
        gate = _sigmoid(jnp.dot(xb, wg_ref[:, br * D_MODEL:(br + 1) * D_MODEL], preferred_element_type=F32)
                        + bg_ref[:, br * D_MODEL:(br + 1) * D_MODEL])
        proj = jnp.dot(branches[br].astype(BF16), wb_ref[br], preferred_element_type=F32)
        mix = mix + gate * proj
    ln = ln_ref[...]
    o_ref[...] = _ln_rows(ALPHA * x + jnp.dot(mix.astype(BF16), wo_ref[...], preferred_element_type=F32),
                          ln[0:1, :], ln[1:2, :])


def _merge(x, ya, yc, h_conv, e0, e1, h_sgu, conv_w, sgu_ln, wm, sgu_bias, wg, bgate, wb, wo, ln1, tm, seg):
    t = x.shape[0]
    prompt = seg >= tm
    tiles_per_seq = max(seg // tm, 1)
    z_rows = 8 if prompt else tm
    n_seq = t // seg if prompt else 0
    if prompt:
        cp_spec = pl.BlockSpec((8, 3 * MIX_W), lambda i: (jnp.maximum(i * (tm // 8) - 1, 0), 0))
        e_spec = _const_spec(e0.shape)
        z_spec = pl.BlockSpec((8, MIX_W), lambda i: (i // tiles_per_seq, 0))
        z_shape = jax.ShapeDtypeStruct((n_seq * 8, MIX_W), F32)
    else:
        cp_spec = pl.BlockSpec((8, 3 * MIX_W), lambda i: (0, 0))
        e_spec = pl.BlockSpec((tm, MIX_W), lambda i: (i, 0))
        z_spec = pl.BlockSpec((tm, MIX_W), lambda i: (i, 0))
        z_shape = jax.ShapeDtypeStruct((t, MIX_W), F32)
    row = lambda w: pl.BlockSpec((tm, w), lambda i: (i, 0))
    return pl.pallas_call(
        functools.partial(_merge_body, tm=tm, seg=seg, tiles_per_seq=tiles_per_seq),
        grid=(t // tm,),
        in_specs=[row(D_MODEL), row(MIX_W), row(MIX_W), row(3 * MIX_W), cp_spec, e_spec, e_spec, row(2 * MIX_W),
                  _const_spec(conv_w.shape), _const_spec(sgu_ln.shape), _const_spec(wm.shape),
                  _const_spec(sgu_bias.shape), _const_spec(wg.shape), _const_spec(bgate.shape),
                  _const_spec(wb.shape), _const_spec(wo.shape), _const_spec(ln1.shape)],
        out_specs=[row(D_MODEL), z_spec, row(MIX_W)],
        out_shape=[jax.ShapeDtypeStruct((t, D_MODEL), F32), z_shape, jax.ShapeDtypeStruct((t, MIX_W), F32)],
        compiler_params=_cparams("arbitrary"),
        name="merge",
    )(x, ya, yc, h_conv, h_conv, e0, e1, h_sgu, conv_w, sgu_ln, wm, sgu_bias, wg, bgate, wb, wo, ln1)


def _matmul2_body(x_ref, w1_ref, w2_ref, o1_ref, o2_ref):
    xb = x_ref[...].astype(BF16)
    o1_ref[...] = jnp.dot(xb, w1_ref[...], preferred_element_type=F32).astype(o1_ref.dtype)
    o2_ref[...] = jnp.dot(xb, w2_ref[...], preferred_element_type=F32).astype(o2_ref.dtype)


def _matmul2(x, w1, w2, out_dtype, tm):
    t, kdim = x.shape
    n1, n2 = w1.shape[1], w2.shape[1]
    return pl.pallas_call(
        _matmul2_body,
        grid=(t // tm,),
        in_specs=[pl.BlockSpec((tm, kdim), lambda i: (i, 0)), _const_spec(w1.shape), _const_spec(w2.shape)],
        out_specs=[pl.BlockSpec((tm, n1), lambda i: (i, 0)), pl.BlockSpec((tm, n2), lambda i: (i, 0))],
        out_shape=[jax.ShapeDtypeStruct((t, n1), out_dtype), jax.ShapeDtypeStruct((t, n2), out_dtype)],
        compiler_params=_cparams("parallel"),
        name="matmul2",
    )(x, w1, w2)


def _matmul_body(x_ref, w_ref, o_ref):
    o_ref[...] = jnp.dot(x_ref[...].astype(BF16), w_ref[...], preferred_element_type=F32).astype(o_ref.dtype)


def _matmul(x, w, out_dtype, tm):
    t, kdim = x.shape
    n = w.shape[1]
    return pl.pallas_call(
        _matmul_body,
        grid=(t // tm,),
        in_specs=[pl.BlockSpec((tm, kdim), lambda i: (i, 0)), _const_spec(w.shape)],
        out_specs=pl.BlockSpec((tm, n), lambda i: (i, 0)),
        out_shape=jax.ShapeDtypeStruct((t, n), out_dtype),
        compiler_params=_cparams("parallel"),
        name="matmul",
    )(x, w)


def _xattn_body(q_ref, k_ref, v_ref, o_ref, *, tq, slots):
    for s in range(slots):
        q = q_ref[s * tq:(s + 1) * tq, :]
        for h in range(X_HEADS):
            cols = slice(h * X_HD, (h + 1) * X_HD)
            kb = k_ref[s, :, h, :].astype(BF16)
            vb = v_ref[s, :, h, :].astype(BF16)
            sc = lax.dot_general(q[:, cols].astype(BF16), kb, (((1,), (1,)), ((), ())),
                                 preferred_element_type=F32) * (X_HD ** -0.5)
            m = jnp.max(sc, axis=-1, keepdims=True)
            e = jnp.exp(sc - m)
            pr = e / jnp.sum(e, axis=-1, keepdims=True)
            o = jnp.dot(pr.astype(BF16), vb, preferred_element_type=F32)
            o_ref[s * tq:(s + 1) * tq, cols] = o.astype(o_ref.dtype)


def _xattn(q, mem_k, mem_v, layer, tq, slots, tiles_per_mem):
    t = q.shape[0]
    rows = tq * slots
    mem_spec = pl.BlockSpec((None, slots, N_MEM, X_HEADS, X_HD), lambda i: (layer, i // tiles_per_mem, 0, 0, 0))
    return pl.pallas_call(
        functools.partial(_xattn_body, tq=tq, slots=slots),
        grid=(t // rows,),
        in_specs=[pl.BlockSpec((rows, D_MODEL), lambda i: (i, 0)), mem_spec, mem_spec],
        out_specs=pl.BlockSpec((rows, D_MODEL), lambda i: (i, 0)),
        out_shape=jax.ShapeDtypeStruct((t, D_MODEL), q.dtype),
        compiler_params=_cparams("parallel"),
        name="xattn",
    )(q, mem_k, mem_v)


def _proj_ln_body(y_ref, w_ref, x_ref, ln_ref, o_ref):
    ln = ln_ref[...]
    acc = jnp.dot(y_ref[...].astype(BF16), w_ref[...], preferred_element_type=F32)
    o_ref[...] = _ln_rows(ALPHA * x_ref[...] + acc, ln[0:1, :], ln[1:2, :])


def _proj_ln(y, w, x, ln, tm):
    t = x.shape[0]
    return pl.pallas_call(
        _proj_ln_body,
        grid=(t // tm,),
        in_specs=[pl.BlockSpec((tm, D_MODEL), lambda i: (i, 0)), _const_spec(w.shape),
                  pl.BlockSpec((tm, D_MODEL), lambda i: (i, 0)), _const_spec(ln.shape)],
        out_specs=pl.BlockSpec((tm, D_MODEL), lambda i: (i, 0)),
        out_shape=jax.ShapeDtypeStruct((t, D_MODEL), F32),
        compiler_params=_cparams("parallel"),
        name="proj_ln",
    )(y, w, x, ln)


def _mlp_body(x_ref, wu_ref, wd_ref, ln_ref, o_ref):
    x = x_ref[...]
    xb = x.astype(BF16)
    acc = jnp.zeros(x.shape, F32)
    for j in range(D_FF // D_MODEL):
        cols = slice(j * D_MODEL, (j + 1) * D_MODEL)
        hid = jnp.maximum(jnp.dot(xb, wu_ref[:, cols], preferred_element_type=F32), 0.0)
        acc = acc + jnp.dot((hid * hid).astype(BF16), wd_ref[cols, :], preferred_element_type=F32)
    ln = ln_ref[...]
    o_ref[...] = _ln_rows(ALPHA * x + acc, ln[0:1, :], ln[1:2, :])


def _mlp(x, wu, wd, ln, tm):
    t = x.shape[0]
    return pl.pallas_call(
        _mlp_body,
        grid=(t // tm,),
        in_specs=[pl.BlockSpec((tm, D_MODEL), lambda i: (i, 0)), _const_spec(wu.shape), _const_spec(wd.shape),
                  _const_spec(ln.shape)],
        out_specs=pl.BlockSpec((tm, D_MODEL), lambda i: (i, 0)),
        out_shape=jax.ShapeDtypeStruct((t, D_MODEL), F32),
        compiler_params=_cparams("parallel"),
        name="mlp",
    )(x, wu, wd, ln)


def _block_diag_states(s):
    n = s.shape[0]
    eye = jnp.eye(N_HEADS, dtype=s.dtype)
    return (s[:, :, :, None, :] * eye[None, :, None, :, None]).reshape(n, MIX_W, MIX_W)


def _diag_states(s):
    n = s.shape[0]
    s5 = s.reshape(n, N_HEADS, HEAD_W, N_HEADS, HEAD_W)
    return jnp.stack([s5[:, h, :, h, :] for h in range(N_HEADS)], axis=1)


def _layer_params(l, w_in, sb_bias, w_gate, b_gate, w_branch, w_o, conv_w, rw_mu, rw_w0, rw_w2, rw_a0, rw_a2, rw_g2,
                  rw_kk, rw_ka, rw_rk, rw_gn_g, rw_gn_b, sgu_ln_g, sgu_ln_b, sgu_ws, sgu_b, w_mq, w_mk, w_mv, w_mo,
                  w_up, w_down, ln1_g, ln1_b, ln2_g, ln2_b, ln3_g, ln3_b):
    wi = w_in[l].astype(BF16)
    off_b, off_c, off_d = 3 * MIX_W, 6 * MIX_W, 6 * MIX_W + RW_COLS
    zpad = lambda w, r0: jnp.zeros((128, MIX_W), F32).at[r0:r0 + w.shape[0]].set(w).astype(BF16)
    par = jnp.zeros((8, MIX_W), F32)
    for i, vec in enumerate((rw_w0[l], rw_a0[l], rw_kk[l], rw_ka[l], rw_rk[l].reshape(MIX_W), rw_gn_g[l], rw_gn_b[l])):
        par = par.at[i].set(vec)
    return dict(
        w_qkv=wi[:, :off_b], w_conv=wi[:, off_b:off_c], w_rw=wi[:, off_c:off_d], w_sgu=wi[:, off_d:],
        sb_bias=sb_bias[l],
        w_gate=w_gate[l].astype(BF16), b_gate=b_gate[l].reshape(1, -1), w_branch=w_branch[l].astype(BF16),
        w_o=w_o[l].astype(BF16), conv_w=jnp.zeros((8, MIX_W), F32).at[:3].set(conv_w[l]),
        rw_mu=rw_mu[l].reshape(1, RW_COLS), rw_par=par,
        rw_w2=zpad(rw_w2[l], 0), rw_a2=zpad(rw_a2[l], 32), rw_g2=zpad(rw_g2[l], 64),
        sgu_ln=jnp.stack([sgu_ln_g[l], sgu_ln_b[l]]), sgu_ws=sgu_ws[l], sgu_b=sgu_b[l],
        w_mq=w_mq[l].astype(BF16), w_mk=w_mk[l].astype(BF16), w_mv=w_mv[l].astype(BF16), w_mo=w_mo[l].astype(BF16),
        w_up=w_up[l].astype(BF16), w_down=w_down[l].astype(BF16),
        ln1=jnp.stack([ln1_g[l], ln1_b[l]]), ln2=jnp.stack([ln2_g[l], ln2_b[l]]), ln3=jnp.stack([ln3_g[l], ln3_b[l]]),
    )


def _sgu_tables(lp, seg):
    ws = lp['sgu_ws']
    sb = lp['sgu_b']
    if seg < SGU_CHUNK:
        reps = SGU_CHUNK // seg
        ws = jnp.tile(ws[:, :seg, :seg], (1, reps, reps))
        sb = jnp.tile(sb[:, :seg], (1, reps))
    wm = jnp.transpose(ws, (1, 0, 2)).reshape(SGU_CHUNK, N_HEADS * SGU_CHUNK)
    bias = jnp.repeat(sb.T, HEAD_W, axis=1)
    return wm, bias


def _sb_bias_rows(bias, tq, width):
    return jnp.broadcast_to(jnp.repeat(bias, tq)[:, None], (N_HEADS * tq, width)).astype(F32)


def _tail(x, lp, mem_k, mem_v, mem_layer, tm, xattn_tq, xattn_slots, tiles_per_mem):
    qm = _matmul(x, lp['w_mq'], BF16 if xattn_tq % 16 == 0 else F32, tm)
    att = _xattn(qm, mem_k, mem_v, mem_layer, xattn_tq, xattn_slots, tiles_per_mem)
    x = _proj_ln(att, lp['w_mo'], x, lp['ln2'], tm)
    return _mlp(x, lp['w_up'], lp['w_down'], lp['ln3'], tm)


def _layer_prompt(x, lp, mem_k, mem_v, n, seq):
    t = n * seq
    h_qkv, h_conv, h_rw, h_sgu = _inproj(x, lp['w_qkv'], lp['w_conv'], lp['w_rw'], lp['w_sgu'], 512)
    tq = 256
    ya = _sb_prompt(h_qkv, _sb_bias_rows(lp['sb_bias'], tq, tq), n, seq, tq)
    pe = jnp.zeros((n, 8, RW_COLS), F32)
    s0 = jnp.zeros((n, MIX_W, MIX_W), F32)
    yc, s_fin = _rwkv(h_rw.reshape(n, seq, RW_COLS), pe, s0, lp['rw_mu'], lp['rw_par'], lp['rw_w2'], lp['rw_a2'],
                      lp['rw_g2'], n_prob=n, nb=1, c=RW_ROWS)
    wm, sgu_bias = _sgu_tables(lp, SGU_CHUNK)
    zero_e = jnp.zeros((8, MIX_W), F32)
    x1, z_tail, _ = _merge(x, ya, yc.reshape(t, MIX_W), h_conv, zero_e, zero_e, h_sgu, lp['conv_w'], lp['sgu_ln'],
                           wm, sgu_bias, lp['w_gate'], lp['b_gate'], lp['w_branch'], lp['w_o'], lp['ln1'],
                           tm=256, seg=seq)
    x3 = _tail(x1, lp, mem_k, mem_v, 0, 512, 512, 1, seq // 512)
    k_new = h_qkv[:, MIX_W:2 * MIX_W].reshape(n, seq, N_HEADS, HEAD_W)
    v_new = h_qkv[:, 2 * MIX_W:].reshape(n, seq, N_HEADS, HEAD_W)
    conv_new = z_tail.reshape(n, 8, MIX_W)[:, 6:8]
    shift_new = h_rw.reshape(n, seq, RW_COLS)[:, -1]
    return x3, k_new, v_new, conv_new, shift_new, _diag_states(s_fin)


def _layer_sample(x, lp, layer, mem_k, mem_v, cache_kt, cache_vt, page_table, state_conv, state_shift, state_wkv,
                  nb, seq):
    t = nb * seq
    h_qkv, h_conv, h_rw, h_sgu = _inproj(x, lp['w_qkv'], lp['w_conv'], lp['w_rw'], lp['w_sgu'], 512)
    k_new = h_qkv[:, MIX_W:2 * MIX_W].reshape(nb, seq, MIX_W)
    v_new = h_qkv[:, 2 * MIX_W:].reshape(nb, seq, MIX_W)
    ya = _sb_sample(h_qkv, cache_kt, cache_vt, page_table, _sb_bias_rows(lp['sb_bias'], seq, PAGE_SIZE),
                    layer, nb, seq)
    per = RW_ROWS // seq
    pe = jnp.repeat(state_shift, seq, axis=0).reshape(nb // per, RW_ROWS, RW_COLS)
    yc, s_fin = _rwkv(h_rw.reshape(nb // per, RW_ROWS, RW_COLS), pe, _block_diag_states(state_wkv), lp['rw_mu'],
                      lp['rw_par'], lp['rw_w2'], lp['rw_a2'], lp['rw_g2'], n_prob=2, nb=per, c=seq)
    wm, sgu_bias = _sgu_tables(lp, seq)
    e0 = jnp.repeat(state_conv[:, 0], seq, axis=0)
    e1 = jnp.repeat(state_conv[:, 1], seq, axis=0)
    x1, z_all, sgu_v = _merge(x, ya, yc.reshape(t, MIX_W), h_conv, e0, e1, h_sgu, lp['conv_w'], lp['sgu_ln'], wm,
                              sgu_bias, lp['w_gate'], lp['b_gate'], lp['w_branch'], lp['w_o'], lp['ln1'],
                              tm=256, seg=seq)
    x3 = _tail(x1, lp, mem_k, mem_v, layer, 512, seq, 4, 1)
    conv_new = z_all.reshape(nb, seq, MIX_W)[:, seq - 2:]
    shift_new = h_rw.reshape(nb, seq, RW_COLS)[:, -1]
    return (x3, k_new.reshape(nb, seq, N_HEADS, HEAD_W), v_new.reshape(nb, seq, N_HEADS, HEAD_W), conv_new,
            shift_new, _diag_states(s_fin), sgu_v.reshape(nb, seq, MIX_W))


def kernel(x_prompt, x_sample, mem_prompt, cache_k, cache_v, page_table, cache_mem_k, cache_mem_v, state_conv,
           state_wkv, state_shift, w_in, sb_bias, w_gate, b_gate, w_branch, w_o, conv_w, rw_mu, rw_w0, rw_w2, rw_a0,
           rw_a2, rw_g2, rw_kk, rw_ka, rw_rk, rw_gn_g, rw_gn_b, sgu_ln_g, sgu_ln_b, sgu_ws, sgu_b, w_mq, w_mk, w_mv,
           w_mo, w_up, w_down, ln1_g, ln1_b, ln2_g, ln2_b, ln3_g, ln3_b):
    n_p, seq_p, _ = x_prompt.shape
    n_s, seq_s, _ = x_sample.shape
    n_phys = cache_k.shape[1]
    xp = x_prompt.reshape(n_p * seq_p, D_MODEL)
    xs = x_sample.reshape(n_s * seq_s, D_MODEL)
    mem2d = mem_prompt.reshape(n_p * N_MEM, D_MODEL)
    cache_kt = jnp.transpose(cache_k, (0, 1, 3, 4, 2)).reshape(DEPTH, n_phys, MIX_W, PAGE_SIZE)
    cache_vt = jnp.transpose(cache_v, (0, 1, 3, 4, 2)).reshape(DEPTH, n_phys, MIX_W, PAGE_SIZE)
    outs = [[] for _ in range(13)]
    for l in range(DEPTH):
        lp = _layer_params(l, w_in, sb_bias, w_gate, b_gate, w_branch, w_o, conv_w, rw_mu, rw_w0, rw_w2, rw_a0,
                           rw_a2, rw_g2, rw_kk, rw_ka, rw_rk, rw_gn_g, rw_gn_b, sgu_ln_g, sgu_ln_b, sgu_ws, sgu_b,
                           w_mq, w_mk, w_mv, w_mo, w_up, w_down, ln1_g, ln1_b, ln2_g, ln2_b, ln3_g, ln3_b)
        mk, mv = _matmul2(mem2d, lp['w_mk'], lp['w_mv'], F32, 512)
        mk4 = mk.reshape(n_p, N_MEM, X_HEADS, X_HD)
        mv4 = mv.reshape(n_p, N_MEM, X_HEADS, X_HD)
        xp, pk, pv, pc, psh, pst = _layer_prompt(xp, lp, mk4[None], mv4[None], n_p, seq_p)
        xs, sk, sv, sc, ssh, sst, scv = _layer_sample(
            xs, lp, l, cache_mem_k, cache_mem_v, cache_kt, cache_vt, page_table,
            state_conv[l], state_shift[l], state_wkv[l], n_s, seq_s)
        for lst, val in zip(outs, (pk, pv, mk4, mv4, pc, pst, psh, sk, sv, sc, sst, ssh, scv)):
            lst.append(val)
    return (xp.reshape(n_p, seq_p, D_MODEL), xs.reshape(n_s, seq_s, D_MODEL)) + tuple(jnp.stack(o) for o in outs)
```

```python
import functools
import math

import jax
import jax.numpy as jnp
from jax import lax
from jax.experimental import pallas as pl
from jax.experimental.pallas import tpu as pltpu

F32 = jnp.float32
BF16 = jnp.bfloat16

D_MODEL = 1024
DEPTH = 2
MIX_W = 256
HEAD_W = 64
N_HEADS = MIX_W // HEAD_W
RW_COLS = 896
PAGE_SIZE = 128
N_MEM = 256
X_HEADS = 4
X_HD = D_MODEL // X_HEADS
D_FF = 4 * D_MODEL
ALPHA = (2 * DEPTH) ** 0.25
LN_EPS = 1e-5
RW_GN_EPS = 64e-5
SGU_CHUNK = 128
RW_ROWS = 64
VMEM_LIMIT = 56 * 1024 * 1024


def _cparams(*sem):
    return pltpu.CompilerParams(dimension_semantics=sem, vmem_limit_bytes=VMEM_LIMIT)


def _const_spec(shape):
    nd = len(shape)
    return pl.BlockSpec(shape, lambda *_: (0,) * nd, pipeline_mode=pl.Buffered(1))


def _bdot(a, b):
    return jnp.dot(a.astype(BF16), b.astype(BF16), preferred_element_type=F32)


def _bdot_nt(a, b):
    return lax.dot_general(a.astype(BF16), b.astype(BF16), (((1,), (1,)), ((), ())),
                           preferred_element_type=F32)


def _bdot_tn(a, b):
    return lax.dot_general(a.astype(BF16), b.astype(BF16), (((0,), (0,)), ((), ())),
                           preferred_element_type=F32)


def _split2(x):
    hi = x.astype(BF16)
    lo = (x - hi.astype(F32)).astype(BF16)
    return hi, lo


def _split3(x):
    hi = x.astype(BF16)
    r1 = x - hi.astype(F32)
    mid = r1.astype(BF16)
    lo = (r1 - mid.astype(F32)).astype(BF16)
    return hi, mid, lo


def _sel_dot_l(sel_b, x):
    return sum(jnp.dot(sel_b, part, preferred_element_type=F32) for part in _split3(x))


def _sel_dot_r(x, sel_b):
    return sum(jnp.dot(part, sel_b, preferred_element_type=F32) for part in _split3(x))


def _dot3(a, b):
    ah, al = _split2(a)
    bh, bl = _split2(b)
    d = lambda x, y: jnp.dot(x, y, preferred_element_type=F32)
    return d(ah, bh) + d(ah, bl) + d(al, bh)


def _sigmoid(x):
    return 1.0 / (1.0 + jnp.exp(-x))


def _softplus(x):
    return jnp.maximum(x, 0.0) + jnp.log(1.0 + jnp.exp(-jnp.abs(x)))


def _gelu_tanh(x):
    return 0.5 * x * (1.0 + jnp.tanh(0.7978845608028654 * (x + 0.044715 * (x * x * x))))


def _ln_rows(x, g, b, eps=LN_EPS):
    mu = jnp.mean(x, axis=-1, keepdims=True)
    xc = x - mu
    var = jnp.mean(xc * xc, axis=-1, keepdims=True)
    return xc * lax.rsqrt(var + eps) * g + b


def _iota(shape, dim):
    return lax.broadcasted_iota(jnp.int32, shape, dim)


def _head_block_mask(rows, row_shift):
    return (_iota((rows, MIX_W), 0) >> row_shift) == (_iota((rows, MIX_W), 1) >> 6)


def _stack_heads(x, mask):
    t = jnp.concatenate([x] * N_HEADS, axis=0)
    return jnp.where(mask, t, jnp.zeros_like(t))


def _unstack_heads(x, rows):
    return x[0:rows] + x[rows:2 * rows] + x[2 * rows:3 * rows] + x[3 * rows:4 * rows]


def _inproj_body(x_ref, wq_ref, wc_ref, wr_ref, ws_ref, oq_ref, oc_ref, or_ref, os_ref):
    xb = x_ref[...].astype(BF16)
    oq_ref[...] = jnp.dot(xb, wq_ref[...], preferred_element_type=F32)
    oc_ref[...] = jnp.dot(xb, wc_ref[...], preferred_element_type=F32)
    or_ref[...] = jnp.dot(xb, wr_ref[...], preferred_element_type=F32)
    os_ref[...] = jnp.dot(xb, ws_ref[...], preferred_element_type=F32)


def _inproj(x, wq, wc, wr, ws, tm):
    t = x.shape[0]
    widths = (wq.shape[1], wc.shape[1], wr.shape[1], ws.shape[1])
    return pl.pallas_call(
        _inproj_body,
        grid=(t // tm,),
        in_specs=[pl.BlockSpec((tm, D_MODEL), lambda i: (i, 0))] + [_const_spec(w.shape) for w in (wq, wc, wr, ws)],
        out_specs=[pl.BlockSpec((tm, w), lambda i: (i, 0)) for w in widths],
        out_shape=[jax.ShapeDtypeStruct((t, w), F32) for w in widths],
        compiler_params=_cparams("parallel"),
        name="inproj",
    )(x, wq, wc, wr, ws)


def _suffix_ones(tk):
    return (_iota((tk, tk), 0) >= _iota((tk, tk), 1)).astype(BF16)


def _sb_weights(z, carry, suffix, mask):
    tk = z.shape[1]
    log_stop = -_softplus(z)
    if mask is not None:
        log_stop = jnp.where(mask, log_stop, 0.0)
    hi, lo = _split2(log_stop)
    cs = jnp.dot(hi, suffix, preferred_element_type=F32) + jnp.dot(lo, suffix, preferred_element_type=F32)
    later = carry if tk == 128 else jnp.concatenate([carry] * (tk // 128), axis=1)
    w = jnp.exp(z + cs + later)
    if mask is not None:
        w = jnp.where(mask, w, 0.0)
    total = jnp.sum(log_stop, axis=1, keepdims=True)
    return w, carry + jnp.broadcast_to(total, carry.shape)


def _sb_prompt_body(q_ref, k_ref, v_ref, bias_ref, o_ref, kb, vb, carry_ref, acc_ref, *, tq):
    qi = pl.program_id(1)
    rows = N_HEADS * tq

    @pl.when(qi == 0)
    def _():
        kb[...] = k_ref[...].astype(BF16)
        vb[...] = v_ref[...].astype(BF16)

    hmask = _head_block_mask(rows, int(math.log2(tq)))
    qs = _stack_heads((q_ref[...] * (HEAD_W ** -0.5)).astype(BF16), hmask)
    suffix = _suffix_ones(tq)

    def block(start, carry, acc, mask):
        z = lax.dot_general(qs, kb[pl.ds(start, tq), :], (((1,), (1,)), ((), ())),
                            preferred_element_type=F32) + bias_ref[...]
        w, carry = _sb_weights(z, carry, suffix, mask)
        acc = acc + jnp.dot(w.astype(BF16), vb[pl.ds(start, tq), :], preferred_element_type=F32)
        return carry, acc

    def block_pair(start, carry, acc):
        z2 = lax.dot_general(qs, kb[pl.ds(start, 2 * tq), :], (((1,), (1,)), ((), ())),
                             preferred_element_type=F32)
        bias = bias_ref[...]
        z = jnp.concatenate([z2[:, tq:] + bias, z2[:, :tq] + bias], axis=0)
        log_stop = -_softplus(z)
        hi, lo = _split2(log_stop)
        cs2 = jnp.dot(jnp.concatenate([hi, lo], axis=0), suffix, preferred_element_type=F32)
        cs = cs2[:2 * rows] + cs2[2 * rows:]
        total = jnp.broadcast_to(jnp.sum(log_stop, axis=1, keepdims=True), (2 * rows, 128))
        carry2 = jnp.concatenate([carry, carry + total[:rows]], axis=0)
        later = jnp.concatenate([carry2] * (tq // 128), axis=1)
        w = jnp.exp(z + cs + later).astype(BF16)
        w2 = jnp.concatenate([w[rows:], w[:rows]], axis=1)
        acc = acc + jnp.dot(w2, vb[pl.ds(start, 2 * tq), :], preferred_element_type=F32)
        return carry2[rows:] + total[rows:], acc

    t_idx = _iota((rows, tq), 0) & (tq - 1)
    causal = _iota((rows, tq), 1) < t_idx
    carry, acc = block(pl.multiple_of(qi * tq, tq), jnp.zeros((rows, 128), F32),
                       jnp.zeros((rows, MIX_W), F32), causal)
    carry_ref[...] = carry
    acc_ref[...] = acc

    @pl.when((qi & 1) == 1)
    def _():
        c, a = block(pl.multiple_of((qi - 1) * tq, tq), carry_ref[...], acc_ref[...], None)
        carry_ref[...] = c
        acc_ref[...] = a

    n_pairs = qi >> 1

    def step(jj, _):
        c, a = block_pair(pl.multiple_of((n_pairs - 1 - jj) * (2 * tq), 2 * tq), carry_ref[...], acc_ref[...])
        carry_ref[...] = c
        acc_ref[...] = a
        return 0

    lax.fori_loop(0, n_pairs, step, 0)
    acc = jnp.where(hmask, acc_ref[...], 0.0)
    o_ref[...] = _unstack_heads(acc, tq)


def _sb_prompt(h_qkv, bias_rows, n, seq, tq):
    nq = seq // tq
    rows = N_HEADS * tq
    return pl.pallas_call(
        functools.partial(_sb_prompt_body, tq=tq),
        grid=(n, nq),
        in_specs=[
            pl.BlockSpec((tq, MIX_W), lambda b, i: (b * nq + i, 0)),
            pl.BlockSpec((seq, MIX_W), lambda b, i: (b, 1)),
            pl.BlockSpec((seq, MIX_W), lambda b, i: (b, 2)),
            _const_spec((rows, tq)),
        ],
        out_specs=pl.BlockSpec((tq, MIX_W), lambda b, i: (b * nq + i, 0)),
        out_shape=jax.ShapeDtypeStruct((n * seq, MIX_W), F32),
        scratch_shapes=[pltpu.VMEM((seq, MIX_W), BF16), pltpu.VMEM((seq, MIX_W), BF16),
                        pltpu.VMEM((rows, 128), F32), pltpu.VMEM((rows, MIX_W), F32)],
        compiler_params=_cparams("parallel", "arbitrary"),
        name="sb_prompt",
    )(h_qkv, h_qkv, h_qkv, bias_rows)


SB_SEQ_SLOTS = 3


def _sb_sample_body(pt_ref, q_ref, kn_ref, vn_ref, ck_ref, cv_ref, bias_ref, o_ref, kbuf, vbuf, sem,
                    *, layer, tq, n_pages):
    b = pl.program_id(0)
    rows = N_HEADS * tq
    n_blk = n_pages + 1

    def seq_copies(s):
        slot = lax.rem(s, SB_SEQ_SLOTS)
        out = []
        for j in range(n_pages):
            page = pt_ref[s, n_pages - 1 - j]
            out.append(pltpu.make_async_copy(ck_ref.at[layer, page], kbuf.at[slot, j], sem.at[slot]))
            out.append(pltpu.make_async_copy(cv_ref.at[layer, page], vbuf.at[slot, j], sem.at[slot]))
        return slot, out

    @pl.when(b == 0)
    def _():
        for s in range(SB_SEQ_SLOTS - 1):
            for cp in seq_copies(jnp.int32(s))[1]:
                cp.start()

    @pl.when(b + (SB_SEQ_SLOTS - 1) < pl.num_programs(0))
    def _():
        for cp in seq_copies(b + (SB_SEQ_SLOTS - 1))[1]:
            cp.start()

    slot, copies = seq_copies(b)
    for cp in copies:
        cp.wait()

    hmask = _head_block_mask(rows, int(math.log2(tq)))
    qs = _stack_heads(q_ref[...] * (HEAD_W ** -0.5), hmask).astype(BF16)
    bias = bias_ref[...]

    pad = jnp.zeros((PAGE_SIZE - tq, MIX_W), F32)
    k_new = jnp.concatenate([kn_ref[...], pad], axis=0).astype(BF16)
    v_new = jnp.concatenate([vn_ref[...], pad], axis=0).astype(BF16)
    z_new = lax.dot_general(qs, k_new, (((1,), (1,)), ((), ())), preferred_element_type=F32)
    k_pages = jnp.concatenate([kbuf[slot, j].astype(BF16) for j in range(n_pages)], axis=1)
    z_pages = jnp.dot(qs, k_pages, preferred_element_type=F32)
    z = jnp.concatenate([z_new] + [z_pages[:, j * PAGE_SIZE:(j + 1) * PAGE_SIZE] for j in range(n_pages)], axis=0)
    z = z + jnp.concatenate([bias] * n_blk, axis=0)

    r_i = _iota((n_blk * rows, PAGE_SIZE), 0)
    valid = (r_i >= rows) | (_iota((n_blk * rows, PAGE_SIZE), 1) < (r_i & (tq - 1)))
    log_stop = jnp.where(valid, -_softplus(z), 0.0)
    hi, lo = _split2(log_stop)
    cs2 = jnp.dot(jnp.concatenate([hi, lo], axis=0), _suffix_ones(PAGE_SIZE), preferred_element_type=F32)
    cs = cs2[:n_blk * rows] + cs2[n_blk * rows:]
    total = jnp.broadcast_to(jnp.sum(log_stop, axis=1, keepdims=True), (n_blk * rows, PAGE_SIZE))
    carry = [jnp.zeros((rows, PAGE_SIZE), F32)]
    for i in range(n_blk - 1):
        carry.append(carry[-1] + total[i * rows:(i + 1) * rows])
    w = jnp.where(valid, jnp.exp(z + cs + jnp.concatenate(carry, axis=0)), 0.0).astype(BF16)

    acc = jnp.dot(w[:rows], v_new, preferred_element_type=F32)
    w_pages = jnp.concatenate([w[(j + 1) * rows:(j + 2) * rows] for j in range(n_pages)], axis=1)
    v_pages = jnp.concatenate([vbuf[slot, j].astype(BF16) for j in range(n_pages)], axis=1)
    acc = acc + lax.dot_general(w_pages, v_pages, (((1,), (1,)), ((), ())), preferred_element_type=F32)
    o_ref[...] = _unstack_heads(jnp.where(hmask, acc, 0.0), tq)


def _sb_sample(h_qkv, cache_kt, cache_vt, page_table, bias_rows, layer, nb, tq):
    n_pages = page_table.shape[1]
    rows = N_HEADS * tq
    buf = (SB_SEQ_SLOTS, n_pages, MIX_W, PAGE_SIZE)
    grid_spec = pltpu.PrefetchScalarGridSpec(
        num_scalar_prefetch=1,
        grid=(nb,),
        in_specs=[
            pl.BlockSpec((tq, MIX_W), lambda b, pt: (b, 0)),
            pl.BlockSpec((tq, MIX_W), lambda b, pt: (b, 1)),
            pl.BlockSpec((tq, MIX_W), lambda b, pt: (b, 2)),
            pl.BlockSpec(memory_space=pl.ANY),
            pl.BlockSpec(memory_space=pl.ANY),
            pl.BlockSpec((rows, PAGE_SIZE), lambda b, pt: (0, 0)),
        ],
        out_specs=pl.BlockSpec((tq, MIX_W), lambda b, pt: (b, 0)),
        scratch_shapes=[pltpu.VMEM(buf, F32), pltpu.VMEM(buf, F32), pltpu.SemaphoreType.DMA((SB_SEQ_SLOTS,))],
    )
    return pl.pallas_call(
        functools.partial(_sb_sample_body, layer=layer, tq=tq, n_pages=n_pages),
        grid_spec=grid_spec,
        out_shape=jax.ShapeDtypeStruct((nb * tq, MIX_W), F32),
        compiler_params=_cparams("arbitrary"),
        name="sb_sample",
    )(page_table, h_qkv, h_qkv, h_qkv, cache_kt, cache_vt, bias_rows)


def _each(fn, *lists):
    return [fn(*args) for args in zip(*lists)]


def _rwkv_problems(ps, p_prevs, s_lists, mu, par, w2p, a2p, g2p, masks, *, nb, c):
    ones_bd, bd, m_strict, m_incl, eye_cat, l_tri, l_all, eye_s = masks
    w0, a0, kk_s, ka_s, rk, gn_g, gn_b = (par[i:i + 1, :] for i in range(7))
    mm = lambda x, y: jnp.dot(x.astype(BF16), y.astype(BF16), preferred_element_type=F32)
    expand = lambda x: _stack_heads(x.astype(BF16), bd)
    left = lambda x: x[:, :MIX_W]
    right = lambda x: x[:, MIX_W:]

    xs = _each(lambda p, pp: p + (pp - p) * mu, ps, p_prevs)
    r = [x[:, 0:256] for x in xs]
    k = [x[:, 256:512] for x in xs]
    v = [x[:, 512:768] for x in xs]
    lora = [x[:, 768:896] for x in xs]
    u = _each(lambda x: w0 + mm(jnp.tanh(x), w2p), lora)
    logw = [(-math.exp(-0.5)) * _sigmoid(x) for x in u]
    asig = _each(lambda x: _sigmoid(a0 + mm(x, a2p)), lora)
    gate = _each(lambda x: mm(_sigmoid(x), g2p), lora)
    kk = [x * kk_s for x in k]
    ssq = _each(lambda x: _sel_dot_r(x * x, ones_bd), kk)
    kk = _each(lambda x, q: x / jnp.maximum(jnp.sqrt(q), 1e-12), kk, ssq)
    k_eff = _each(lambda x, s: x * (1.0 + (s - 1.0) * ka_s), k, asig)
    rk_sum = _each(lambda x, y: _sel_dot_r(x * y * rk, ones_bd), r, k_eff)
    bonus = _each(lambda x, y: x * y, rk_sum, v)
    b = _each(lambda x, s: x * s, kk, asig)

    cum = _each(lambda x: _sel_dot_l(l_tri, x), logw)
    cum_all = _each(lambda x: _sel_dot_l(l_all, x), logw)
    rt = _each(lambda x, q: (x * jnp.exp(q)).astype(BF16), r, cum)
    at = _each(lambda x, q, lw: (-x * jnp.exp(q - lw)).astype(BF16), kk, cum, logw)
    ginv = [jnp.exp(-q) for q in cum]
    bt = _each(lambda x, g: x * g, b, ginv)
    kt = _each(lambda x, g: x * g, k_eff, ginv)
    e_tail = _each(lambda qa, q: jnp.exp(qa - q), cum_all, cum)
    bg = _each(lambda x, e: x * e, b, e_tail)
    kg = _each(lambda x, e: x * e, k_eff, e_tail)
    g_end = [jnp.exp(q) for q in cum_all]

    lhs = _each(lambda x, y: jnp.concatenate([x, y], axis=0), at, rt)
    rhs = _each(lambda x, y: jnp.concatenate([expand(x), expand(y)], axis=0), bt, kt)
    a_cat = _each(lambda x, y: lax.dot_general(x, y, (((1,), (1,)), ((), ())), preferred_element_type=F32),
                  lhs, rhs)
    a_ab = [jnp.where(m_strict, left(x[:RW_ROWS]), 0.0) for x in a_cat]
    a_ak = [jnp.where(m_strict, right(x[:RW_ROWS]), 0.0) for x in a_cat]
    a_rb = [jnp.where(m_incl, left(x[RW_ROWS:]), 0.0) for x in a_cat]
    a_rk = [jnp.where(m_incl, right(x[RW_ROWS:]), 0.0) for x in a_cat]

    inv = [eye_cat + x for x in a_ab]
    apow = a_ab
    apow_bd = [expand(x) for x in apow]
    span = 1
    while 2 * span < c:
        apow = _each(mm, apow, apow_bd)
        apow_bd = [expand(x) for x in apow]
        inv = _each(lambda x, y: x + mm(x, y), inv, apow_bd)
        span *= 2

    sv = [expand(x) for x in v]
    w1 = _each(mm, a_ak, sv)
    uu = _each(lambda i, x, y: mm(i, jnp.concatenate([expand(x), expand(y)], axis=1)), inv, at, w1)
    ua = [left(x) for x in uu]
    uv = [right(x) for x in uu]
    qy = _each(lambda m, x, y: mm(m, jnp.concatenate([expand(x), expand(y)], axis=1)), a_rb, ua, uv)
    y0 = _each(lambda q, m, x: right(q) + mm(m, x), qy, a_rk, sv)
    qe = _each(lambda x, q: x.astype(F32) + left(q), rt, qy)

    row_seq = _iota((RW_ROWS, MIX_W), 0) >> int(math.log2(c))
    y_parts = [[] for _ in ps]
    s_new = [[] for _ in ps]
    for s in range(nb):
        rows = slice(s * c, (s + 1) * c)
        own = lambda x: x if nb == 1 else jnp.where(row_seq == s, x, 0.0)
        y_s = _each(lambda q, st, y: _bdot_nt(q[rows], st[s]) + y[rows], qe, s_lists, y0)
        bg_s = [own(x) for x in bg]
        kg_s = [own(x) for x in kg]
        m_c = _each(lambda x, y, g: jnp.where(bd, _bdot_tn(x, y), 0.0)
                    + jnp.where(eye_s, g[s * c:s * c + 1, :], 0.0), ua, bg_s, g_end)
        n_c = _each(lambda x, y, z, w: jnp.where(bd, _bdot_tn(x, y) + _bdot_tn(z, w), 0.0), uv, bg_s, v, kg_s)
        s_s = _each(lambda st, m, n: _dot3(st[s], m) + n, s_lists, m_c, n_c)
        for i in range(len(ps)):
            y_parts[i].append(y_s[i])
            s_new[i].append(s_s[i])
    y = [parts[0] if nb == 1 else jnp.concatenate(parts, axis=0) for parts in y_parts]

    mean = [_sel_dot_r(x, ones_bd) * (1.0 / HEAD_W) for x in y]
    d = _each(lambda x, m: x - m, y, mean)
    var = [_sel_dot_r(x * x, ones_bd) * (1.0 / HEAD_W) for x in d]
    out = _each(lambda x, q, bo, g: (x * lax.rsqrt(q + RW_GN_EPS) * gn_g + gn_b + bo) * g, d, var, bonus, gate)
    return out, s_new


def _rwkv_body(p_ref, pe_ref, s0_ref, mu_ref, par_ref, w2_ref, a2_ref, g2_ref, y_ref, so_ref, s_scr, plast_scr,
               *, n_prob, nb, c):
    ci = pl.program_id(1)
    g_rows = RW_ROWS
    r4 = N_HEADS * g_rows
    log_c = int(math.log2(c))

    @pl.when(ci == 0)
    def _():
        s_scr[...] = s0_ref[...]
        if nb == 1:
            plast_scr[...] = pe_ref[...]

    bd = _head_block_mask(r4, 6)
    ones_bd = bd.astype(BF16)
    t_i = _iota((g_rows, r4), 0)
    s_i = _iota((g_rows, r4), 1) & (g_rows - 1)
    same = (t_i >> log_c) == (s_i >> log_c)
    m_strict = same & (s_i < t_i)
    m_incl = same & (s_i <= t_i)
    eye_cat = (s_i == t_i).astype(F32)
    ig = _iota((g_rows, g_rows), 0)
    jg = _iota((g_rows, g_rows), 1)
    same_g = (ig >> log_c) == (jg >> log_c)
    l_tri = (same_g & (jg <= ig)).astype(BF16)
    l_all = same_g.astype(BF16)
    eye_s = _iota((MIX_W, MIX_W), 0) == _iota((MIX_W, MIX_W), 1)
    masks = (ones_bd, bd, m_strict, m_incl, eye_cat, l_tri, l_all, eye_s)

    row = _iota((g_rows, RW_COLS), 0)
    ps = [p_ref[pi] for pi in range(n_prob)]
    rolled = [pltpu.roll(p, 1, 0) for p in ps]
    if nb == 1:
        p_prevs = [jnp.where(row == 0, plast_scr[pi][7:8, :], rolled[pi]) for pi in range(n_prob)]
        for pi in range(n_prob):
            plast_scr[pi] = ps[pi][g_rows - 8:g_rows, :]
    else:
        p_prevs = [jnp.where((row & (c - 1)) == 0, pe_ref[pi], rolled[pi]) for pi in range(n_prob)]
    s_lists = [[s_scr[pi * nb + s] for s in range(nb)] for pi in range(n_prob)]
    ys, s_new = _rwkv_problems(ps, p_prevs, s_lists, mu_ref[...], par_ref[...], w2_ref[...], a2_ref[...],
                               g2_ref[...], masks, nb=nb, c=c)
    for pi in range(n_prob):
        y_ref[pi] = ys[pi]
        for s in range(nb):
            s_scr[pi * nb + s] = s_new[pi][s]
    so_ref[...] = s_scr[...]


def _rwkv(p3, pe, s0, mu, par, w2p, a2p, g2p, n_prob, nb, c):
    n_grp, lt, _ = p3.shape
    pe_rows = pe.shape[1]
    n_state = n_prob * nb
    return pl.pallas_call(
        functools.partial(_rwkv_body, n_prob=n_prob, nb=nb, c=c),
        grid=(n_grp // n_prob, lt // RW_ROWS),
        in_specs=[
            pl.BlockSpec((n_prob, RW_ROWS, RW_COLS), lambda i, j: (i, j, 0)),
            pl.BlockSpec((n_prob, pe_rows, RW_COLS), lambda i, j: (i, 0, 0)),
            pl.BlockSpec((n_state, MIX_W, MIX_W), lambda i, j: (i, 0, 0)),
            _const_spec(mu.shape), _const_spec(par.shape),
            _const_spec(w2p.shape), _const_spec(a2p.shape), _const_spec(g2p.shape),
        ],
        out_specs=[
            pl.BlockSpec((n_prob, RW_ROWS, MIX_W), lambda i, j: (i, j, 0)),
            pl.BlockSpec((n_state, MIX_W, MIX_W), lambda i, j: (i, 0, 0)),
        ],
        out_shape=[jax.ShapeDtypeStruct((n_grp, lt, MIX_W), F32),
                   jax.ShapeDtypeStruct((n_grp * nb, MIX_W, MIX_W), F32)],
        scratch_shapes=[pltpu.VMEM((n_state, MIX_W, MIX_W), F32), pltpu.VMEM((n_prob, 8, RW_COLS), F32)],
        compiler_params=_cparams("parallel", "arbitrary"),
        name="rwkv7",
    )(p3, pe, s0, mu, par, w2p, a2p, g2p)


def _merge_body(x_ref, ya_ref, yc_ref, hc_ref, cp_ref, e0_ref, e1_ref, hs_ref, cw_ref, sln_ref, wm_ref, sb_ref,
                wg_ref, bg_ref, wb_ref, wo_ref, ln_ref, o_ref, z_ref, sv_ref, *, tm, seg, tiles_per_seq):
    i = pl.program_id(0)
    x = x_ref[...]
    hc = hc_ref[...]
    gb = hc[:, 0:MIX_W]
    z = hc[:, MIX_W:2 * MIX_W] * hc[:, 2 * MIX_W:3 * MIX_W]
    row = _iota((tm, MIX_W), 0)
    z1 = pltpu.roll(z, 1, 0)
    z2 = pltpu.roll(z, 2, 0)
    if seg >= tm:
        cp = cp_ref[...]
        zp = cp[:, MIX_W:2 * MIX_W] * cp[:, 2 * MIX_W:3 * MIX_W]
        zp = jnp.where(i % tiles_per_seq == 0, jnp.zeros_like(zp), zp)
        e1 = zp[7:8, :]
        e0 = zp[6:7, :]
        pos = row
    else:
        e1 = e1_ref[...]
        e0 = e0_ref[...]
        pos = row & (seg - 1)
    z1 = jnp.where(pos == 0, e1, z1)
    z2 = jnp.where(pos == 0, e0, jnp.where(pos == 1, e1, z2))
    cw = cw_ref[...]
    yb = gb * (z2 * cw[0:1, :] + z1 * cw[1:2, :] + z * cw[2:3, :])
    z_ref[...] = z[tm - z_ref.shape[0]:, :]

    hs = _gelu_tanh(hs_ref[...])
    u = hs[:, 0:MIX_W]
    sln = sln_ref[...]
    sv = _ln_rows(hs[:, MIX_W:], sln[0:1, :], sln[1:2, :])
    sv_ref[...] = sv
    t_i = _iota((SGU_CHUNK, N_HEADS * SGU_CHUNK), 0)
    s_i = _iota((SGU_CHUNK, N_HEADS * SGU_CHUNK), 1) & (SGU_CHUNK - 1)
    log_seg = int(math.log2(min(seg, SGU_CHUNK)))
    keep = (s_i <= t_i) & ((s_i >> log_seg) == (t_i >> log_seg))
    wm = jnp.where(keep, wm_ref[...], 0.0).astype(BF16)
    gmask = (_iota((N_HEADS * SGU_CHUNK, MIX_W), 0) >> 7) == (_iota((N_HEADS * SGU_CHUNK, MIX_W), 1) >> 6)
    sbias = sb_ref[...]
    svb = sv.astype(BF16)
    yd_parts = []
    for ck in range(tm // SGU_CHUNK):
        v_c = svb[ck * SGU_CHUNK:(ck + 1) * SGU_CHUNK]
        mixed = jnp.dot(wm, _stack_heads(v_c, gmask), preferred_element_type=F32) + sbias
        yd_parts.append(u[ck * SGU_CHUNK:(ck + 1) * SGU_CHUNK] * mixed)
    yd = jnp.concatenate(yd_parts, axis=0)

    xb = x.astype(BF16)
    branches = (ya_ref[...], yb, yc_ref[...], yd)
    mix = jnp.zeros((tm, D_MODEL), F32)
    for br in range(4):
        gate = _sigmoid(jnp.dot(xb, wg_ref[:, br * D_MODEL:(br + 1) * D_MODEL], preferred_element_type=F32)
                        + bg_ref[:, br * D_MODEL:(br + 1) * D_MODEL])
        proj = jnp.dot(branches[br].astype(BF16), wb_ref[br], preferred_element_type=F32)
        mix = mix + gate * proj
    ln = ln_ref[...]
    o_ref[...] = _ln_rows(ALPHA * x + jnp.dot(mix.astype(BF16), wo_ref[...], preferred_element_type=F32),
                          ln[0:1, :], ln[1:2, :])


def _merge(x, ya, yc, h_conv, e0, e1, h_sgu, conv_w, sgu_ln, wm, sgu_bias, wg, bgate, wb, wo, ln1, tm, seg):
    t = x.shape[0]
    prompt = seg >= tm
    tiles_per_seq = max(seg // tm, 1)
    z_rows = 8 if prompt else tm
    n_seq = t // seg if prompt else 0
    if prompt:
        cp_spec = pl.BlockSpec((8, 3 * MIX_W), lambda i: (jnp.maximum(i * (tm // 8) - 1, 0), 0))
        e_spec = _const_spec(e0.shape)
        z_spec = pl.BlockSpec((8, MIX_W), lambda i: (i // tiles_per_seq, 0))
        z_shape = jax.ShapeDtypeStruct((n_seq * 8, MIX_W), F32)
    else:
        cp_spec = pl.BlockSpec((8, 3 * MIX_W), lambda i: (0, 0))
        e_spec = pl.BlockSpec((tm, MIX_W), lambda i: (i, 0))
        z_spec = pl.BlockSpec((tm, MIX_W), lambda i: (i, 0))
        z_shape = jax.ShapeDtypeStruct((t, MIX_W), F32)
    row = lambda w: pl.BlockSpec((tm, w), lambda i: (i, 0))
    return pl.pallas_call(
        functools.partial(_merge_body, tm=tm, seg=seg, tiles_per_seq=tiles_per_seq),
        grid=(t // tm,),
        in_specs=[row(D_MODEL), row(MIX_W), row(MIX_W), row(3 * MIX_W), cp_spec, e_spec, e_spec, row(2 * MIX_W),
                  _const_spec(conv_w.shape), _const_spec(sgu_ln.shape), _const_spec(wm.shape),
                  _const_spec(sgu_bias.shape), _const_spec(wg.shape), _const_spec(bgate.shape),
                  _const_spec(wb.shape), _const_spec(wo.shape), _const_spec(ln1.shape)],
        out_specs=[row(D_MODEL), z_spec, row(MIX_W)],
        out_shape=[jax.ShapeDtypeStruct((t, D_MODEL), F32), z_shape, jax.ShapeDtypeStruct((t, MIX_W), F32)],
        compiler_params=_cparams("arbitrary"),
        name="merge",
    )(x, ya, yc, h_conv, h_conv, e0, e1, h_sgu, conv_w, sgu_ln, wm, sgu_bias, wg, bgate, wb, wo, ln1)


def _matmul2_body(x_ref, w1_ref, w2_ref, o1_ref, o2_ref):
    xb = x_ref[...].astype(BF16)
    o1_ref[...] = jnp.dot(xb, w1_ref[...], preferred_element_type=F32).astype(o1_ref.dtype)
    o2_ref[...] = jnp.dot(xb, w2_ref[...], preferred_element_type=F32).astype(o2_ref.dtype)


def _matmul2(x, w1, w2, out_dtype, tm):
    t, kdim = x.shape
    n1, n2 = w1.shape[1], w2.shape[1]
    return pl.pallas_call(
        _matmul2_body,
        grid=(t // tm,),
        in_specs=[pl.BlockSpec((tm, kdim), lambda i: (i, 0)), _const_spec(w1.shape), _const_spec(w2.shape)],
        out_specs=[pl.BlockSpec((tm, n1), lambda i: (i, 0)), pl.BlockSpec((tm, n2), lambda i: (i, 0))],
        out_shape=[jax.ShapeDtypeStruct((t, n1), out_dtype), jax.ShapeDtypeStruct((t, n2), out_dtype)],
        compiler_params=_cparams("parallel"),
        name="matmul2",
    )(x, w1, w2)


def _matmul_body(x_ref, w_ref, o_ref):
    o_ref[...] = jnp.dot(x_ref[...].astype(BF16), w_ref[...], preferred_element_type=F32).astype(o_ref.dtype)


def _matmul(x, w, out_dtype, tm):
    t, kdim = x.shape
    n = w.shape[1]
    return pl.pallas_call(
        _matmul_body,
        grid=(t // tm,),
        in_specs=[pl.BlockSpec((tm, kdim), lambda i: (i, 0)), _const_spec(w.shape)],
        out_specs=pl.BlockSpec((tm, n), lambda i: (i, 0)),
        out_shape=jax.ShapeDtypeStruct((t, n), out_dtype),
        compiler_params=_cparams("parallel"),
        name="matmul",
    )(x, w)


def _softmax_rows(sc):
    m = jnp.max(sc, axis=-1, keepdims=True)
    e = jnp.exp(sc - m)
    return e / jnp.sum(e, axis=-1, keepdims=True)


def _xattn_body(q_ref, k_ref, v_ref, o_ref):
    q = q_ref[...]
    kb = k_ref[...].astype(BF16)
    vb = v_ref[...].astype(BF16)
    for h in range(X_HEADS):
        cols = slice(h * X_HD, (h + 1) * X_HD)
        sc = lax.dot_general(q[:, cols].astype(BF16), kb[:, cols], (((1,), (1,)), ((), ())),
                             preferred_element_type=F32) * (X_HD ** -0.5)
        o = jnp.dot(_softmax_rows(sc).astype(BF16), vb[:, cols], preferred_element_type=F32)
        o_ref[:, cols] = o.astype(o_ref.dtype)


def _xattn(q, mem_k, mem_v, tq):
    t = q.shape[0]
    tiles_per_mem = t // mem_k.shape[0] // tq
    mem_spec = pl.BlockSpec((None, N_MEM, D_MODEL), lambda i: (i // tiles_per_mem, 0, 0))
    return pl.pallas_call(
        _xattn_body,
        grid=(t // tq,),
        in_specs=[pl.BlockSpec((tq, D_MODEL), lambda i: (i, 0)), mem_spec, mem_spec],
        out_specs=pl.BlockSpec((tq, D_MODEL), lambda i: (i, 0)),
        out_shape=jax.ShapeDtypeStruct((t, D_MODEL), q.dtype),
        compiler_params=_cparams("parallel"),
        name="xattn",
    )(q, mem_k, mem_v)


def _xattn_rows_body(q_ref, k_ref, v_ref, o_ref, *, tq, slots):
    rows = X_HEADS * tq
    n_col = N_MEM * X_HEADS
    own = (_iota((rows, n_col), 1) & (X_HEADS - 1)) == (_iota((rows, n_col), 0) >> int(math.log2(tq)))
    for s in range(slots):
        q = q_ref[s * tq:(s + 1) * tq, :]
        qs = jnp.concatenate([q[:, h * X_HD:(h + 1) * X_HD] for h in range(X_HEADS)], axis=0).astype(BF16)
        k_all = k_ref[s].reshape(n_col, X_HD).astype(BF16)
        v_all = v_ref[s].reshape(n_col, X_HD).astype(BF16)
        sc = lax.dot_general(qs, k_all, (((1,), (1,)), ((), ())), preferred_element_type=F32) * (X_HD ** -0.5)
        pr = _softmax_rows(jnp.where(own, sc, -1e30))
        o = jnp.dot(pr.astype(BF16), v_all, preferred_element_type=F32)
        for h in range(X_HEADS):
            o_ref[s * tq:(s + 1) * tq, h * X_HD:(h + 1) * X_HD] = o[h * tq:(h + 1) * tq].astype(o_ref.dtype)


def _xattn_rows(q, mem_k, mem_v, layer, tq, slots):
    t = q.shape[0]
    rows = tq * slots
    mem_spec = pl.BlockSpec((None, slots, N_MEM, X_HEADS, X_HD), lambda i: (layer, i, 0, 0, 0))
    return pl.pallas_call(
        functools.partial(_xattn_rows_body, tq=tq, slots=slots),
        grid=(t // rows,),
        in_specs=[pl.BlockSpec((rows, D_MODEL), lambda i: (i, 0)), mem_spec, mem_spec],
        out_specs=pl.BlockSpec((rows, D_MODEL), lambda i: (i, 0)),
        out_shape=jax.ShapeDtypeStruct((t, D_MODEL), q.dtype),
        compiler_params=_cparams("parallel"),
        name="xattn_rows",
    )(q, mem_k, mem_v)


def _proj_ln_body(y_ref, w_ref, x_ref, ln_ref, o_ref):
    ln = ln_ref[...]
    acc = jnp.dot(y_ref[...].astype(BF16), w_ref[...], preferred_element_type=F32)
    o_ref[...] = _ln_rows(ALPHA * x_ref[...] + acc, ln[0:1, :], ln[1:2, :])


def _proj_ln(y, w, x, ln, tm):
    t = x.shape[0]
    return pl.pallas_call(
        _proj_ln_body,
        grid=(t // tm,),
        in_specs=[pl.BlockSpec((tm, D_MODEL), lambda i: (i, 0)), _const_spec(w.shape),
                  pl.BlockSpec((tm, D_MODEL), lambda i: (i, 0)), _const_spec(ln.shape)],
        out_specs=pl.BlockSpec((tm, D_MODEL), lambda i: (i, 0)),
        out_shape=jax.ShapeDtypeStruct((t, D_MODEL), F32),
        compiler_params=_cparams("parallel"),
        name="proj_ln",
    )(y, w, x, ln)


def _mlp_body(x_ref, wu_ref, wd_ref, ln_ref, o_ref):
    x = x_ref[...]
    xb = x.astype(BF16)
    acc = jnp.zeros(x.shape, F32)
    for j in range(D_FF // D_MODEL):
        cols = slice(j * D_MODEL, (j + 1) * D_MODEL)
        hid = jnp.maximum(jnp.dot(xb, wu_ref[:, cols], preferred_element_type=F32), 0.0)
        acc = acc + jnp.dot((hid * hid).astype(BF16), wd_ref[cols, :], preferred_element_type=F32)
    ln = ln_ref[...]
    o_ref[...] = _ln_rows(ALPHA * x + acc, ln[0:1, :], ln[1:2, :])


def _mlp(x, wu, wd, ln, tm):
    t = x.shape[0]
    return pl.pallas_call(
        _mlp_body,
        grid=(t // tm,),
        in_specs=[pl.BlockSpec((tm, D_MODEL), lambda i: (i, 0)), _const_spec(wu.shape), _const_spec(wd.shape),
                  _const_spec(ln.shape)],
        out_specs=pl.BlockSpec((tm, D_MODEL), lambda i: (i, 0)),
        out_shape=jax.ShapeDtypeStruct((t, D_MODEL), F32),
        compiler_params=_cparams("parallel"),
        name="mlp",
    )(x, wu, wd, ln)


def _block_diag_states(s):
    n = s.shape[0]
    eye = jnp.eye(N_HEADS, dtype=s.dtype)
    return (s[:, :, :, None, :] * eye[None, :, None, :, None]).reshape(n, MIX_W, MIX_W)


def _diag_states(s):
    n = s.shape[0]
    s5 = s.reshape(n, N_HEADS, HEAD_W, N_HEADS, HEAD_W)
    return jnp.stack([s5[:, h, :, h, :] for h in range(N_HEADS)], axis=1)


def _layer_params(l, w_in, sb_bias, w_gate, b_gate, w_branch, w_o, conv_w, rw_mu, rw_w0, rw_w2, rw_a0, rw_a2, rw_g2,
                  rw_kk, rw_ka, rw_rk, rw_gn_g, rw_gn_b, sgu_ln_g, sgu_ln_b, sgu_ws, sgu_b, w_mq, w_mk, w_mv, w_mo,
                  w_up, w_down, ln1_g, ln1_b, ln2_g, ln2_b, ln3_g, ln3_b):
    wi = w_in[l].astype(BF16)
    off_b, off_c, off_d = 3 * MIX_W, 6 * MIX_W, 6 * MIX_W + RW_COLS
    zpad = lambda w, r0: jnp.zeros((128, MIX_W), F32).at[r0:r0 + w.shape[0]].set(w).astype(BF16)
    par = jnp.zeros((8, MIX_W), F32)
    for i, vec in enumerate((rw_w0[l], rw_a0[l], rw_kk[l], rw_ka[l], rw_rk[l].reshape(MIX_W), rw_gn_g[l], rw_gn_b[l])):
        par = par.at[i].set(vec)
    return dict(
        w_qkv=wi[:, :off_b], w_conv=wi[:, off_b:off_c], w_rw=wi[:, off_c:off_d], w_sgu=wi[:, off_d:],
        sb_bias=sb_bias[l],
        w_gate=w_gate[l].astype(BF16), b_gate=b_gate[l].reshape(1, -1), w_branch=w_branch[l].astype(BF16),
        w_o=w_o[l].astype(BF16), conv_w=jnp.zeros((8, MIX_W), F32).at[:3].set(conv_w[l]),
        rw_mu=rw_mu[l].reshape(1, RW_COLS), rw_par=par,
        rw_w2=zpad(rw_w2[l], 0), rw_a2=zpad(rw_a2[l], 32), rw_g2=zpad(rw_g2[l], 64),
        sgu_ln=jnp.stack([sgu_ln_g[l], sgu_ln_b[l]]), sgu_ws=sgu_ws[l], sgu_b=sgu_b[l],
        w_mq=w_mq[l].astype(BF16), w_mk=w_mk[l].astype(BF16), w_mv=w_mv[l].astype(BF16), w_mo=w_mo[l].astype(BF16),
        w_up=w_up[l].astype(BF16), w_down=w_down[l].astype(BF16),
        ln1=jnp.stack([ln1_g[l], ln1_b[l]]), ln2=jnp.stack([ln2_g[l], ln2_b[l]]), ln3=jnp.stack([ln3_g[l], ln3_b[l]]),
    )


def _sgu_tables(lp, seg):
    ws = lp['sgu_ws']
    sb = lp['sgu_b']
    if seg < SGU_CHUNK:
        reps = SGU_CHUNK // seg
        ws = jnp.tile(ws[:, :seg, :seg], (1, reps, reps))
        sb = jnp.tile(sb[:, :seg], (1, reps))
    wm = jnp.transpose(ws, (1, 0, 2)).reshape(SGU_CHUNK, N_HEADS * SGU_CHUNK)
    bias = jnp.repeat(sb.T, HEAD_W, axis=1)
    return wm, bias


def _sb_bias_rows(bias, tq, width):
    return jnp.broadcast_to(jnp.repeat(bias, tq)[:, None], (N_HEADS * tq, width)).astype(F32)


def _tail(x, lp, attend, q_dtype, tm):
    qm = _matmul(x, lp['w_mq'], q_dtype, tm)
    x = _proj_ln(attend(qm), lp['w_mo'], x, lp['ln2'], tm)
    return _mlp(x, lp['w_up'], lp['w_down'], lp['ln3'], tm)


def _layer_prompt(x, lp, mem_k, mem_v, n, seq):
    t = n * seq
    h_qkv, h_conv, h_rw, h_sgu = _inproj(x, lp['w_qkv'], lp['w_conv'], lp['w_rw'], lp['w_sgu'], 512)
    tq = 256
    ya = _sb_prompt(h_qkv, _sb_bias_rows(lp['sb_bias'], tq, tq), n, seq, tq)
    pe = jnp.zeros((n, 8, RW_COLS), F32)
    s0 = jnp.zeros((n, MIX_W, MIX_W), F32)
    yc, s_fin = _rwkv(h_rw.reshape(n, seq, RW_COLS), pe, s0, lp['rw_mu'], lp['rw_par'], lp['rw_w2'], lp['rw_a2'],
                      lp['rw_g2'], n_prob=n, nb=1, c=RW_ROWS)
    wm, sgu_bias = _sgu_tables(lp, SGU_CHUNK)
    zero_e = jnp.zeros((8, MIX_W), F32)
    x1, z_tail, _ = _merge(x, ya, yc.reshape(t, MIX_W), h_conv, zero_e, zero_e, h_sgu, lp['conv_w'], lp['sgu_ln'],
                           wm, sgu_bias, lp['w_gate'], lp['b_gate'], lp['w_branch'], lp['w_o'], lp['ln1'],
                           tm=256, seg=seq)
    x3 = _tail(x1, lp, lambda qm: _xattn(qm, mem_k, mem_v, 512), BF16, 512)
    k_new = h_qkv[:, MIX_W:2 * MIX_W].reshape(n, seq, N_HEADS, HEAD_W)
    v_new = h_qkv[:, 2 * MIX_W:].reshape(n, seq, N_HEADS, HEAD_W)
    conv_new = z_tail.reshape(n, 8, MIX_W)[:, 6:8]
    shift_new = h_rw.reshape(n, seq, RW_COLS)[:, -1]
    return x3, k_new, v_new, conv_new, shift_new, _diag_states(s_fin)


def _layer_sample(x, lp, layer, mem_k, mem_v, cache_kt, cache_vt, page_table, state_conv, state_shift, state_wkv,
                  nb, seq):
    t = nb * seq
    h_qkv, h_conv, h_rw, h_sgu = _inproj(x, lp['w_qkv'], lp['w_conv'], lp['w_rw'], lp['w_sgu'], 512)
    k_new = h_qkv[:, MIX_W:2 * MIX_W].reshape(nb, seq, MIX_W)
    v_new = h_qkv[:, 2 * MIX_W:].reshape(nb, seq, MIX_W)
    ya = _sb_sample(h_qkv, cache_kt, cache_vt, page_table, _sb_bias_rows(lp['sb_bias'], seq, PAGE_SIZE),
                    layer, nb, seq)
    per = RW_ROWS // seq
    pe = jnp.repeat(state_shift, seq, axis=0).reshape(nb // per, RW_ROWS, RW_COLS)
    yc, s_fin = _rwkv(h_rw.reshape(nb // per, RW_ROWS, RW_COLS), pe, _block_diag_states(state_wkv), lp['rw_mu'],
                      lp['rw_par'], lp['rw_w2'], lp['rw_a2'], lp['rw_g2'], n_prob=2, nb=per, c=seq)
    wm, sgu_bias = _sgu_tables(lp, seq)
    e0 = jnp.repeat(state_conv[:, 0], seq, axis=0)
    e1 = jnp.repeat(state_conv[:, 1], seq, axis=0)
    x1, z_all, sgu_v = _merge(x, ya, yc.reshape(t, MIX_W), h_conv, e0, e1, h_sgu, lp['conv_w'], lp['sgu_ln'], wm,
                              sgu_bias, lp['w_gate'], lp['b_gate'], lp['w_branch'], lp['w_o'], lp['ln1'],
                              tm=256, seg=seq)
    x3 = _tail(x1, lp, lambda qm: _xattn_rows(qm, mem_k, mem_v, layer, seq, 4), F32, 512)
    conv_new = z_all.reshape(nb, seq, MIX_W)[:, seq - 2:]
    shift_new = h_rw.reshape(nb, seq, RW_COLS)[:, -1]
    return (x3, k_new.reshape(nb, seq, N_HEADS, HEAD_W), v_new.reshape(nb, seq, N_HEADS, HEAD_W), conv_new,
            shift_new, _diag_states(s_fin), sgu_v.reshape(nb, seq, MIX_W))


def kernel(x_prompt, x_sample, mem_prompt, cache_k, cache_v, page_table, cache_mem_k, cache_mem_v, state_conv,
           state_wkv, state_shift, w_in, sb_bias, w_gate, b_gate, w_branch, w_o, conv_w, rw_mu, rw_w0, rw_w2, rw_a0,
           rw_a2, rw_g2, rw_kk, rw_ka, rw_rk, rw_gn_g, rw_gn_b, sgu_ln_g, sgu_ln_b, sgu_ws, sgu_b, w_mq, w_mk, w_mv,
           w_mo, w_up, w_down, ln1_g, ln1_b, ln2_g, ln2_b, ln3_g, ln3_b):
    n_p, seq_p, _ = x_prompt.shape
    n_s, seq_s, _ = x_sample.shape
    n_phys = cache_k.shape[1]
    xp = x_prompt.reshape(n_p * seq_p, D_MODEL)
    xs = x_sample.reshape(n_s * seq_s, D_MODEL)
    mem2d = mem_prompt.reshape(n_p * N_MEM, D_MODEL)
    cache_kt = jnp.transpose(cache_k, (0, 1, 3, 4, 2)).reshape(DEPTH, n_phys, MIX_W, PAGE_SIZE)
    cache_vt = jnp.transpose(cache_v, (0, 1, 3, 4, 2)).reshape(DEPTH, n_phys, MIX_W, PAGE_SIZE)
    outs = [[] for _ in range(13)]
    for l in range(DEPTH):
        lp = _layer_params(l, w_in, sb_bias, w_gate, b_gate, w_branch, w_o, conv_w, rw_mu, rw_w0, rw_w2, rw_a0,
                           rw_a2, rw_g2, rw_kk, rw_ka, rw_rk, rw_gn_g, rw_gn_b, sgu_ln_g, sgu_ln_b, sgu_ws, sgu_b,
                           w_mq, w_mk, w_mv, w_mo, w_up, w_down, ln1_g, ln1_b, ln2_g, ln2_b, ln3_g, ln3_b)
        mk, mv = _matmul2(mem2d, lp['w_mk'], lp['w_mv'], F32, 512)
        mk4 = mk.reshape(n_p, N_MEM, X_HEADS, X_HD)
        mv4 = mv.reshape(n_p, N_MEM, X_HEADS, X_HD)
        xp, pk, pv, pc, psh, pst = _layer_prompt(xp, lp, mk.reshape(n_p, N_MEM, D_MODEL),
                                                 mv.reshape(n_p, N_MEM, D_MODEL), n_p, seq_p)
        xs, sk, sv, sc, ssh, sst, scv = _layer_sample(
            xs, lp, l, cache_mem_k, cache_mem_v, cache_kt, cache_vt, page_table,
            state_conv[l], state_shift[l], state_wkv[l], n_s, seq_s)
        for lst, val in zip(outs, (pk, pv, mk4, mv4, pc, pst, psh, sk, sv, sc, sst, ssh, scv)):
            lst.append(val)
    return (xp.reshape(n_p, seq_p, D_MODEL), xs.reshape(n_s, seq_s, D_MODEL)) + tuple(jnp.stack(o) for o in outs)
```

```python
import functools
import math

import jax
import jax.numpy as jnp
from jax import lax
from jax.experimental import pallas as pl
from jax.experimental.pallas import tpu as pltpu

F32 = jnp.float32
BF16 = jnp.bfloat16

D_MODEL = 1024
DEPTH = 2
MIX_W = 256
HEAD_W = 64
N_HEADS = MIX_W // HEAD_W
RW_COLS = 896
PAGE_SIZE = 128
N_MEM = 256
X_HEADS = 4
X_HD = D_MODEL // X_HEADS
D_FF = 4 * D_MODEL
ALPHA = (2 * DEPTH) ** 0.25
LN_EPS = 1e-5
RW_GN_EPS = 64e-5
SGU_CHUNK = 128
RW_ROWS = 64
VMEM_LIMIT = 56 * 1024 * 1024


def _cparams(*sem):
    return pltpu.CompilerParams(dimension_semantics=sem, vmem_limit_bytes=VMEM_LIMIT)


def _const_spec(shape):
    nd = len(shape)
    return pl.BlockSpec(shape, lambda *_: (0,) * nd, pipeline_mode=pl.Buffered(1))


def _bdot(a, b):
    return jnp.dot(a.astype(BF16), b.astype(BF16), preferred_element_type=F32)


def _bdot_nt(a, b):
    return lax.dot_general(a.astype(BF16), b.astype(BF16), (((1,), (1,)), ((), ())),
                           preferred_element_type=F32)


def _bdot_tn(a, b):
    return lax.dot_general(a.astype(BF16), b.astype(BF16), (((0,), (0,)), ((), ())),
                           preferred_element_type=F32)


def _split2(x):
    hi = x.astype(BF16)
    lo = (x - hi.astype(F32)).astype(BF16)
    return hi, lo


def _split3(x):
    hi = x.astype(BF16)
    r1 = x - hi.astype(F32)
    mid = r1.astype(BF16)
    lo = (r1 - mid.astype(F32)).astype(BF16)
    return hi, mid, lo


def _sel_dot_l(sel_b, x):
    return sum(jnp.dot(sel_b, part, preferred_element_type=F32) for part in _split3(x))


def _sel_dot_r(x, sel_b):
    return sum(jnp.dot(part, sel_b, preferred_element_type=F32) for part in _split3(x))


def _dot3(a, b):
    ah, al = _split2(a)
    bh, bl = _split2(b)
    d = lambda x, y: jnp.dot(x, y, preferred_element_type=F32)
    return d(ah, bh) + d(ah, bl) + d(al, bh)


def _sigmoid(x):
    return 1.0 / (1.0 + jnp.exp(-x))


def _softplus(x):
    return jnp.maximum(x, 0.0) + jnp.log(1.0 + jnp.exp(-jnp.abs(x)))


def _gelu_tanh(x):
    return 0.5 * x * (1.0 + jnp.tanh(0.7978845608028654 * (x + 0.044715 * (x * x * x))))


def _ln_rows(x, g, b, eps=LN_EPS):
    mu = jnp.mean(x, axis=-1, keepdims=True)
    xc = x - mu
    var = jnp.mean(xc * xc, axis=-1, keepdims=True)
    return xc * lax.rsqrt(var + eps) * g + b


def _iota(shape, dim):
    return lax.broadcasted_iota(jnp.int32, shape, dim)


def _head_block_mask(rows, row_shift):
    return (_iota((rows, MIX_W), 0) >> row_shift) == (_iota((rows, MIX_W), 1) >> 6)


def _stack_heads(x, mask):
    t = jnp.concatenate([x] * N_HEADS, axis=0)
    return jnp.where(mask, t, jnp.zeros_like(t))


def _unstack_heads(x, rows):
    return x[0:rows] + x[rows:2 * rows] + x[2 * rows:3 * rows] + x[3 * rows:4 * rows]


def _inproj_body(x_ref, wq_ref, wc_ref, wr_ref, ws_ref, oq_ref, oc_ref, or_ref, os_ref):
    xb = x_ref[...].astype(BF16)
    oq_ref[...] = jnp.dot(xb, wq_ref[...], preferred_element_type=F32)
    oc_ref[...] = jnp.dot(xb, wc_ref[...], preferred_element_type=F32)
    or_ref[...] = jnp.dot(xb, wr_ref[...], preferred_element_type=F32)
    os_ref[...] = jnp.dot(xb, ws_ref[...], preferred_element_type=F32)


def _inproj(x, wq, wc, wr, ws, tm):
    t = x.shape[0]
    widths = (wq.shape[1], wc.shape[1], wr.shape[1], ws.shape[1])
    return pl.pallas_call(
        _inproj_body,
        grid=(t // tm,),
        in_specs=[pl.BlockSpec((tm, D_MODEL), lambda i: (i, 0))] + [_const_spec(w.shape) for w in (wq, wc, wr, ws)],
        out_specs=[pl.BlockSpec((tm, w), lambda i: (i, 0)) for w in widths],
        out_shape=[jax.ShapeDtypeStruct((t, w), F32) for w in widths],
        compiler_params=_cparams("parallel"),
        name="inproj",
    )(x, wq, wc, wr, ws)


def _suffix_ones(tk):
    return (_iota((tk, tk), 0) >= _iota((tk, tk), 1)).astype(BF16)


def _sb_weights(z, carry, suffix, mask):
    tk = z.shape[1]
    stop = _softplus(z)
    if mask is not None:
        stop = jnp.where(mask, stop, 0.0)
    cs = jnp.dot(stop.astype(BF16), suffix, preferred_element_type=F32)
    later = carry if tk == 128 else jnp.concatenate([carry] * (tk // 128), axis=1)
    w = jnp.exp(z - cs - later)
    if mask is not None:
        w = jnp.where(mask, w, 0.0)
    total = jnp.sum(stop, axis=1, keepdims=True)
    return w, carry + jnp.broadcast_to(total, carry.shape)


def _sb_prompt_body(q_ref, k_ref, v_ref, bias_ref, o_ref, kb, vb, carry_ref, acc_ref, *, tq):
    qi = pl.program_id(1)
    rows = N_HEADS * tq

    @pl.when(qi == 0)
    def _():
        kb[...] = k_ref[...].astype(BF16)
        vb[...] = v_ref[...].astype(BF16)

    hmask = _head_block_mask(rows, int(math.log2(tq)))
    qs = _stack_heads((q_ref[...] * (HEAD_W ** -0.5)).astype(BF16), hmask)
    suffix = _suffix_ones(tq)

    def block(start, carry, acc, mask):
        z = lax.dot_general(qs, kb[pl.ds(start, tq), :], (((1,), (1,)), ((), ())),
                            preferred_element_type=F32) + bias_ref[...]
        w, carry = _sb_weights(z, carry, suffix, mask)
        acc = acc + jnp.dot(w.astype(BF16), vb[pl.ds(start, tq), :], preferred_element_type=F32)
        return carry, acc

    def block_pair(start):
        k2 = kb[pl.ds(start, 2 * tq), :]
        v2 = vb[pl.ds(start, 2 * tq), :]
        hrows = lambda h: slice(h * tq, (h + 1) * tq)
        logits = lambda h: lax.dot_general(qs[hrows(h)], k2, (((1,), (1,)), ((), ())),
                                           preferred_element_type=F32)
        z2 = [logits(0), logits(1)]
        for h in range(N_HEADS):
            bias = bias_ref[hrows(h), :]
            z = jnp.concatenate([z2[h][:, tq:] + bias, z2[h][:, :tq] + bias], axis=0)
            stop = _softplus(z)
            cs = jnp.dot(stop.astype(BF16), suffix, preferred_element_type=F32)
            if h + 2 < N_HEADS:
                z2.append(logits(h + 2))
            total = jnp.broadcast_to(jnp.sum(stop, axis=1, keepdims=True), (2 * tq, 128))
            carry = carry_ref[hrows(h), :]
            carry2 = jnp.concatenate([carry, carry + total[:tq]], axis=0)
            later = jnp.concatenate([carry2] * (tq // 128), axis=1)
            w = jnp.exp(z - cs - later).astype(BF16)
            w2 = jnp.concatenate([w[tq:], w[:tq]], axis=1)
            acc_ref[hrows(h), :] += jnp.dot(w2, v2, preferred_element_type=F32)
            carry_ref[hrows(h), :] = carry2[tq:] + total[tq:]

    t_idx = _iota((rows, tq), 0) & (tq - 1)
    causal = _iota((rows, tq), 1) < t_idx
    carry, acc = block(pl.multiple_of(qi * tq, tq), jnp.zeros((rows, 128), F32),
                       jnp.zeros((rows, MIX_W), F32), causal)
    carry_ref[...] = carry
    acc_ref[...] = acc

    @pl.when((qi & 1) == 1)
    def _():
        c, a = block(pl.multiple_of((qi - 1) * tq, tq), carry_ref[...], acc_ref[...], None)
        carry_ref[...] = c
        acc_ref[...] = a

    n_pairs = qi >> 1

    def step(jj, _):
        block_pair(pl.multiple_of((n_pairs - 1 - jj) * (2 * tq), 2 * tq))
        return 0

    lax.fori_loop(0, n_pairs, step, 0)
    acc = jnp.where(hmask, acc_ref[...], 0.0)
    o_ref[...] = _unstack_heads(acc, tq)


def _sb_prompt(h_qkv, bias_rows, n, seq, tq):
    nq = seq // tq
    rows = N_HEADS * tq
    return pl.pallas_call(
        functools.partial(_sb_prompt_body, tq=tq),
        grid=(n, nq),
        in_specs=[
            pl.BlockSpec((tq, MIX_W), lambda b, i: (b * nq + i, 0)),
            pl.BlockSpec((seq, MIX_W), lambda b, i: (b, 1)),
            pl.BlockSpec((seq, MIX_W), lambda b, i: (b, 2)),
            _const_spec((rows, tq)),
        ],
        out_specs=pl.BlockSpec((tq, MIX_W), lambda b, i: (b * nq + i, 0)),
        out_shape=jax.ShapeDtypeStruct((n * seq, MIX_W), F32),
        scratch_shapes=[pltpu.VMEM((seq, MIX_W), BF16), pltpu.VMEM((seq, MIX_W), BF16),
                        pltpu.VMEM((rows, 128), F32), pltpu.VMEM((rows, MIX_W), F32)],
        compiler_params=_cparams("parallel", "arbitrary"),
        name="sb_prompt",
    )(h_qkv, h_qkv, h_qkv, bias_rows)


SB_SEQ_SLOTS = 3


def _sb_sample_body(pt_ref, q_ref, kn_ref, vn_ref, ck_ref, cv_ref, bias_ref, o_ref, kbuf, vbuf, sem,
                    *, layer, tq, n_pages):
    b = pl.program_id(0)
    rows = N_HEADS * tq
    n_blk = n_pages + 1

    def seq_copies(s):
        slot = lax.rem(s, SB_SEQ_SLOTS)
        out = []
        for j in range(n_pages):
            page = pt_ref[s, n_pages - 1 - j]
            out.append(pltpu.make_async_copy(ck_ref.at[layer, page], kbuf.at[slot, j], sem.at[slot]))
            out.append(pltpu.make_async_copy(cv_ref.at[layer, page], vbuf.at[slot, j], sem.at[slot]))
        return slot, out

    @pl.when(b == 0)
    def _():
        for s in range(SB_SEQ_SLOTS - 1):
            for cp in seq_copies(jnp.int32(s))[1]:
                cp.start()

    @pl.when(b + (SB_SEQ_SLOTS - 1) < pl.num_programs(0))
    def _():
        for cp in seq_copies(b + (SB_SEQ_SLOTS - 1))[1]:
            cp.start()

    slot, copies = seq_copies(b)
    for cp in copies:
        cp.wait()

    hmask = _head_block_mask(rows, int(math.log2(tq)))
    qs = _stack_heads(q_ref[...] * (HEAD_W ** -0.5), hmask).astype(BF16)
    bias = bias_ref[...]

    pad = jnp.zeros((PAGE_SIZE - tq, MIX_W), F32)
    k_new = jnp.concatenate([kn_ref[...], pad], axis=0).astype(BF16)
    v_new = jnp.concatenate([vn_ref[...], pad], axis=0).astype(BF16)
    z_new = lax.dot_general(qs, k_new, (((1,), (1,)), ((), ())), preferred_element_type=F32)
    k_pages = jnp.concatenate([kbuf[slot, j].astype(BF16) for j in range(n_pages)], axis=1)
    z_pages = jnp.dot(qs, k_pages, preferred_element_type=F32)
    z = jnp.concatenate([z_new] + [z_pages[:, j * PAGE_SIZE:(j + 1) * PAGE_SIZE] for j in range(n_pages)], axis=0)
    z = z + jnp.concatenate([bias] * n_blk, axis=0)

    r_i = _iota((n_blk * rows, PAGE_SIZE), 0)
    valid = (r_i >= rows) | (_iota((n_blk * rows, PAGE_SIZE), 1) < (r_i & (tq - 1)))
    stop = jnp.where(valid, _softplus(z), 0.0)
    cs = jnp.dot(stop.astype(BF16), _suffix_ones(PAGE_SIZE), preferred_element_type=F32)
    total = jnp.broadcast_to(jnp.sum(stop, axis=1, keepdims=True), (n_blk * rows, PAGE_SIZE))
    carry = [jnp.zeros((rows, PAGE_SIZE), F32)]
    for i in range(n_blk - 1):
        carry.append(carry[-1] + total[i * rows:(i + 1) * rows])
    w = jnp.where(valid, jnp.exp(z - cs - jnp.concatenate(carry, axis=0)), 0.0).astype(BF16)

    acc = jnp.dot(w[:rows], v_new, preferred_element_type=F32)
    w_pages = jnp.concatenate([w[(j + 1) * rows:(j + 2) * rows] for j in range(n_pages)], axis=1)
    v_pages = jnp.concatenate([vbuf[slot, j].astype(BF16) for j in range(n_pages)], axis=1)
    acc = acc + lax.dot_general(w_pages, v_pages, (((1,), (1,)), ((), ())), preferred_element_type=F32)
    o_ref[...] = _unstack_heads(jnp.where(hmask, acc, 0.0), tq)


def _sb_sample(h_qkv, cache_kt, cache_vt, page_table, bias_rows, layer, nb, tq):
    n_pages = page_table.shape[1]
    rows = N_HEADS * tq
    buf = (SB_SEQ_SLOTS, n_pages, MIX_W, PAGE_SIZE)
    grid_spec = pltpu.PrefetchScalarGridSpec(
        num_scalar_prefetch=1,
        grid=(nb,),
        in_specs=[
            pl.BlockSpec((tq, MIX_W), lambda b, pt: (b, 0)),
            pl.BlockSpec((tq, MIX_W), lambda b, pt: (b, 1)),
            pl.BlockSpec((tq, MIX_W), lambda b, pt: (b, 2)),
            pl.BlockSpec(memory_space=pl.ANY),
            pl.BlockSpec(memory_space=pl.ANY),
            pl.BlockSpec((rows, PAGE_SIZE), lambda b, pt: (0, 0)),
        ],
        out_specs=pl.BlockSpec((tq, MIX_W), lambda b, pt: (b, 0)),
        scratch_shapes=[pltpu.VMEM(buf, F32), pltpu.VMEM(buf, F32), pltpu.SemaphoreType.DMA((SB_SEQ_SLOTS,))],
    )
    return pl.pallas_call(
        functools.partial(_sb_sample_body, layer=layer, tq=tq, n_pages=n_pages),
        grid_spec=grid_spec,
        out_shape=jax.ShapeDtypeStruct((nb * tq, MIX_W), F32),
        compiler_params=_cparams("arbitrary"),
        name="sb_sample",
    )(page_table, h_qkv, h_qkv, h_qkv, cache_kt, cache_vt, bias_rows)


def _each(fn, *lists):
    return [fn(*args) for args in zip(*lists)]


def _rwkv_problems(ps, p_prevs, s_lists, mu, par, w2p, a2p, g2p, masks, *, nb, c, chain):
    ones_bd, bd, m_strict, m_incl, eye_cat, l_tri, l_all, eye_s = masks
    w0, a0, kk_s, ka_s, rk, gn_g, gn_b = (par[i:i + 1, :] for i in range(7))
    mm = lambda x, y: jnp.dot(x.astype(BF16), y.astype(BF16), preferred_element_type=F32)
    expand = lambda x: _stack_heads(x.astype(BF16), bd)
    left = lambda x: x[:, :MIX_W]
    right = lambda x: x[:, MIX_W:]

    n_p = len(ps)

    def rows_batched(fn, xs):
        out = fn(xs[0] if n_p == 1 else jnp.concatenate(xs, axis=0))
        return [out[i * RW_ROWS:(i + 1) * RW_ROWS] for i in range(n_p)]

    head_sums = lambda xs: rows_batched(lambda x: _sel_dot_r(x, ones_bd), xs)

    xs = _each(lambda p, pp: p + (pp - p) * mu, ps, p_prevs)
    r = [x[:, 0:256] for x in xs]
    k = [x[:, 256:512] for x in xs]
    v = [x[:, 512:768] for x in xs]
    lora = [x[:, 768:896] for x in xs]
    u = rows_batched(lambda x: w0 + mm(jnp.tanh(x), w2p), lora)
    logw = [(-math.exp(-0.5)) * _sigmoid(x) for x in u]
    asig = rows_batched(lambda x: _sigmoid(a0 + mm(x, a2p)), lora)
    gate = rows_batched(lambda x: mm(_sigmoid(x), g2p), lora)
    kk = [x * kk_s for x in k]
    ssq = head_sums([x * x for x in kk])
    kk = _each(lambda x, q: x / jnp.maximum(jnp.sqrt(q), 1e-12), kk, ssq)
    k_eff = _each(lambda x, s: x * (1.0 + (s - 1.0) * ka_s), k, asig)
    rk_sum = head_sums(_each(lambda x, y: x * y * rk, r, k_eff))
    bonus = _each(lambda x, y: x * y, rk_sum, v)
    b = _each(lambda x, s: x * s, kk, asig)

    sums = _sel_dot_l(jnp.concatenate([l_tri, l_all], axis=0), jnp.concatenate(logw, axis=1))
    cum = [sums[:RW_ROWS, i * MIX_W:(i + 1) * MIX_W] for i in range(n_p)]
    cum_all = [sums[RW_ROWS:, i * MIX_W:(i + 1) * MIX_W] for i in range(n_p)]
    rt = _each(lambda x, q: (x * jnp.exp(q)).astype(BF16), r, cum)
    at = _each(lambda x, q, lw: (-x * jnp.exp(q - lw)).astype(BF16), kk, cum, logw)
    ginv = [jnp.exp(-q) for q in cum]
    bt = _each(lambda x, g: x * g, b, ginv)
    kt = _each(lambda x, g: x * g, k_eff, ginv)
    e_tail = _each(lambda qa, q: jnp.exp(qa - q), cum_all, cum)
    bg = _each(lambda x, e: x * e, b, e_tail)
    kg = _each(lambda x, e: x * e, k_eff, e_tail)
    g_end = [jnp.exp(q) for q in cum_all]

    lhs = _each(lambda x, y: jnp.concatenate([x, y], axis=0), at, rt)
    rhs = _each(lambda x, y: jnp.concatenate([expand(x), expand(y)], axis=0), bt, kt)
    a_cat = _each(lambda x, y: lax.dot_general(x, y, (((1,), (1,)), ((), ())), preferred_element_type=F32),
                  lhs, rhs)
    a_ab = [jnp.where(m_strict, left(x[:RW_ROWS]), 0.0) for x in a_cat]
    a_ak = [jnp.where(m_strict, right(x[:RW_ROWS]), 0.0) for x in a_cat]
    a_rb = [jnp.where(m_incl, left(x[RW_ROWS:]), 0.0) for x in a_cat]
    a_rk = [jnp.where(m_incl, right(x[RW_ROWS:]), 0.0) for x in a_cat]

    inv = [eye_cat + x for x in a_ab]
    apow = a_ab
    apow_bd = [expand(x) for x in apow]
    span = 1
    while 2 * span < c:
        apow = _each(mm, apow, apow_bd)
        apow_bd = [expand(x) for x in apow]
        inv = _each(lambda x, y: x + mm(x, y), inv, apow_bd)
        span *= 2

    sv = [expand(x) for x in v]
    w1 = _each(mm, a_ak, sv)
    uu = _each(lambda i, x, y: mm(i, jnp.concatenate([expand(x), expand(y)], axis=1)), inv, at, w1)
    ua = [left(x) for x in uu]
    uv = [right(x) for x in uu]
    qy = _each(lambda m, x, y: mm(m, jnp.concatenate([expand(x), expand(y)], axis=1)), a_rb, ua, uv)
    y0 = _each(lambda q, m, x: right(q) + mm(m, x), qy, a_rk, sv)
    qe = _each(lambda x, q: x.astype(F32) + left(q), rt, qy)

    row_seq = _iota((RW_ROWS, MIX_W), 0) >> int(math.log2(c))
    m_c, n_c = [], []
    for s in range(nb):
        own = lambda x: x if nb == 1 else jnp.where(row_seq == s, x, 0.0)
        bg_s = [own(x) for x in bg]
        kg_s = [own(x) for x in kg]
        m_c.append(_each(lambda x, y, g: jnp.where(bd, _bdot_tn(x, y), 0.0)
                         + jnp.where(eye_s, g[s * c:s * c + 1, :], 0.0), ua, bg_s, g_end))
        n_c.append(_each(lambda x, y, z, w: jnp.where(bd, _bdot_tn(x, y) + _bdot_tn(z, w), 0.0),
                         uv, bg_s, v, kg_s))

    states = [list(sl) for sl in s_lists]
    y_parts = [[] for _ in ps]
    for kk in range(chain):
        idx = [g * chain + kk for g in range(len(s_lists))]
        pick = lambda xs: [xs[i] for i in idx]
        for s in range(nb):
            rows = slice(s * c, (s + 1) * c)
            cur = [st[s] for st in states]
            y_s = _each(lambda q, st, y: _bdot_nt(q[rows], st) + y[rows], pick(qe), cur, pick(y0))
            nxt = _each(lambda st, m, n: _dot3(st, m) + n, cur, pick(m_c[s]), pick(n_c[s]))
            for gi, i in enumerate(idx):
                y_parts[i].append(y_s[gi])
                states[gi][s] = nxt[gi]
    s_new = states
    y = [parts[0] if nb == 1 else jnp.concatenate(parts, axis=0) for parts in y_parts]

    mean = [x * (1.0 / HEAD_W) for x in head_sums(y)]
    d = _each(lambda x, m: x - m, y, mean)
    var = [x * (1.0 / HEAD_W) for x in head_sums([x * x for x in d])]
    out = _each(lambda x, q, bo, g: (x * lax.rsqrt(q + RW_GN_EPS) * gn_g + gn_b + bo) * g, d, var, bonus, gate)
    return out, s_new


def _rwkv_body(p_ref, pe_ref, s0_ref, mu_ref, par_ref, w2_ref, a2_ref, g2_ref, y_ref, so_ref, s_scr, plast_scr,
               *, n_prob, nb, c, chain):
    ci = pl.program_id(1)
    g_rows = RW_ROWS
    r4 = N_HEADS * g_rows
    log_c = int(math.log2(c))

    @pl.when(ci == 0)
    def _():
        s_scr[...] = s0_ref[...]
        if nb == 1:
            plast_scr[...] = pe_ref[...]

    bd = _head_block_mask(r4, 6)
    ones_bd = bd.astype(BF16)
    t_i = _iota((g_rows, r4), 0)
    s_i = _iota((g_rows, r4), 1) & (g_rows - 1)
    same = (t_i >> log_c) == (s_i >> log_c)
    m_strict = same & (s_i < t_i)
    m_incl = same & (s_i <= t_i)
    eye_cat = (s_i == t_i).astype(F32)
    ig = _iota((g_rows, g_rows), 0)
    jg = _iota((g_rows, g_rows), 1)
    same_g = (ig >> log_c) == (jg >> log_c)
    l_tri = (same_g & (jg <= ig)).astype(BF16)
    l_all = same_g.astype(BF16)
    eye_s = _iota((MIX_W, MIX_W), 0) == _iota((MIX_W, MIX_W), 1)
    masks = (ones_bd, bd, m_strict, m_incl, eye_cat, l_tri, l_all, eye_s)

    blk_rows = chain * g_rows
    row = _iota((blk_rows, RW_COLS), 0)
    blocks = [p_ref[pi] for pi in range(n_prob)]
    rolled = [pltpu.roll(p, 1, 0) for p in blocks]
    if nb == 1:
        prevs = [jnp.where(row == 0, plast_scr[pi][7:8, :], rolled[pi]) for pi in range(n_prob)]
        for pi in range(n_prob):
            plast_scr[pi] = blocks[pi][blk_rows - 8:blk_rows, :]
    else:
        prevs = [jnp.where((row & (c - 1)) == 0, pe_ref[pi], rolled[pi]) for pi in range(n_prob)]
    chunks = lambda xs: [x[kk * g_rows:(kk + 1) * g_rows] for x in xs for kk in range(chain)]
    s_lists = [[s_scr[pi * nb + s] for s in range(nb)] for pi in range(n_prob)]
    ys, s_new = _rwkv_problems(chunks(blocks), chunks(prevs), s_lists, mu_ref[...], par_ref[...], w2_ref[...],
                               a2_ref[...], g2_ref[...], masks, nb=nb, c=c, chain=chain)
    for pi in range(n_prob):
        for kk in range(chain):
            y_ref[pi, kk * g_rows:(kk + 1) * g_rows, :] = ys[pi * chain + kk]
        for s in range(nb):
            s_scr[pi * nb + s] = s_new[pi][s]
    so_ref[...] = s_scr[...]


def _rwkv(p3, pe, s0, mu, par, w2p, a2p, g2p, n_prob, nb, c, chain):
    n_grp, lt, _ = p3.shape
    pe_rows = pe.shape[1]
    n_state = n_prob * nb
    blk_rows = chain * RW_ROWS
    return pl.pallas_call(
        functools.partial(_rwkv_body, n_prob=n_prob, nb=nb, c=c, chain=chain),
        grid=(n_grp // n_prob, lt // blk_rows),
        in_specs=[
            pl.BlockSpec((n_prob, blk_rows, RW_COLS), lambda i, j: (i, j, 0)),
            pl.BlockSpec((n_prob, pe_rows, RW_COLS), lambda i, j: (i, 0, 0)),
            pl.BlockSpec((n_state, MIX_W, MIX_W), lambda i, j: (i, 0, 0)),
            _const_spec(mu.shape), _const_spec(par.shape),
            _const_spec(w2p.shape), _const_spec(a2p.shape), _const_spec(g2p.shape),
        ],
        out_specs=[
            pl.BlockSpec((n_prob, blk_rows, MIX_W), lambda i, j: (i, j, 0)),
            pl.BlockSpec((n_state, MIX_W, MIX_W), lambda i, j: (i, 0, 0)),
        ],
        out_shape=[jax.ShapeDtypeStruct((n_grp, lt, MIX_W), F32),
                   jax.ShapeDtypeStruct((n_grp * nb, MIX_W, MIX_W), F32)],
        scratch_shapes=[pltpu.VMEM((n_state, MIX_W, MIX_W), F32), pltpu.VMEM((n_prob, 8, RW_COLS), F32)],
        compiler_params=_cparams("parallel", "arbitrary"),
        name="rwkv7",
    )(p3, pe, s0, mu, par, w2p, a2p, g2p)


def _merge_body(x_ref, ya_ref, yc_ref, hc_ref, cp_ref, e0_ref, e1_ref, hs_ref, cw_ref, sln_ref, wm_ref, sb_ref,
                wg_ref, bg_ref, wb_ref, wo_ref, ln_ref, o_ref, z_ref, sv_ref, *, tm, seg, tiles_per_seq):
    i = pl.program_id(0)
    x = x_ref[...]
    hc = hc_ref[...]
    gb = hc[:, 0:MIX_W]
    z = hc[:, MIX_W:2 * MIX_W] * hc[:, 2 * MIX_W:3 * MIX_W]
    row = _iota((tm, MIX_W), 0)
    z1 = pltpu.roll(z, 1, 0)
    z2 = pltpu.roll(z, 2, 0)
    if seg >= tm:
        cp = cp_ref[...]
        zp = cp[:, MIX_W:2 * MIX_W] * cp[:, 2 * MIX_W:3 * MIX_W]
        zp = jnp.where(i % tiles_per_seq == 0, jnp.zeros_like(zp), zp)
        e1 = zp[7:8, :]
        e0 = zp[6:7, :]
        pos = row
    else:
        e1 = e1_ref[...]
        e0 = e0_ref[...]
        pos = row & (seg - 1)
    z1 = jnp.where(pos == 0, e1, z1)
    z2 = jnp.where(pos == 0, e0, jnp.where(pos == 1, e1, z2))
    cw = cw_ref[...]
    yb = gb * (z2 * cw[0:1, :] + z1 * cw[1:2, :] + z * cw[2:3, :])
    z_ref[...] = z[tm - z_ref.shape[0]:, :]

    hs = _gelu_tanh(hs_ref[...])
    u = hs[:, 0:MIX_W]
    sln = sln_ref[...]
    sv = _ln_rows(hs[:, MIX_W:], sln[0:1, :], sln[1:2, :])
    sv_ref[...] = sv
    t_i = _iota((SGU_CHUNK, N_HEADS * SGU_CHUNK), 0)
    s_i = _iota((SGU_CHUNK, N_HEADS * SGU_CHUNK), 1) & (SGU_CHUNK - 1)
    log_seg = int(math.log2(min(seg, SGU_CHUNK)))
    keep = (s_i <= t_i) & ((s_i >> log_seg) == (t_i >> log_seg))
    wm = jnp.where(keep, wm_ref[...], 0.0).astype(BF16)
    gmask = (_iota((N_HEADS * SGU_CHUNK, MIX_W), 0) >> 7) == (_iota((N_HEADS * SGU_CHUNK, MIX_W), 1) >> 6)
    sbias = sb_ref[...]
    svb = sv.astype(BF16)
    yd_parts = []
    for ck in range(tm // SGU_CHUNK):
        v_c = svb[ck * SGU_CHUNK:(ck + 1) * SGU_CHUNK]
        mixed = jnp.dot(wm, _stack_heads(v_c, gmask), preferred_element_type=F32) + sbias
        yd_parts.append(u[ck * SGU_CHUNK:(ck + 1) * SGU_CHUNK] * mixed)
    yd = jnp.concatenate(yd_parts, axis=0)

    xb = x.astype(BF16)
    branches = (ya_ref[...], yb, yc_ref[...], yd)
    mix = jnp.zeros((tm, D_MODEL), F32)
    for br in range(4):
        gate = _sigmoid(jnp.dot(xb, wg_ref[:, br * D_MODEL:(br + 1) * D_MODEL], preferred_element_type=F32)
                        + bg_ref[:, br * D_MODEL:(br + 1) * D_MODEL])
        proj = jnp.dot(branches[br].astype(BF16), wb_ref[br], preferred_element_type=F32)
        mix = mix + gate * proj
    ln = ln_ref[...]
    o_ref[...] = _ln_rows(ALPHA * x + jnp.dot(mix.astype(BF16), wo_ref[...], preferred_element_type=F32),
                          ln[0:1, :], ln[1:2, :])


def _merge(x, ya, yc, h_conv, e0, e1, h_sgu, conv_w, sgu_ln, wm, sgu_bias, wg, bgate, wb, wo, ln1, tm, seg):
    t = x.shape[0]
    prompt = seg >= tm
    tiles_per_seq = max(seg // tm, 1)
    z_rows = 8 if prompt else tm
    n_seq = t // seg if prompt else 0
    if prompt:
        cp_spec = pl.BlockSpec((8, 3 * MIX_W), lambda i: (jnp.maximum(i * (tm // 8) - 1, 0), 0))
        e_spec = _const_spec(e0.shape)
        z_spec = pl.BlockSpec((8, MIX_W), lambda i: (i // tiles_per_seq, 0))
        z_shape = jax.ShapeDtypeStruct((n_seq * 8, MIX_W), F32)
    else:
        cp_spec = pl.BlockSpec((8, 3 * MIX_W), lambda i: (0, 0))
        e_spec = pl.BlockSpec((tm, MIX_W), lambda i: (i, 0))
        z_spec = pl.BlockSpec((tm, MIX_W), lambda i: (i, 0))
        z_shape = jax.ShapeDtypeStruct((t, MIX_W), F32)
    row = lambda w: pl.BlockSpec((tm, w), lambda i: (i, 0))
    return pl.pallas_call(
        functools.partial(_merge_body, tm=tm, seg=seg, tiles_per_seq=tiles_per_seq),
        grid=(t // tm,),
        in_specs=[row(D_MODEL), row(MIX_W), row(MIX_W), row(3 * MIX_W), cp_spec, e_spec, e_spec, row(2 * MIX_W),
                  _const_spec(conv_w.shape), _const_spec(sgu_ln.shape), _const_spec(wm.shape),
                  _const_spec(sgu_bias.shape), _const_spec(wg.shape), _const_spec(bgate.shape),
                  _const_spec(wb.shape), _const_spec(wo.shape), _const_spec(ln1.shape)],
        out_specs=[row(D_MODEL), z_spec, row(MIX_W)],
        out_shape=[jax.ShapeDtypeStruct((t, D_MODEL), F32), z_shape, jax.ShapeDtypeStruct((t, MIX_W), F32)],
        compiler_params=_cparams("arbitrary"),
        name="merge",
    )(x, ya, yc, h_conv, h_conv, e0, e1, h_sgu, conv_w, sgu_ln, wm, sgu_bias, wg, bgate, wb, wo, ln1)


def _matmul2_body(x_ref, w1_ref, w2_ref, o1_ref, o2_ref):
    xb = x_ref[...].astype(BF16)
    o1_ref[...] = jnp.dot(xb, w1_ref[...], preferred_element_type=F32).astype(o1_ref.dtype)
    o2_ref[...] = jnp.dot(xb, w2_ref[...], preferred_element_type=F32).astype(o2_ref.dtype)


def _matmul2(x, w1, w2, out_dtype, tm):
    t, kdim = x.shape
    n1, n2 = w1.shape[1], w2.shape[1]
    return pl.pallas_call(
        _matmul2_body,
        grid=(t // tm,),
        in_specs=[pl.BlockSpec((tm, kdim), lambda i: (i, 0)), _const_spec(w1.shape), _const_spec(w2.shape)],
        out_specs=[pl.BlockSpec((tm, n1), lambda i: (i, 0)), pl.BlockSpec((tm, n2), lambda i: (i, 0))],
        out_shape=[jax.ShapeDtypeStruct((t, n1), out_dtype), jax.ShapeDtypeStruct((t, n2), out_dtype)],
        compiler_params=_cparams("parallel"),
        name="matmul2",
    )(x, w1, w2)


def _matmul_body(x_ref, w_ref, o_ref):
    o_ref[...] = jnp.dot(x_ref[...].astype(BF16), w_ref[...], preferred_element_type=F32).astype(o_ref.dtype)


def _matmul(x, w, out_dtype, tm):
    t, kdim = x.shape
    n = w.shape[1]
    return pl.pallas_call(
        _matmul_body,
        grid=(t // tm,),
        in_specs=[pl.BlockSpec((tm, kdim), lambda i: (i, 0)), _const_spec(w.shape)],
        out_specs=pl.BlockSpec((tm, n), lambda i: (i, 0)),
        out_shape=jax.ShapeDtypeStruct((t, n), out_dtype),
        compiler_params=_cparams("parallel"),
        name="matmul",
    )(x, w)


def _softmax_rows(sc):
    m = jnp.max(sc, axis=-1, keepdims=True)
    e = jnp.exp(sc - m)
    return e / jnp.sum(e, axis=-1, keepdims=True)


def _xattn_body(q_ref, k_ref, v_ref, o_ref):
    q = q_ref[...]
    kb = k_ref[...].astype(BF16)
    vb = v_ref[...].astype(BF16)
    for h in range(X_HEADS):
        cols = slice(h * X_HD, (h + 1) * X_HD)
        sc = lax.dot_general(q[:, cols].astype(BF16), kb[:, cols], (((1,), (1,)), ((), ())),
                             preferred_element_type=F32) * (X_HD ** -0.5)
        o = jnp.dot(_softmax_rows(sc).astype(BF16), vb[:, cols], preferred_element_type=F32)
        o_ref[:, cols] = o.astype(o_ref.dtype)


def _xattn(q, mem_k, mem_v, tq):
    t = q.shape[0]
    tiles_per_mem = t // mem_k.shape[0] // tq
    mem_spec = pl.BlockSpec((None, N_MEM, D_MODEL), lambda i: (i // tiles_per_mem, 0, 0))
    return pl.pallas_call(
        _xattn_body,
        grid=(t // tq,),
        in_specs=[pl.BlockSpec((tq, D_MODEL), lambda i: (i, 0)), mem_spec, mem_spec],
        out_specs=pl.BlockSpec((tq, D_MODEL), lambda i: (i, 0)),
        out_shape=jax.ShapeDtypeStruct((t, D_MODEL), q.dtype),
        compiler_params=_cparams("parallel"),
        name="xattn",
    )(q, mem_k, mem_v)


def _xattn_rows_body(q_ref, k_ref, v_ref, o_ref, *, tq, slots):
    rows = X_HEADS * tq
    n_col = N_MEM * X_HEADS
    own = (_iota((rows, n_col), 1) & (X_HEADS - 1)) == (_iota((rows, n_col), 0) >> int(math.log2(tq)))
    for s in range(slots):
        q = q_ref[s * tq:(s + 1) * tq, :]
        qs = jnp.concatenate([q[:, h * X_HD:(h + 1) * X_HD] for h in range(X_HEADS)], axis=0).astype(BF16)
        k_all = k_ref[s].reshape(n_col, X_HD).astype(BF16)
        v_all = v_ref[s].reshape(n_col, X_HD).astype(BF16)
        sc = lax.dot_general(qs, k_all, (((1,), (1,)), ((), ())), preferred_element_type=F32) * (X_HD ** -0.5)
        pr = _softmax_rows(jnp.where(own, sc, -1e30))
        o = jnp.dot(pr.astype(BF16), v_all, preferred_element_type=F32)
        for h in range(X_HEADS):
            o_ref[s * tq:(s + 1) * tq, h * X_HD:(h + 1) * X_HD] = o[h * tq:(h + 1) * tq].astype(o_ref.dtype)


def _xattn_rows(q, mem_k, mem_v, layer, tq, slots):
    t = q.shape[0]
    rows = tq * slots
    mem_spec = pl.BlockSpec((None, slots, N_MEM, X_HEADS, X_HD), lambda i: (layer, i, 0, 0, 0))
    return pl.pallas_call(
        functools.partial(_xattn_rows_body, tq=tq, slots=slots),
        grid=(t // rows,),
        in_specs=[pl.BlockSpec((rows, D_MODEL), lambda i: (i, 0)), mem_spec, mem_spec],
        out_specs=pl.BlockSpec((rows, D_MODEL), lambda i: (i, 0)),
        out_shape=jax.ShapeDtypeStruct((t, D_MODEL), q.dtype),
        compiler_params=_cparams("parallel"),
        name="xattn_rows",
    )(q, mem_k, mem_v)


def _proj_ln_body(y_ref, w_ref, x_ref, ln_ref, o_ref):
    ln = ln_ref[...]
    acc = jnp.dot(y_ref[...].astype(BF16), w_ref[...], preferred_element_type=F32)
    o_ref[...] = _ln_rows(ALPHA * x_ref[...] + acc, ln[0:1, :], ln[1:2, :])


def _proj_ln(y, w, x, ln, tm):
    t = x.shape[0]
    return pl.pallas_call(
        _proj_ln_body,
        grid=(t // tm,),
        in_specs=[pl.BlockSpec((tm, D_MODEL), lambda i: (i, 0)), _const_spec(w.shape),
                  pl.BlockSpec((tm, D_MODEL), lambda i: (i, 0)), _const_spec(ln.shape)],
        out_specs=pl.BlockSpec((tm, D_MODEL), lambda i: (i, 0)),
        out_shape=jax.ShapeDtypeStruct((t, D_MODEL), F32),
        compiler_params=_cparams("parallel"),
        name="proj_ln",
    )(y, w, x, ln)


def _mlp_body(x_ref, wu_ref, wd_ref, ln_ref, o_ref):
    x = x_ref[...]
    xb = x.astype(BF16)
    acc = jnp.zeros(x.shape, F32)
    for j in range(D_FF // D_MODEL):
        cols = slice(j * D_MODEL, (j + 1) * D_MODEL)
        hid = jnp.maximum(jnp.dot(xb, wu_ref[:, cols], preferred_element_type=F32), 0.0)
        acc = acc + jnp.dot((hid * hid).astype(BF16), wd_ref[cols, :], preferred_element_type=F32)
    ln = ln_ref[...]
    o_ref[...] = _ln_rows(ALPHA * x + acc, ln[0:1, :], ln[1:2, :])


def _mlp(x, wu, wd, ln, tm):
    t = x.shape[0]
    return pl.pallas_call(
        _mlp_body,
        grid=(t // tm,),
        in_specs=[pl.BlockSpec((tm, D_MODEL), lambda i: (i, 0)), _const_spec(wu.shape), _const_spec(wd.shape),
                  _const_spec(ln.shape)],
        out_specs=pl.BlockSpec((tm, D_MODEL), lambda i: (i, 0)),
        out_shape=jax.ShapeDtypeStruct((t, D_MODEL), F32),
        compiler_params=_cparams("parallel"),
        name="mlp",
    )(x, wu, wd, ln)


def _block_diag_states(s):
    n = s.shape[0]
    eye = jnp.eye(N_HEADS, dtype=s.dtype)
    return (s[:, :, :, None, :] * eye[None, :, None, :, None]).reshape(n, MIX_W, MIX_W)


def _diag_states(s):
    n = s.shape[0]
    s5 = s.reshape(n, N_HEADS, HEAD_W, N_HEADS, HEAD_W)
    return jnp.stack([s5[:, h, :, h, :] for h in range(N_HEADS)], axis=1)


def _layer_params(l, w_in, sb_bias, w_gate, b_gate, w_branch, w_o, conv_w, rw_mu, rw_w0, rw_w2, rw_a0, rw_a2, rw_g2,
                  rw_kk, rw_ka, rw_rk, rw_gn_g, rw_gn_b, sgu_ln_g, sgu_ln_b, sgu_ws, sgu_b, w_mq, w_mk, w_mv, w_mo,
                  w_up, w_down, ln1_g, ln1_b, ln2_g, ln2_b, ln3_g, ln3_b):
    wi = w_in[l].astype(BF16)
    off_b, off_c, off_d = 3 * MIX_W, 6 * MIX_W, 6 * MIX_W + RW_COLS
    zpad = lambda w, r0: jnp.zeros((128, MIX_W), F32).at[r0:r0 + w.shape[0]].set(w).astype(BF16)
    par = jnp.zeros((8, MIX_W), F32)
    for i, vec in enumerate((rw_w0[l], rw_a0[l], rw_kk[l], rw_ka[l], rw_rk[l].reshape(MIX_W), rw_gn_g[l], rw_gn_b[l])):
        par = par.at[i].set(vec)
    return dict(
        w_qkv=wi[:, :off_b], w_conv=wi[:, off_b:off_c], w_rw=wi[:, off_c:off_d], w_sgu=wi[:, off_d:],
        sb_bias=sb_bias[l],
        w_gate=w_gate[l].astype(BF16), b_gate=b_gate[l].reshape(1, -1), w_branch=w_branch[l].astype(BF16),
        w_o=w_o[l].astype(BF16), conv_w=jnp.zeros((8, MIX_W), F32).at[:3].set(conv_w[l]),
        rw_mu=rw_mu[l].reshape(1, RW_COLS), rw_par=par,
        rw_w2=zpad(rw_w2[l], 0), rw_a2=zpad(rw_a2[l], 32), rw_g2=zpad(rw_g2[l], 64),
        sgu_ln=jnp.stack([sgu_ln_g[l], sgu_ln_b[l]]), sgu_ws=sgu_ws[l], sgu_b=sgu_b[l],
        w_mq=w_mq[l].astype(BF16), w_mk=w_mk[l].astype(BF16), w_mv=w_mv[l].astype(BF16), w_mo=w_mo[l].astype(BF16),
        w_up=w_up[l].astype(BF16), w_down=w_down[l].astype(BF16),
        ln1=jnp.stack([ln1_g[l], ln1_b[l]]), ln2=jnp.stack([ln2_g[l], ln2_b[l]]), ln3=jnp.stack([ln3_g[l], ln3_b[l]]),
    )


def _sgu_tables(lp, seg):
    ws = lp['sgu_ws']
    sb = lp['sgu_b']
    if seg < SGU_CHUNK:
        reps = SGU_CHUNK // seg
        ws = jnp.tile(ws[:, :seg, :seg], (1, reps, reps))
        sb = jnp.tile(sb[:, :seg], (1, reps))
    wm = jnp.transpose(ws, (1, 0, 2)).reshape(SGU_CHUNK, N_HEADS * SGU_CHUNK)
    bias = jnp.repeat(sb.T, HEAD_W, axis=1)
    return wm, bias


def _sb_bias_rows(bias, tq, width):
    return jnp.broadcast_to(jnp.repeat(bias, tq)[:, None], (N_HEADS * tq, width)).astype(F32)


def _tail(x, lp, attend, q_dtype, tm):
    qm = _matmul(x, lp['w_mq'], q_dtype, tm)
    x = _proj_ln(attend(qm), lp['w_mo'], x, lp['ln2'], tm)
    return _mlp(x, lp['w_up'], lp['w_down'], lp['ln3'], tm)


def _layer_prompt(x, lp, mem_k, mem_v, n, seq):
    t = n * seq
    h_qkv, h_conv, h_rw, h_sgu = _inproj(x, lp['w_qkv'], lp['w_conv'], lp['w_rw'], lp['w_sgu'], 512)
    tq = 256
    ya = _sb_prompt(h_qkv, _sb_bias_rows(lp['sb_bias'], tq, tq), n, seq, tq)
    pe = jnp.zeros((n, 8, RW_COLS), F32)
    s0 = jnp.zeros((n, MIX_W, MIX_W), F32)
    yc, s_fin = _rwkv(h_rw.reshape(n, seq, RW_COLS), pe, s0, lp['rw_mu'], lp['rw_par'], lp['rw_w2'], lp['rw_a2'],
                      lp['rw_g2'], n_prob=n, nb=1, c=RW_ROWS, chain=2)
    wm, sgu_bias = _sgu_tables(lp, SGU_CHUNK)
    zero_e = jnp.zeros((8, MIX_W), F32)
    x1, z_tail, _ = _merge(x, ya, yc.reshape(t, MIX_W), h_conv, zero_e, zero_e, h_sgu, lp['conv_w'], lp['sgu_ln'],
                           wm, sgu_bias, lp['w_gate'], lp['b_gate'], lp['w_branch'], lp['w_o'], lp['ln1'],
                           tm=256, seg=seq)
    x3 = _tail(x1, lp, lambda qm: _xattn(qm, mem_k, mem_v, 512), BF16, 512)
    k_new = h_qkv[:, MIX_W:2 * MIX_W].reshape(n, seq, N_HEADS, HEAD_W)
    v_new = h_qkv[:, 2 * MIX_W:].reshape(n, seq, N_HEADS, HEAD_W)
    conv_new = z_tail.reshape(n, 8, MIX_W)[:, 6:8]
    shift_new = h_rw.reshape(n, seq, RW_COLS)[:, -1]
    return x3, k_new, v_new, conv_new, shift_new, _diag_states(s_fin)


def _layer_sample(x, lp, layer, mem_k, mem_v, cache_kt, cache_vt, page_table, state_conv, state_shift, state_wkv,
                  nb, seq):
    t = nb * seq
    h_qkv, h_conv, h_rw, h_sgu = _inproj(x, lp['w_qkv'], lp['w_conv'], lp['w_rw'], lp['w_sgu'], 512)
    k_new = h_qkv[:, MIX_W:2 * MIX_W].reshape(nb, seq, MIX_W)
    v_new = h_qkv[:, 2 * MIX_W:].reshape(nb, seq, MIX_W)
    ya = _sb_sample(h_qkv, cache_kt, cache_vt, page_table, _sb_bias_rows(lp['sb_bias'], seq, PAGE_SIZE),
                    layer, nb, seq)
    per = RW_ROWS // seq
    pe = jnp.repeat(state_shift, seq, axis=0).reshape(nb // per, RW_ROWS, RW_COLS)
    yc, s_fin = _rwkv(h_rw.reshape(nb // per, RW_ROWS, RW_COLS), pe, _block_diag_states(state_wkv), lp['rw_mu'],
                      lp['rw_par'], lp['rw_w2'], lp['rw_a2'], lp['rw_g2'], n_prob=2, nb=per, c=seq, chain=1)
    wm, sgu_bias = _sgu_tables(lp, seq)
    e0 = jnp.repeat(state_conv[:, 0], seq, axis=0)
    e1 = jnp.repeat(state_conv[:, 1], seq, axis=0)
    x1, z_all, sgu_v = _merge(x, ya, yc.reshape(t, MIX_W), h_conv, e0, e1, h_sgu, lp['conv_w'], lp['sgu_ln'], wm,
                              sgu_bias, lp['w_gate'], lp['b_gate'], lp['w_branch'], lp['w_o'], lp['ln1'],
                              tm=256, seg=seq)
    x3 = _tail(x1, lp, lambda qm: _xattn_rows(qm, mem_k, mem_v, layer, seq, 4), F32, 512)
    conv_new = z_all.reshape(nb, seq, MIX_W)[:, seq - 2:]
    shift_new = h_rw.reshape(nb, seq, RW_COLS)[:, -1]
    return (x3, k_new.reshape(nb, seq, N_HEADS, HEAD_W), v_new.reshape(nb, seq, N_HEADS, HEAD_W), conv_new,
            shift_new, _diag_states(s_fin), sgu_v.reshape(nb, seq, MIX_W))


def kernel(x_prompt, x_sample, mem_prompt, cache_k, cache_v, page_table, cache_mem_k, cache_mem_v, state_conv,
           state_wkv, state_shift, w_in, sb_bias, w_gate, b_gate, w_branch, w_o, conv_w, rw_mu, rw_w0, rw_w2, rw_a0,
           rw_a2, rw_g2, rw_kk, rw_ka, rw_rk, rw_gn_g, rw_gn_b, sgu_ln_g, sgu_ln_b, sgu_ws, sgu_b, w_mq, w_mk, w_mv,
           w_mo, w_up, w_down, ln1_g, ln1_b, ln2_g, ln2_b, ln3_g, ln3_b):
    n_p, seq_p, _ = x_prompt.shape
    n_s, seq_s, _ = x_sample.shape
    n_phys = cache_k.shape[1]
    xp = x_prompt.reshape(n_p * seq_p, D_MODEL)
    xs = x_sample.reshape(n_s * seq_s, D_MODEL)
    mem2d = mem_prompt.reshape(n_p * N_MEM, D_MODEL)
    cache_kt = jnp.transpose(cache_k, (0, 1, 3, 4, 2)).reshape(DEPTH, n_phys, MIX_W, PAGE_SIZE)
    cache_vt = jnp.transpose(cache_v, (0, 1, 3, 4, 2)).reshape(DEPTH, n_phys, MIX_W, PAGE_SIZE)
    outs = [[] for _ in range(13)]
    for l in range(DEPTH):
        lp = _layer_params(l, w_in, sb_bias, w_gate, b_gate, w_branch, w_o, conv_w, rw_mu, rw_w0, rw_w2, rw_a0,
                           rw_a2, rw_g2, rw_kk, rw_ka, rw_rk, rw_gn_g, rw_gn_b, sgu_ln_g, sgu_ln_b, sgu_ws, sgu_b,
                           w_mq, w_mk, w_mv, w_mo, w_up, w_down, ln1_g, ln1_b, ln2_g, ln2_b, ln3_g, ln3_b)
        mk, mv = _matmul2(mem2d, lp['w_mk'], lp['w_mv'], F32, 512)
        mk4 = mk.reshape(n_p, N_MEM, X_HEADS, X_HD)
        mv4 = mv.reshape(n_p, N_MEM, X_HEADS, X_HD)
        xp, pk, pv, pc, psh, pst = _layer_prompt(xp, lp, mk.reshape(n_p, N_MEM, D_MODEL),
                                                 mv.reshape(n_p, N_MEM, D_MODEL), n_p, seq_p)
        xs, sk, sv, sc, ssh, sst, scv = _layer_sample(
            xs, lp, l, cache_mem_k, cache_mem_v, cache_kt, cache_vt, page_table,
            state_conv[l], state_shift[l], state_wkv[l], n_s, seq_s)
        for lst, val in zip(outs, (pk, pv, mk4, mv4, pc, pst, psh, sk, sv, sc, sst, ssh, scv)):
            lst.append(val)
    return (xp.reshape(n_p, seq_p, D_MODEL), xs.reshape(n_s, seq_s, D_MODEL)) + tuple(jnp.stack(o) for o in outs)
```

```python
import functools
import math

import jax
import jax.numpy as jnp
from jax import lax
from jax.experimental import pallas as pl
from jax.experimental.pallas import tpu as pltpu

F32 = jnp.float32
BF16 = jnp.bfloat16

D_MODEL = 1024
DEPTH = 2
MIX_W = 256
HEAD_W = 64
N_HEADS = MIX_W // HEAD_W
RW_COLS = 896
PAGE_SIZE = 128
N_MEM = 256
X_HEADS = 4
X_HD = D_MODEL // X_HEADS
D_FF = 4 * D_MODEL
ALPHA = (2 * DEPTH) ** 0.25
LN_EPS = 1e-5
RW_GN_EPS = 64e-5
SGU_CHUNK = 128
RW_ROWS = 64
VMEM_LIMIT = 56 * 1024 * 1024


def _cparams(*sem):
    return pltpu.CompilerParams(dimension_semantics=sem, vmem_limit_bytes=VMEM_LIMIT)


def _const_spec(shape):
    nd = len(shape)
    return pl.BlockSpec(shape, lambda *_: (0,) * nd, pipeline_mode=pl.Buffered(1))


def _bdot(a, b):
    return jnp.dot(a.astype(BF16), b.astype(BF16), preferred_element_type=F32)


def _bdot_nt(a, b):
    return lax.dot_general(a.astype(BF16), b.astype(BF16), (((1,), (1,)), ((), ())),
                           preferred_element_type=F32)


def _bdot_tn(a, b):
    return lax.dot_general(a.astype(BF16), b.astype(BF16), (((0,), (0,)), ((), ())),
                           preferred_element_type=F32)


def _split2(x):
    hi = x.astype(BF16)
    lo = (x - hi.astype(F32)).astype(BF16)
    return hi, lo


def _split3(x):
    hi = x.astype(BF16)
    r1 = x - hi.astype(F32)
    mid = r1.astype(BF16)
    lo = (r1 - mid.astype(F32)).astype(BF16)
    return hi, mid, lo


def _sel_dot_l(sel_b, x):
    return sum(jnp.dot(sel_b, part, preferred_element_type=F32) for part in _split3(x))


def _sel_dot_r(x, sel_b):
    return sum(jnp.dot(part, sel_b, preferred_element_type=F32) for part in _split3(x))


def _dot3(a, b):
    ah, al = _split2(a)
    bh, bl = _split2(b)
    d = lambda x, y: jnp.dot(x, y, preferred_element_type=F32)
    return d(ah, bh) + d(ah, bl) + d(al, bh)


def _sigmoid(x):
    return 1.0 / (1.0 + jnp.exp(-x))


def _softplus(x):
    return jnp.maximum(x, 0.0) + jnp.log(1.0 + jnp.exp(-jnp.abs(x)))


def _gelu_tanh(x):
    return 0.5 * x * (1.0 + jnp.tanh(0.7978845608028654 * (x + 0.044715 * (x * x * x))))


def _ln_rows(x, g, b, eps=LN_EPS):
    mu = jnp.mean(x, axis=-1, keepdims=True)
    xc = x - mu
    var = jnp.mean(xc * xc, axis=-1, keepdims=True)
    return xc * lax.rsqrt(var + eps) * g + b


def _iota(shape, dim):
    return lax.broadcasted_iota(jnp.int32, shape, dim)


def _head_block_mask(rows, row_shift):
    return (_iota((rows, MIX_W), 0) >> row_shift) == (_iota((rows, MIX_W), 1) >> 6)


def _stack_heads(x, mask):
    t = jnp.concatenate([x] * N_HEADS, axis=0)
    return jnp.where(mask, t, jnp.zeros_like(t))


def _unstack_heads(x, rows):
    return x[0:rows] + x[rows:2 * rows] + x[2 * rows:3 * rows] + x[3 * rows:4 * rows]


def _inproj_body(x_ref, wq_ref, wc_ref, wr_ref, ws_ref, oq_ref, oc_ref, or_ref, os_ref):
    xb = x_ref[...].astype(BF16)
    oq_ref[...] = jnp.dot(xb, wq_ref[...], preferred_element_type=F32)
    oc_ref[...] = jnp.dot(xb, wc_ref[...], preferred_element_type=F32)
    or_ref[...] = jnp.dot(xb, wr_ref[...], preferred_element_type=F32)
    os_ref[...] = jnp.dot(xb, ws_ref[...], preferred_element_type=F32)


def _inproj(x, wq, wc, wr, ws, tm):
    t = x.shape[0]
    widths = (wq.shape[1], wc.shape[1], wr.shape[1], ws.shape[1])
    return pl.pallas_call(
        _inproj_body,
        grid=(t // tm,),
        in_specs=[pl.BlockSpec((tm, D_MODEL), lambda i: (i, 0))] + [_const_spec(w.shape) for w in (wq, wc, wr, ws)],
        out_specs=[pl.BlockSpec((tm, w), lambda i: (i, 0)) for w in widths],
        out_shape=[jax.ShapeDtypeStruct((t, w), F32) for w in widths],
        compiler_params=_cparams("parallel"),
        name="inproj",
    )(x, wq, wc, wr, ws)


def _suffix_ones(tk):
    return (_iota((tk, tk), 0) >= _iota((tk, tk), 1)).astype(BF16)


def _sb_weights(z, carry, suffix, mask):
    tk = z.shape[1]
    stop = _softplus(z)
    if mask is not None:
        stop = jnp.where(mask, stop, 0.0)
    cs = jnp.dot(stop.astype(BF16), suffix, preferred_element_type=F32)
    later = carry if tk == 128 else jnp.concatenate([carry] * (tk // 128), axis=1)
    w = jnp.exp(z - cs - later)
    if mask is not None:
        w = jnp.where(mask, w, 0.0)
    total = jnp.sum(stop, axis=1, keepdims=True)
    return w, carry + jnp.broadcast_to(total, carry.shape)


def _sb_prompt_body(q_ref, k_ref, v_ref, bias_ref, o_ref, kb, vb, carry_ref, acc_ref, *, tq):
    qi = pl.program_id(1)
    rows = N_HEADS * tq

    @pl.when(qi == 0)
    def _():
        kb[...] = k_ref[...].astype(BF16)
        vb[...] = v_ref[...].astype(BF16)

    hmask = _head_block_mask(rows, int(math.log2(tq)))
    qs = _stack_heads((q_ref[...] * (HEAD_W ** -0.5)).astype(BF16), hmask)
    suffix = _suffix_ones(tq)

    def block(start, carry, acc, mask):
        z = lax.dot_general(qs, kb[pl.ds(start, tq), :], (((1,), (1,)), ((), ())),
                            preferred_element_type=F32) + bias_ref[...]
        w, carry = _sb_weights(z, carry, suffix, mask)
        acc = acc + jnp.dot(w.astype(BF16), vb[pl.ds(start, tq), :], preferred_element_type=F32)
        return carry, acc

    def block_pair(start):
        k2 = kb[pl.ds(start, 2 * tq), :]
        v2 = vb[pl.ds(start, 2 * tq), :]
        hrows = lambda h: slice(h * tq, (h + 1) * tq)
        logits = lambda h: lax.dot_general(qs[hrows(h)], k2, (((1,), (1,)), ((), ())),
                                           preferred_element_type=F32)
        z2 = [logits(0), logits(1)]
        for h in range(N_HEADS):
            bias = bias_ref[hrows(h), :]
            z = jnp.concatenate([z2[h][:, tq:] + bias, z2[h][:, :tq] + bias], axis=0)
            stop = _softplus(z)
            cs = jnp.dot(stop.astype(BF16), suffix, preferred_element_type=F32)
            if h + 2 < N_HEADS:
                z2.append(logits(h + 2))
            total = jnp.broadcast_to(jnp.sum(stop, axis=1, keepdims=True), (2 * tq, 128))
            carry = carry_ref[hrows(h), :]
            carry2 = jnp.concatenate([carry, carry + total[:tq]], axis=0)
            later = jnp.concatenate([carry2] * (tq // 128), axis=1)
            w = jnp.exp(z - cs - later).astype(BF16)
            w2 = jnp.concatenate([w[tq:], w[:tq]], axis=1)
            acc_ref[hrows(h), :] += jnp.dot(w2, v2, preferred_element_type=F32)
            carry_ref[hrows(h), :] = carry2[tq:] + total[tq:]

    t_idx = _iota((rows, tq), 0) & (tq - 1)
    causal = _iota((rows, tq), 1) < t_idx
    carry, acc = block(pl.multiple_of(qi * tq, tq), jnp.zeros((rows, 128), F32),
                       jnp.zeros((rows, MIX_W), F32), causal)
    carry_ref[...] = carry
    acc_ref[...] = acc

    @pl.when((qi & 1) == 1)
    def _():
        c, a = block(pl.multiple_of((qi - 1) * tq, tq), carry_ref[...], acc_ref[...], None)
        carry_ref[...] = c
        acc_ref[...] = a

    n_pairs = qi >> 1

    def step(jj, _):
        block_pair(pl.multiple_of((n_pairs - 1 - jj) * (2 * tq), 2 * tq))
        return 0

    lax.fori_loop(0, n_pairs, step, 0)
    acc = jnp.where(hmask, acc_ref[...], 0.0)
    o_ref[...] = _unstack_heads(acc, tq)


def _sb_prompt(h_qkv, bias_rows, n, seq, tq):
    nq = seq // tq
    rows = N_HEADS * tq
    return pl.pallas_call(
        functools.partial(_sb_prompt_body, tq=tq),
        grid=(n, nq),
        in_specs=[
            pl.BlockSpec((tq, MIX_W), lambda b, i: (b * nq + i, 0)),
            pl.BlockSpec((seq, MIX_W), lambda b, i: (b, 1)),
            pl.BlockSpec((seq, MIX_W), lambda b, i: (b, 2)),
            _const_spec((rows, tq)),
        ],
        out_specs=pl.BlockSpec((tq, MIX_W), lambda b, i: (b * nq + i, 0)),
        out_shape=jax.ShapeDtypeStruct((n * seq, MIX_W), F32),
        scratch_shapes=[pltpu.VMEM((seq, MIX_W), BF16), pltpu.VMEM((seq, MIX_W), BF16),
                        pltpu.VMEM((rows, 128), F32), pltpu.VMEM((rows, MIX_W), F32)],
        compiler_params=_cparams("parallel", "arbitrary"),
        name="sb_prompt",
    )(h_qkv, h_qkv, h_qkv, bias_rows)


SB_SEQ_SLOTS = 3


def _sb_sample_body(pt_ref, q_ref, kn_ref, vn_ref, ck_ref, cv_ref, bias_ref, o_ref, kbuf, vbuf, sem,
                    *, layer, tq, n_pages):
    b = pl.program_id(0)
    rows = N_HEADS * tq
    n_blk = n_pages + 1

    def seq_copies(s):
        slot = lax.rem(s, SB_SEQ_SLOTS)
        out = []
        for j in range(n_pages):
            page = pt_ref[s, n_pages - 1 - j]
            out.append(pltpu.make_async_copy(ck_ref.at[layer, page], kbuf.at[slot, j], sem.at[slot]))
            out.append(pltpu.make_async_copy(cv_ref.at[layer, page], vbuf.at[slot, j], sem.at[slot]))
        return slot, out

    @pl.when(b == 0)
    def _():
        for s in range(SB_SEQ_SLOTS - 1):
            for cp in seq_copies(jnp.int32(s))[1]:
                cp.start()

    @pl.when(b + (SB_SEQ_SLOTS - 1) < pl.num_programs(0))
    def _():
        for cp in seq_copies(b + (SB_SEQ_SLOTS - 1))[1]:
            cp.start()

    slot, copies = seq_copies(b)
    for cp in copies:
        cp.wait()

    hmask = _head_block_mask(rows, int(math.log2(tq)))
    qs = _stack_heads(q_ref[...] * (HEAD_W ** -0.5), hmask).astype(BF16)
    bias = bias_ref[...]

    pad = jnp.zeros((PAGE_SIZE - tq, MIX_W), F32)
    k_new = jnp.concatenate([kn_ref[...], pad], axis=0).astype(BF16)
    v_new = jnp.concatenate([vn_ref[...], pad], axis=0).astype(BF16)
    z_new = lax.dot_general(qs, k_new, (((1,), (1,)), ((), ())), preferred_element_type=F32)
    k_pages = jnp.concatenate([kbuf[slot, j].astype(BF16) for j in range(n_pages)], axis=1)
    z_pages = jnp.dot(qs, k_pages, preferred_element_type=F32)
    z = jnp.concatenate([z_new] + [z_pages[:, j * PAGE_SIZE:(j + 1) * PAGE_SIZE] for j in range(n_pages)], axis=0)
    z = z + jnp.concatenate([bias] * n_blk, axis=0)

    r_i = _iota((n_blk * rows, PAGE_SIZE), 0)
    valid = (r_i >= rows) | (_iota((n_blk * rows, PAGE_SIZE), 1) < (r_i & (tq - 1)))
    stop = jnp.where(valid, _softplus(z), 0.0)
    cs = jnp.dot(stop.astype(BF16), _suffix_ones(PAGE_SIZE), preferred_element_type=F32)
    total = jnp.broadcast_to(jnp.sum(stop, axis=1, keepdims=True), (n_blk * rows, PAGE_SIZE))
    carry = [jnp.zeros((rows, PAGE_SIZE), F32)]
    for i in range(n_blk - 1):
        carry.append(carry[-1] + total[i * rows:(i + 1) * rows])
    w = jnp.where(valid, jnp.exp(z - cs - jnp.concatenate(carry, axis=0)), 0.0).astype(BF16)

    acc = jnp.dot(w[:rows], v_new, preferred_element_type=F32)
    w_pages = jnp.concatenate([w[(j + 1) * rows:(j + 2) * rows] for j in range(n_pages)], axis=1)
    v_pages = jnp.concatenate([vbuf[slot, j].astype(BF16) for j in range(n_pages)], axis=1)
    acc = acc + lax.dot_general(w_pages, v_pages, (((1,), (1,)), ((), ())), preferred_element_type=F32)
    o_ref[...] = _unstack_heads(jnp.where(hmask, acc, 0.0), tq)


def _sb_sample(h_qkv, cache_kt, cache_vt, page_table, bias_rows, layer, nb, tq):
    n_pages = page_table.shape[1]
    rows = N_HEADS * tq
    buf = (SB_SEQ_SLOTS, n_pages, MIX_W, PAGE_SIZE)
    grid_spec = pltpu.PrefetchScalarGridSpec(
        num_scalar_prefetch=1,
        grid=(nb,),
        in_specs=[
            pl.BlockSpec((tq, MIX_W), lambda b, pt: (b, 0)),
            pl.BlockSpec((tq, MIX_W), lambda b, pt: (b, 1)),
            pl.BlockSpec((tq, MIX_W), lambda b, pt: (b, 2)),
            pl.BlockSpec(memory_space=pl.ANY),
            pl.BlockSpec(memory_space=pl.ANY),
            pl.BlockSpec((rows, PAGE_SIZE), lambda b, pt: (0, 0)),
        ],
        out_specs=pl.BlockSpec((tq, MIX_W), lambda b, pt: (b, 0)),
        scratch_shapes=[pltpu.VMEM(buf, F32), pltpu.VMEM(buf, F32), pltpu.SemaphoreType.DMA((SB_SEQ_SLOTS,))],
    )
    return pl.pallas_call(
        functools.partial(_sb_sample_body, layer=layer, tq=tq, n_pages=n_pages),
        grid_spec=grid_spec,
        out_shape=jax.ShapeDtypeStruct((nb * tq, MIX_W), F32),
        compiler_params=_cparams("arbitrary"),
        name="sb_sample",
    )(page_table, h_qkv, h_qkv, h_qkv, cache_kt, cache_vt, bias_rows)


def _each(fn, *lists):
    return [fn(*args) for args in zip(*lists)]


def _rwkv_problems(ps, p_prevs, s_lists, mu, par, w2p, a2p, g2p, masks, *, nb, c, chain):
    ones_bd, bd, m_strict, m_incl, eye_cat, l_tri, l_all, eye_s = masks
    w0, a0, kk_s, ka_s, rk, gn_g, gn_b = (par[i:i + 1, :] for i in range(7))
    mm = lambda x, y: jnp.dot(x.astype(BF16), y.astype(BF16), preferred_element_type=F32)
    expand = lambda x: _stack_heads(x.astype(BF16), bd)
    left = lambda x: x[:, :MIX_W]
    right = lambda x: x[:, MIX_W:]

    n_p = len(ps)

    def rows_batched(fn, xs):
        out = fn(xs[0] if n_p == 1 else jnp.concatenate(xs, axis=0))
        return [out[i * RW_ROWS:(i + 1) * RW_ROWS] for i in range(n_p)]

    head_sums = lambda xs: rows_batched(lambda x: _sel_dot_r(x, ones_bd), xs)

    xs = _each(lambda p, pp: p + (pp - p) * mu, ps, p_prevs)
    r = [x[:, 0:256] for x in xs]
    k = [x[:, 256:512] for x in xs]
    v = [x[:, 512:768] for x in xs]
    lora = [x[:, 768:896] for x in xs]
    u = rows_batched(lambda x: w0 + mm(jnp.tanh(x), w2p), lora)
    logw = [(-math.exp(-0.5)) * _sigmoid(x) for x in u]
    asig = rows_batched(lambda x: _sigmoid(a0 + mm(x, a2p)), lora)
    gate = rows_batched(lambda x: mm(_sigmoid(x), g2p), lora)
    kk = [x * kk_s for x in k]
    ssq = head_sums([x * x for x in kk])
    kk = _each(lambda x, q: x / jnp.maximum(jnp.sqrt(q), 1e-12), kk, ssq)
    k_eff = _each(lambda x, s: x * (1.0 + (s - 1.0) * ka_s), k, asig)
    rk_sum = head_sums(_each(lambda x, y: x * y * rk, r, k_eff))
    bonus = _each(lambda x, y: x * y, rk_sum, v)
    b = _each(lambda x, s: x * s, kk, asig)

    sums = _sel_dot_l(jnp.concatenate([l_tri, l_all], axis=0), jnp.concatenate(logw, axis=1))
    cum = [sums[:RW_ROWS, i * MIX_W:(i + 1) * MIX_W] for i in range(n_p)]
    cum_all = [sums[RW_ROWS:, i * MIX_W:(i + 1) * MIX_W] for i in range(n_p)]
    rt = _each(lambda x, q: (x * jnp.exp(q)).astype(BF16), r, cum)
    at = _each(lambda x, q, lw: (-x * jnp.exp(q - lw)).astype(BF16), kk, cum, logw)
    ginv = [jnp.exp(-q) for q in cum]
    bt = _each(lambda x, g: x * g, b, ginv)
    kt = _each(lambda x, g: x * g, k_eff, ginv)
    e_tail = _each(lambda qa, q: jnp.exp(qa - q), cum_all, cum)
    bg = _each(lambda x, e: x * e, b, e_tail)
    kg = _each(lambda x, e: x * e, k_eff, e_tail)
    g_end = [jnp.exp(q) for q in cum_all]

    lhs = _each(lambda x, y: jnp.concatenate([x, y], axis=0), at, rt)
    rhs = _each(lambda x, y: jnp.concatenate([expand(x), expand(y)], axis=0), bt, kt)
    a_cat = _each(lambda x, y: lax.dot_general(x, y, (((1,), (1,)), ((), ())), preferred_element_type=F32),
                  lhs, rhs)
    a_ab = [jnp.where(m_strict, left(x[:RW_ROWS]), 0.0) for x in a_cat]
    a_ak = [jnp.where(m_strict, right(x[:RW_ROWS]), 0.0) for x in a_cat]
    a_rb = [jnp.where(m_incl, left(x[RW_ROWS:]), 0.0) for x in a_cat]
    a_rk = [jnp.where(m_incl, right(x[RW_ROWS:]), 0.0) for x in a_cat]

    inv = [eye_cat + x for x in a_ab]
    apow = a_ab
    apow_bd = [expand(x) for x in apow]
    span = 1
    while 2 * span < c:
        apow = _each(mm, apow, apow_bd)
        apow_bd = [expand(x) for x in apow]
        inv = _each(lambda x, y: x + mm(x, y), inv, apow_bd)
        span *= 2

    sv = [expand(x) for x in v]
    w1 = _each(mm, a_ak, sv)
    uu = _each(lambda i, x, y: mm(i, jnp.concatenate([expand(x), expand(y)], axis=1)), inv, at, w1)
    ua = [left(x) for x in uu]
    uv = [right(x) for x in uu]
    qy = _each(lambda m, x, y: mm(m, jnp.concatenate([expand(x), expand(y)], axis=1)), a_rb, ua, uv)
    y0 = _each(lambda q, m, x: right(q) + mm(m, x), qy, a_rk, sv)
    qe = _each(lambda x, q: x.astype(F32) + left(q), rt, qy)

    row_seq = _iota((RW_ROWS, MIX_W), 0) >> int(math.log2(c))
    m_c, n_c = [], []
    for s in range(nb):
        own = lambda x: x if nb == 1 else jnp.where(row_seq == s, x, 0.0)
        bg_s = [own(x) for x in bg]
        kg_s = [own(x) for x in kg]
        m_c.append(_each(lambda x, y, g: jnp.where(bd, _bdot_tn(x, y), 0.0)
                         + jnp.where(eye_s, g[s * c:s * c + 1, :], 0.0), ua, bg_s, g_end))
        n_c.append(_each(lambda x, y, z, w: jnp.where(bd, _bdot_tn(x, y) + _bdot_tn(z, w), 0.0),
                         uv, bg_s, v, kg_s))

    states = [list(sl) for sl in s_lists]
    y_parts = [[] for _ in ps]
    for kk in range(chain):
        idx = [g * chain + kk for g in range(len(s_lists))]
        pick = lambda xs: [xs[i] for i in idx]
        for s in range(nb):
            rows = slice(s * c, (s + 1) * c)
            cur = [st[s] for st in states]
            y_s = _each(lambda q, st, y: _bdot_nt(q[rows], st) + y[rows], pick(qe), cur, pick(y0))
            nxt = _each(lambda st, m, n: _dot3(st, m) + n, cur, pick(m_c[s]), pick(n_c[s]))
            for gi, i in enumerate(idx):
                y_parts[i].append(y_s[gi])
                states[gi][s] = nxt[gi]
    s_new = states
    y = [parts[0] if nb == 1 else jnp.concatenate(parts, axis=0) for parts in y_parts]

    mean = [x * (1.0 / HEAD_W) for x in head_sums(y)]
    d = _each(lambda x, m: x - m, y, mean)
    var = [x * (1.0 / HEAD_W) for x in head_sums([x * x for x in d])]
    out = _each(lambda x, q, bo, g: (x * lax.rsqrt(q + RW_GN_EPS) * gn_g + gn_b + bo) * g, d, var, bonus, gate)
    return out, s_new


def _rwkv_body(p_ref, pe_ref, s0_ref, mu_ref, par_ref, w2_ref, a2_ref, g2_ref, y_ref, so_ref, s_scr, plast_scr,
               *, n_prob, nb, c, chain):
    ci = pl.program_id(1)
    g_rows = RW_ROWS
    r4 = N_HEADS * g_rows
    log_c = int(math.log2(c))

    bd = _head_block_mask(r4, 6)

    @pl.when(ci == 0)
    def _():
        for idx in range(n_prob * nb):
            s_scr[idx] = jnp.where(bd, jnp.concatenate([s0_ref[idx]] * N_HEADS, axis=1), 0.0)
        if nb == 1:
            plast_scr[...] = pe_ref[...]

    ones_bd = bd.astype(BF16)
    t_i = _iota((g_rows, r4), 0)
    s_i = _iota((g_rows, r4), 1) & (g_rows - 1)
    same = (t_i >> log_c) == (s_i >> log_c)
    m_strict = same & (s_i < t_i)
    m_incl = same & (s_i <= t_i)
    eye_cat = (s_i == t_i).astype(F32)
    ig = _iota((g_rows, g_rows), 0)
    jg = _iota((g_rows, g_rows), 1)
    same_g = (ig >> log_c) == (jg >> log_c)
    l_tri = (same_g & (jg <= ig)).astype(BF16)
    l_all = same_g.astype(BF16)
    eye_s = _iota((MIX_W, MIX_W), 0) == _iota((MIX_W, MIX_W), 1)
    masks = (ones_bd, bd, m_strict, m_incl, eye_cat, l_tri, l_all, eye_s)

    blk_rows = chain * g_rows
    row = _iota((blk_rows, RW_COLS), 0)
    blocks = [p_ref[pi] for pi in range(n_prob)]
    rolled = [pltpu.roll(p, 1, 0) for p in blocks]
    if nb == 1:
        prevs = [jnp.where(row == 0, plast_scr[pi][7:8, :], rolled[pi]) for pi in range(n_prob)]
        for pi in range(n_prob):
            plast_scr[pi] = blocks[pi][blk_rows - 8:blk_rows, :]
    else:
        prevs = [jnp.where((row & (c - 1)) == 0, pe_ref[pi], rolled[pi]) for pi in range(n_prob)]
    chunks = lambda xs: [x[kk * g_rows:(kk + 1) * g_rows] for x in xs for kk in range(chain)]
    s_lists = [[s_scr[pi * nb + s] for s in range(nb)] for pi in range(n_prob)]
    ys, s_new = _rwkv_problems(chunks(blocks), chunks(prevs), s_lists, mu_ref[...], par_ref[...], w2_ref[...],
                               a2_ref[...], g2_ref[...], masks, nb=nb, c=c, chain=chain)
    for pi in range(n_prob):
        for kk in range(chain):
            y_ref[pi, kk * g_rows:(kk + 1) * g_rows, :] = ys[pi * chain + kk]
        for s in range(nb):
            s_scr[pi * nb + s] = s_new[pi][s]
            half = s_new[pi][s][:, :2 * HEAD_W] + s_new[pi][s][:, 2 * HEAD_W:]
            so_ref[pi * nb + s] = half[:, :HEAD_W] + half[:, HEAD_W:]


def _rwkv(p3, pe, s0, mu, par, w2p, a2p, g2p, n_prob, nb, c, chain):
    n_grp, lt, _ = p3.shape
    pe_rows = pe.shape[1]
    n_state = n_prob * nb
    blk_rows = chain * RW_ROWS
    return pl.pallas_call(
        functools.partial(_rwkv_body, n_prob=n_prob, nb=nb, c=c, chain=chain),
        grid=(n_grp // n_prob, lt // blk_rows),
        in_specs=[
            pl.BlockSpec((n_prob, blk_rows, RW_COLS), lambda i, j: (i, j, 0)),
            pl.BlockSpec((n_prob, pe_rows, RW_COLS), lambda i, j: (i, 0, 0)),
            pl.BlockSpec((n_state, MIX_W, HEAD_W), lambda i, j: (i, 0, 0)),
            _const_spec(mu.shape), _const_spec(par.shape),
            _const_spec(w2p.shape), _const_spec(a2p.shape), _const_spec(g2p.shape),
        ],
        out_specs=[
            pl.BlockSpec((n_prob, blk_rows, MIX_W), lambda i, j: (i, j, 0)),
            pl.BlockSpec((n_state, MIX_W, HEAD_W), lambda i, j: (i, 0, 0)),
        ],
        out_shape=[jax.ShapeDtypeStruct((n_grp, lt, MIX_W), F32),
                   jax.ShapeDtypeStruct((n_grp * nb, MIX_W, HEAD_W), F32)],
        scratch_shapes=[pltpu.VMEM((n_state, MIX_W, MIX_W), F32), pltpu.VMEM((n_prob, 8, RW_COLS), F32)],
        compiler_params=_cparams("parallel", "arbitrary"),
        name="rwkv7",
    )(p3, pe, s0, mu, par, w2p, a2p, g2p)


def _merge_body(x_ref, ya_ref, yc_ref, hc_ref, cp_ref, e0_ref, e1_ref, hs_ref, cw_ref, sln_ref, wm_ref, sb_ref,
                wg_ref, bg_ref, wb_ref, wo_ref, ln_ref, o_ref, z_ref, sv_ref, *, tm, seg, tiles_per_seq):
    i = pl.program_id(0)
    x = x_ref[...]
    hc = hc_ref[...]
    gb = hc[:, 0:MIX_W]
    z = hc[:, MIX_W:2 * MIX_W] * hc[:, 2 * MIX_W:3 * MIX_W]
    row = _iota((tm, MIX_W), 0)
    z1 = pltpu.roll(z, 1, 0)
    z2 = pltpu.roll(z, 2, 0)
    if seg >= tm:
        cp = cp_ref[...]
        zp = cp[:, MIX_W:2 * MIX_W] * cp[:, 2 * MIX_W:3 * MIX_W]
        zp = jnp.where(i % tiles_per_seq == 0, jnp.zeros_like(zp), zp)
        e1 = zp[7:8, :]
        e0 = zp[6:7, :]
        pos = row
    else:
        e1 = e1_ref[...]
        e0 = e0_ref[...]
        pos = row & (seg - 1)
    z1 = jnp.where(pos == 0, e1, z1)
    z2 = jnp.where(pos == 0, e0, jnp.where(pos == 1, e1, z2))
    cw = cw_ref[...]
    yb = gb * (z2 * cw[0:1, :] + z1 * cw[1:2, :] + z * cw[2:3, :])
    z_ref[...] = z[tm - z_ref.shape[0]:, :]

    hs = _gelu_tanh(hs_ref[...])
    u = hs[:, 0:MIX_W]
    sln = sln_ref[...]
    sv = _ln_rows(hs[:, MIX_W:], sln[0:1, :], sln[1:2, :])
    sv_ref[...] = sv
    t_i = _iota((SGU_CHUNK, N_HEADS * SGU_CHUNK), 0)
    s_i = _iota((SGU_CHUNK, N_HEADS * SGU_CHUNK), 1) & (SGU_CHUNK - 1)
    log_seg = int(math.log2(min(seg, SGU_CHUNK)))
    keep = (s_i <= t_i) & ((s_i >> log_seg) == (t_i >> log_seg))
    wm = jnp.where(keep, wm_ref[...], 0.0).astype(BF16)
    gmask = (_iota((N_HEADS * SGU_CHUNK, MIX_W), 0) >> 7) == (_iota((N_HEADS * SGU_CHUNK, MIX_W), 1) >> 6)
    sbias = sb_ref[...]
    svb = sv.astype(BF16)
    yd_parts = []
    for ck in range(tm // SGU_CHUNK):
        v_c = svb[ck * SGU_CHUNK:(ck + 1) * SGU_CHUNK]
        mixed = jnp.dot(wm, _stack_heads(v_c, gmask), preferred_element_type=F32) + sbias
        yd_parts.append(u[ck * SGU_CHUNK:(ck + 1) * SGU_CHUNK] * mixed)
    yd = jnp.concatenate(yd_parts, axis=0)

    xb = x.astype(BF16)
    branches = (ya_ref[...], yb, yc_ref[...], yd)
    mix = jnp.zeros((tm, D_MODEL), F32)
    for br in range(4):
        gate = _sigmoid(jnp.dot(xb, wg_ref[:, br * D_MODEL:(br + 1) * D_MODEL], preferred_element_type=F32)
                        + bg_ref[:, br * D_MODEL:(br + 1) * D_MODEL])
        proj = jnp.dot(branches[br].astype(BF16), wb_ref[br], preferred_element_type=F32)
        mix = mix + gate * proj
    ln = ln_ref[...]
    o_ref[...] = _ln_rows(ALPHA * x + jnp.dot(mix.astype(BF16), wo_ref[...], preferred_element_type=F32),
                          ln[0:1, :], ln[1:2, :])


def _merge(x, ya, yc, h_conv, e0, e1, h_sgu, conv_w, sgu_ln, wm, sgu_bias, wg, bgate, wb, wo, ln1, tm, seg):
    t = x.shape[0]
    prompt = seg >= tm
    tiles_per_seq = max(seg // tm, 1)
    z_rows = 8 if prompt else tm
    n_seq = t // seg if prompt else 0
    if prompt:
        cp_spec = pl.BlockSpec((8, 3 * MIX_W), lambda i: (jnp.maximum(i * (tm // 8) - 1, 0), 0))
        e_spec = _const_spec(e0.shape)
        z_spec = pl.BlockSpec((8, MIX_W), lambda i: (i // tiles_per_seq, 0))
        z_shape = jax.ShapeDtypeStruct((n_seq * 8, MIX_W), F32)
    else:
        cp_spec = pl.BlockSpec((8, 3 * MIX_W), lambda i: (0, 0))
        e_spec = pl.BlockSpec((tm, MIX_W), lambda i: (i, 0))
        z_spec = pl.BlockSpec((tm, MIX_W), lambda i: (i, 0))
        z_shape = jax.ShapeDtypeStruct((t, MIX_W), F32)
    row = lambda w: pl.BlockSpec((tm, w), lambda i: (i, 0))
    return pl.pallas_call(
        functools.partial(_merge_body, tm=tm, seg=seg, tiles_per_seq=tiles_per_seq),
        grid=(t // tm,),
        in_specs=[row(D_MODEL), row(MIX_W), row(MIX_W), row(3 * MIX_W), cp_spec, e_spec, e_spec, row(2 * MIX_W),
                  _const_spec(conv_w.shape), _const_spec(sgu_ln.shape), _const_spec(wm.shape),
                  _const_spec(sgu_bias.shape), _const_spec(wg.shape), _const_spec(bgate.shape),
                  _const_spec(wb.shape), _const_spec(wo.shape), _const_spec(ln1.shape)],
        out_specs=[row(D_MODEL), z_spec, row(MIX_W)],
        out_shape=[jax.ShapeDtypeStruct((t, D_MODEL), F32), z_shape, jax.ShapeDtypeStruct((t, MIX_W), F32)],
        compiler_params=_cparams("arbitrary"),
        name="merge",
    )(x, ya, yc, h_conv, h_conv, e0, e1, h_sgu, conv_w, sgu_ln, wm, sgu_bias, wg, bgate, wb, wo, ln1)


def _matmul2_body(x_ref, w1_ref, w2_ref, o1_ref, o2_ref):
    xb = x_ref[...].astype(BF16)
    o1_ref[...] = jnp.dot(xb, w1_ref[...], preferred_element_type=F32).astype(o1_ref.dtype)
    o2_ref[...] = jnp.dot(xb, w2_ref[...], preferred_element_type=F32).astype(o2_ref.dtype)


def _matmul2(x, w1, w2, out_dtype, tm):
    t, kdim = x.shape
    n1, n2 = w1.shape[1], w2.shape[1]
    return pl.pallas_call(
        _matmul2_body,
        grid=(t // tm,),
        in_specs=[pl.BlockSpec((tm, kdim), lambda i: (i, 0)), _const_spec(w1.shape), _const_spec(w2.shape)],
        out_specs=[pl.BlockSpec((tm, n1), lambda i: (i, 0)), pl.BlockSpec((tm, n2), lambda i: (i, 0))],
        out_shape=[jax.ShapeDtypeStruct((t, n1), out_dtype), jax.ShapeDtypeStruct((t, n2), out_dtype)],
        compiler_params=_cparams("parallel"),
        name="matmul2",
    )(x, w1, w2)


def _matmul_body(x_ref, w_ref, o_ref):
    o_ref[...] = jnp.dot(x_ref[...].astype(BF16), w_ref[...], preferred_element_type=F32).astype(o_ref.dtype)


def _matmul(x, w, out_dtype, tm):
    t, kdim = x.shape
    n = w.shape[1]
    return pl.pallas_call(
        _matmul_body,
        grid=(t // tm,),
        in_specs=[pl.BlockSpec((tm, kdim), lambda i: (i, 0)), _const_spec(w.shape)],
        out_specs=pl.BlockSpec((tm, n), lambda i: (i, 0)),
        out_shape=jax.ShapeDtypeStruct((t, n), out_dtype),
        compiler_params=_cparams("parallel"),
        name="matmul",
    )(x, w)


def _softmax_rows(sc):
    m = jnp.max(sc, axis=-1, keepdims=True)
    e = jnp.exp(sc - m)
    return e / jnp.sum(e, axis=-1, keepdims=True)


def _xattn_rows_body(q_ref, k_ref, v_ref, o_ref, *, tq, slots):
    rows = X_HEADS * tq
    n_col = N_MEM * X_HEADS
    own = (_iota((rows, n_col), 1) & (X_HEADS - 1)) == (_iota((rows, n_col), 0) >> int(math.log2(tq)))
    for s in range(slots):
        q = q_ref[s * tq:(s + 1) * tq, :]
        qs = jnp.concatenate([q[:, h * X_HD:(h + 1) * X_HD] for h in range(X_HEADS)], axis=0).astype(BF16)
        k_all = k_ref[s].reshape(n_col, X_HD).astype(BF16)
        v_all = v_ref[s].reshape(n_col, X_HD).astype(BF16)
        sc = lax.dot_general(qs, k_all, (((1,), (1,)), ((), ())), preferred_element_type=F32) * (X_HD ** -0.5)
        pr = _softmax_rows(jnp.where(own, sc, -1e30))
        o = jnp.dot(pr.astype(BF16), v_all, preferred_element_type=F32)
        for h in range(X_HEADS):
            o_ref[s * tq:(s + 1) * tq, h * X_HD:(h + 1) * X_HD] = o[h * tq:(h + 1) * tq].astype(o_ref.dtype)


def _xattn_rows(q, mem_k, mem_v, layer, tq, slots):
    t = q.shape[0]
    rows = tq * slots
    mem_spec = pl.BlockSpec((None, slots, N_MEM, X_HEADS, X_HD), lambda i: (layer, i, 0, 0, 0))
    return pl.pallas_call(
        functools.partial(_xattn_rows_body, tq=tq, slots=slots),
        grid=(t // rows,),
        in_specs=[pl.BlockSpec((rows, D_MODEL), lambda i: (i, 0)), mem_spec, mem_spec],
        out_specs=pl.BlockSpec((rows, D_MODEL), lambda i: (i, 0)),
        out_shape=jax.ShapeDtypeStruct((t, D_MODEL), q.dtype),
        compiler_params=_cparams("parallel"),
        name="xattn_rows",
    )(q, mem_k, mem_v)


def _proj_ln_body(y_ref, w_ref, x_ref, ln_ref, o_ref):
    ln = ln_ref[...]
    acc = jnp.dot(y_ref[...].astype(BF16), w_ref[...], preferred_element_type=F32)
    o_ref[...] = _ln_rows(ALPHA * x_ref[...] + acc, ln[0:1, :], ln[1:2, :])


def _proj_ln(y, w, x, ln, tm):
    t = x.shape[0]
    return pl.pallas_call(
        _proj_ln_body,
        grid=(t // tm,),
        in_specs=[pl.BlockSpec((tm, D_MODEL), lambda i: (i, 0)), _const_spec(w.shape),
                  pl.BlockSpec((tm, D_MODEL), lambda i: (i, 0)), _const_spec(ln.shape)],
        out_specs=pl.BlockSpec((tm, D_MODEL), lambda i: (i, 0)),
        out_shape=jax.ShapeDtypeStruct((t, D_MODEL), F32),
        compiler_params=_cparams("parallel"),
        name="proj_ln",
    )(y, w, x, ln)


def _mlp_rows(x, wu_ref, wd_ref, ln):
    xb = x.astype(BF16)
    acc = jnp.zeros(x.shape, F32)
    for j in range(D_FF // D_MODEL):
        cols = slice(j * D_MODEL, (j + 1) * D_MODEL)
        hid = jnp.maximum(jnp.dot(xb, wu_ref[:, cols], preferred_element_type=F32), 0.0)
        acc = acc + jnp.dot((hid * hid).astype(BF16), wd_ref[cols, :], preferred_element_type=F32)
    return _ln_rows(ALPHA * x + acc, ln[0:1, :], ln[1:2, :])


def _mlp_body(x_ref, wu_ref, wd_ref, ln_ref, o_ref):
    o_ref[...] = _mlp_rows(x_ref[...], wu_ref, wd_ref, ln_ref[...])


def _mlp(x, wu, wd, ln, tm):
    t = x.shape[0]
    return pl.pallas_call(
        _mlp_body,
        grid=(t // tm,),
        in_specs=[pl.BlockSpec((tm, D_MODEL), lambda i: (i, 0)), _const_spec(wu.shape), _const_spec(wd.shape),
                  _const_spec(ln.shape)],
        out_specs=pl.BlockSpec((tm, D_MODEL), lambda i: (i, 0)),
        out_shape=jax.ShapeDtypeStruct((t, D_MODEL), F32),
        compiler_params=_cparams("parallel"),
        name="mlp",
    )(x, wu, wd, ln)


def _layer_params(l, w_in, sb_bias, w_gate, b_gate, w_branch, w_o, conv_w, rw_mu, rw_w0, rw_w2, rw_a0, rw_a2, rw_g2,
                  rw_kk, rw_ka, rw_rk, rw_gn_g, rw_gn_b, sgu_ln_g, sgu_ln_b, sgu_ws, sgu_b, w_mq, w_mk, w_mv, w_mo,
                  w_up, w_down, ln1_g, ln1_b, ln2_g, ln2_b, ln3_g, ln3_b):
    wi = w_in[l].astype(BF16)
    off_b, off_c, off_d = 3 * MIX_W, 6 * MIX_W, 6 * MIX_W + RW_COLS
    zpad = lambda w, r0: jnp.zeros((128, MIX_W), F32).at[r0:r0 + w.shape[0]].set(w).astype(BF16)
    par = jnp.zeros((8, MIX_W), F32)
    for i, vec in enumerate((rw_w0[l], rw_a0[l], rw_kk[l], rw_ka[l], rw_rk[l].reshape(MIX_W), rw_gn_g[l], rw_gn_b[l])):
        par = par.at[i].set(vec)
    return dict(
        w_qkv=wi[:, :off_b], w_conv=wi[:, off_b:off_c], w_rw=wi[:, off_c:off_d], w_sgu=wi[:, off_d:],
        sb_bias=sb_bias[l],
        w_gate=w_gate[l].astype(BF16), b_gate=b_gate[l].reshape(1, -1), w_branch=w_branch[l].astype(BF16),
        w_o=w_o[l].astype(BF16), conv_w=jnp.zeros((8, MIX_W), F32).at[:3].set(conv_w[l]),
        rw_mu=rw_mu[l].reshape(1, RW_COLS), rw_par=par,
        rw_w2=zpad(rw_w2[l], 0), rw_a2=zpad(rw_a2[l], 32), rw_g2=zpad(rw_g2[l], 64),
        sgu_ln=jnp.stack([sgu_ln_g[l], sgu_ln_b[l]]), sgu_ws=sgu_ws[l], sgu_b=sgu_b[l],
        w_mq=w_mq[l].astype(BF16), w_mk=w_mk[l].astype(BF16), w_mv=w_mv[l].astype(BF16), w_mo=w_mo[l].astype(BF16),
        w_up=w_up[l].astype(BF16), w_down=w_down[l].astype(BF16),
        ln1=jnp.stack([ln1_g[l], ln1_b[l]]), ln2=jnp.stack([ln2_g[l], ln2_b[l]]), ln3=jnp.stack([ln3_g[l], ln3_b[l]]),
    )


def _sgu_tables(lp, seg):
    ws = lp['sgu_ws']
    sb = lp['sgu_b']
    if seg < SGU_CHUNK:
        reps = SGU_CHUNK // seg
        ws = jnp.tile(ws[:, :seg, :seg], (1, reps, reps))
        sb = jnp.tile(sb[:, :seg], (1, reps))
    wm = jnp.transpose(ws, (1, 0, 2)).reshape(SGU_CHUNK, N_HEADS * SGU_CHUNK)
    bias = jnp.repeat(sb.T, HEAD_W, axis=1)
    return wm, bias


def _sb_bias_rows(bias, tq, width):
    return jnp.broadcast_to(jnp.repeat(bias, tq)[:, None], (N_HEADS * tq, width)).astype(F32)


def _tail_fused_body(x_ref, wq_ref, k_ref, v_ref, wo_ref, ln2_ref, wu_ref, wd_ref, ln3_ref, o_ref):
    x = x_ref[...]
    q = jnp.dot(x.astype(BF16), wq_ref[...], preferred_element_type=F32).astype(BF16)
    kb = k_ref[...].astype(BF16)
    vb = v_ref[...].astype(BF16)
    ctx = []
    for h in range(X_HEADS):
        cols = slice(h * X_HD, (h + 1) * X_HD)
        sc = lax.dot_general(q[:, cols], kb[:, cols], (((1,), (1,)), ((), ())),
                             preferred_element_type=F32) * (X_HD ** -0.5)
        ctx.append(jnp.dot(_softmax_rows(sc).astype(BF16), vb[:, cols], preferred_element_type=F32).astype(BF16))
    att = jnp.concatenate(ctx, axis=1)
    ln2 = ln2_ref[...]
    x2 = _ln_rows(ALPHA * x + jnp.dot(att, wo_ref[...], preferred_element_type=F32), ln2[0:1, :], ln2[1:2, :])
    o_ref[...] = _mlp_rows(x2, wu_ref, wd_ref, ln3_ref[...])


def _tail_fused(x, lp, mem_k, mem_v, tm):
    t = x.shape[0]
    tiles_per_mem = t // mem_k.shape[0] // tm
    mem_spec = pl.BlockSpec((None, N_MEM, D_MODEL), lambda i: (i // tiles_per_mem, 0, 0))
    weights = (lp['w_mq'], lp['w_mo'], lp['ln2'], lp['w_up'], lp['w_down'], lp['ln3'])
    wq, wo, ln2, wu, wd, ln3 = (_const_spec(w.shape) for w in weights)
    return pl.pallas_call(
        _tail_fused_body,
        grid=(t // tm,),
        in_specs=[pl.BlockSpec((tm, D_MODEL), lambda i: (i, 0)), wq, mem_spec, mem_spec, wo, ln2, wu, wd, ln3],
        out_specs=pl.BlockSpec((tm, D_MODEL), lambda i: (i, 0)),
        out_shape=jax.ShapeDtypeStruct((t, D_MODEL), F32),
        compiler_params=_cparams("parallel"),
        name="tail",
    )(x, lp['w_mq'], mem_k, mem_v, lp['w_mo'], lp['ln2'], lp['w_up'], lp['w_down'], lp['ln3'])


def _tail(x, lp, attend, q_dtype, tm):
    qm = _matmul(x, lp['w_mq'], q_dtype, tm)
    x = _proj_ln(attend(qm), lp['w_mo'], x, lp['ln2'], tm)
    return _mlp(x, lp['w_up'], lp['w_down'], lp['ln3'], tm)


def _layer_prompt(x, lp, mem_k, mem_v, n, seq):
    t = n * seq
    h_qkv, h_conv, h_rw, h_sgu = _inproj(x, lp['w_qkv'], lp['w_conv'], lp['w_rw'], lp['w_sgu'], 512)
    tq = 256
    ya = _sb_prompt(h_qkv, _sb_bias_rows(lp['sb_bias'], tq, tq), n, seq, tq)
    pe = jnp.zeros((n, 8, RW_COLS), F32)
    s0 = jnp.zeros((n, MIX_W, HEAD_W), F32)
    yc, s_fin = _rwkv(h_rw.reshape(n, seq, RW_COLS), pe, s0, lp['rw_mu'], lp['rw_par'], lp['rw_w2'], lp['rw_a2'],
                      lp['rw_g2'], n_prob=n, nb=1, c=RW_ROWS, chain=2)
    wm, sgu_bias = _sgu_tables(lp, SGU_CHUNK)
    zero_e = jnp.zeros((8, MIX_W), F32)
    x1, z_tail, _ = _merge(x, ya, yc.reshape(t, MIX_W), h_conv, zero_e, zero_e, h_sgu, lp['conv_w'], lp['sgu_ln'],
                           wm, sgu_bias, lp['w_gate'], lp['b_gate'], lp['w_branch'], lp['w_o'], lp['ln1'],
                           tm=512, seg=seq)
    x3 = _tail_fused(x1, lp, mem_k, mem_v, 512)
    k_new = h_qkv[:, MIX_W:2 * MIX_W].reshape(n, seq, N_HEADS, HEAD_W)
    v_new = h_qkv[:, 2 * MIX_W:].reshape(n, seq, N_HEADS, HEAD_W)
    conv_new = z_tail.reshape(n, 8, MIX_W)[:, 6:8]
    shift_new = h_rw.reshape(n, seq, RW_COLS)[:, -1]
    return x3, k_new, v_new, conv_new, shift_new, s_fin.reshape(n, N_HEADS, HEAD_W, HEAD_W)


def _layer_sample(x, lp, layer, mem_k, mem_v, cache_kt, cache_vt, page_table, state_conv, state_shift, state_wkv,
                  nb, seq):
    t = nb * seq
    h_qkv, h_conv, h_rw, h_sgu = _inproj(x, lp['w_qkv'], lp['w_conv'], lp['w_rw'], lp['w_sgu'], 512)
    k_new = h_qkv[:, MIX_W:2 * MIX_W].reshape(nb, seq, MIX_W)
    v_new = h_qkv[:, 2 * MIX_W:].reshape(nb, seq, MIX_W)
    ya = _sb_sample(h_qkv, cache_kt, cache_vt, page_table, _sb_bias_rows(lp['sb_bias'], seq, PAGE_SIZE),
                    layer, nb, seq)
    per = RW_ROWS // seq
    pe = jnp.repeat(state_shift, seq, axis=0).reshape(nb // per, RW_ROWS, RW_COLS)
    yc, s_fin = _rwkv(h_rw.reshape(nb // per, RW_ROWS, RW_COLS), pe, state_wkv.reshape(nb, MIX_W, HEAD_W), lp['rw_mu'],
                      lp['rw_par'], lp['rw_w2'], lp['rw_a2'], lp['rw_g2'], n_prob=2, nb=per, c=seq, chain=1)
    wm, sgu_bias = _sgu_tables(lp, seq)
    e0 = jnp.repeat(state_conv[:, 0], seq, axis=0)
    e1 = jnp.repeat(state_conv[:, 1], seq, axis=0)
    x1, z_all, sgu_v = _merge(x, ya, yc.reshape(t, MIX_W), h_conv, e0, e1, h_sgu, lp['conv_w'], lp['sgu_ln'], wm,
                              sgu_bias, lp['w_gate'], lp['b_gate'], lp['w_branch'], lp['w_o'], lp['ln1'],
                              tm=512, seg=seq)
    x3 = _tail(x1, lp, lambda qm: _xattn_rows(qm, mem_k, mem_v, layer, seq, 4), F32, 512)
    conv_new = z_all.reshape(nb, seq, MIX_W)[:, seq - 2:]
    shift_new = h_rw.reshape(nb, seq, RW_COLS)[:, -1]
    return (x3, k_new.reshape(nb, seq, N_HEADS, HEAD_W), v_new.reshape(nb, seq, N_HEADS, HEAD_W), conv_new,
            shift_new, s_fin.reshape(nb, N_HEADS, HEAD_W, HEAD_W), sgu_v.reshape(nb, seq, MIX_W))


def kernel(x_prompt, x_sample, mem_prompt, cache_k, cache_v, page_table, cache_mem_k, cache_mem_v, state_conv,
           state_wkv, state_shift, w_in, sb_bias, w_gate, b_gate, w_branch, w_o, conv_w, rw_mu, rw_w0, rw_w2, rw_a0,
           rw_a2, rw_g2, rw_kk, rw_ka, rw_rk, rw_gn_g, rw_gn_b, sgu_ln_g, sgu_ln_b, sgu_ws, sgu_b, w_mq, w_mk, w_mv,
           w_mo, w_up, w_down, ln1_g, ln1_b, ln2_g, ln2_b, ln3_g, ln3_b):
    n_p, seq_p, _ = x_prompt.shape
    n_s, seq_s, _ = x_sample.shape
    n_phys = cache_k.shape[1]
    xp = x_prompt.reshape(n_p * seq_p, D_MODEL)
    xs = x_sample.reshape(n_s * seq_s, D_MODEL)
    mem2d = mem_prompt.reshape(n_p * N_MEM, D_MODEL)
    cache_kt = jnp.transpose(cache_k, (0, 1, 3, 4, 2)).reshape(DEPTH, n_phys, MIX_W, PAGE_SIZE)
    cache_vt = jnp.transpose(cache_v, (0, 1, 3, 4, 2)).reshape(DEPTH, n_phys, MIX_W, PAGE_SIZE)
    outs = [[] for _ in range(13)]
    for l in range(DEPTH):
        lp = _layer_params(l, w_in, sb_bias, w_gate, b_gate, w_branch, w_o, conv_w, rw_mu, rw_w0, rw_w2, rw_a0,
                           rw_a2, rw_g2, rw_kk, rw_ka, rw_rk, rw_gn_g, rw_gn_b, sgu_ln_g, sgu_ln_b, sgu_ws, sgu_b,
                           w_mq, w_mk, w_mv, w_mo, w_up, w_down, ln1_g, ln1_b, ln2_g, ln2_b, ln3_g, ln3_b)
        mk, mv = _matmul2(mem2d, lp['w_mk'], lp['w_mv'], F32, 512)
        mk4 = mk.reshape(n_p, N_MEM, X_HEADS, X_HD)
        mv4 = mv.reshape(n_p, N_MEM, X_HEADS, X_HD)
        xp, pk, pv, pc, psh, pst = _layer_prompt(xp, lp, mk.reshape(n_p, N_MEM, D_MODEL),
                                                 mv.reshape(n_p, N_MEM, D_MODEL), n_p, seq_p)
        xs, sk, sv, sc, ssh, sst, scv = _layer_sample(
            xs, lp, l, cache_mem_k, cache_mem_v, cache_kt, cache_vt, page_table,
            state_conv[l], state_shift[l], state_wkv[l], n_s, seq_s)
        for lst, val in zip(outs, (pk, pv, mk4, mv4, pc, pst, psh, sk, sv, sc, sst, ssh, scv)):
            lst.append(val)
    return (xp.reshape(n_p, seq_p, D_MODEL), xs.reshape(n_s, seq_s, D_MODEL)) + tuple(jnp.stack(o) for o in outs)
```

```python
import functools
import math

import jax
import jax.numpy as jnp
from jax import lax
from jax.experimental import pallas as pl
from jax.experimental.pallas import tpu as pltpu

F32 = jnp.float32
BF16 = jnp.bfloat16

D_MODEL = 1024
DEPTH = 2
MIX_W = 256
HEAD_W = 64
N_HEADS = MIX_W // HEAD_W
RW_COLS = 896
PAGE_SIZE = 128
N_MEM = 256
X_HEADS = 4
X_HD = D_MODEL // X_HEADS
D_FF = 4 * D_MODEL
ALPHA = (2 * DEPTH) ** 0.25
LN_EPS = 1e-5
RW_GN_EPS = 64e-5
SGU_CHUNK = 128
RW_ROWS = 64
VMEM_LIMIT = 56 * 1024 * 1024


def _cparams(*sem):
    return pltpu.CompilerParams(dimension_semantics=sem, vmem_limit_bytes=VMEM_LIMIT)


def _const_spec(shape):
    nd = len(shape)
    return pl.BlockSpec(shape, lambda *_: (0,) * nd, pipeline_mode=pl.Buffered(1))


def _bdot(a, b):
    return jnp.dot(a.astype(BF16), b.astype(BF16), preferred_element_type=F32)


def _bdot_nt(a, b):
    return lax.dot_general(a.astype(BF16), b.astype(BF16), (((1,), (1,)), ((), ())),
                           preferred_element_type=F32)


def _bdot_tn(a, b):
    return lax.dot_general(a.astype(BF16), b.astype(BF16), (((0,), (0,)), ((), ())),
                           preferred_element_type=F32)


def _split2(x):
    hi = x.astype(BF16)
    lo = (x - hi.astype(F32)).astype(BF16)
    return hi, lo


def _split3(x):
    hi = x.astype(BF16)
    r1 = x - hi.astype(F32)
    mid = r1.astype(BF16)
    lo = (r1 - mid.astype(F32)).astype(BF16)
    return hi, mid, lo


def _sel_dot_l(sel_b, x):
    return sum(jnp.dot(sel_b, part, preferred_element_type=F32) for part in _split3(x))


def _sel_dot_r(x, sel_b):
    return sum(jnp.dot(part, sel_b, preferred_element_type=F32) for part in _split2(x))


def _dot3(a, b):
    ah, al = _split2(a)
    bh, bl = _split2(b)
    d = lambda x, y: jnp.dot(x, y, preferred_element_type=F32)
    return d(ah, bh) + d(ah, bl) + d(al, bh)


def _sigmoid(x):
    return 1.0 / (1.0 + jnp.exp(-x))


def _softplus(x):
    return jnp.maximum(x, 0.0) + jnp.log(1.0 + jnp.exp2(jnp.abs(x) * (-1.0 / math.log(2.0))))


def _gelu_tanh(x):
    return 0.5 * x * (1.0 + jnp.tanh(0.7978845608028654 * (x + 0.044715 * (x * x * x))))


def _ln_rows(x, g, b, eps=LN_EPS):
    mu = jnp.mean(x, axis=-1, keepdims=True)
    xc = x - mu
    var = jnp.mean(xc * xc, axis=-1, keepdims=True)
    return xc * lax.rsqrt(var + eps) * g + b


def _iota(shape, dim):
    return lax.broadcasted_iota(jnp.int32, shape, dim)


def _head_block_mask(rows, row_shift):
    return (_iota((rows, MIX_W), 0) >> row_shift) == (_iota((rows, MIX_W), 1) >> 6)


def _stack_heads(x, mask):
    t = jnp.concatenate([x] * N_HEADS, axis=0)
    return jnp.where(mask, t, jnp.zeros_like(t))


def _unstack_heads(x, rows):
    return x[0:rows] + x[rows:2 * rows] + x[2 * rows:3 * rows] + x[3 * rows:4 * rows]


def _inproj_body(x_ref, *refs):
    n = len(refs) // 2
    xb = x_ref[...].astype(BF16)
    for w_ref, o_ref in zip(refs[:n], refs[n:]):
        o_ref[...] = jnp.dot(xb, w_ref[...], preferred_element_type=F32)


def _inproj(x, weights, tm):
    t = x.shape[0]
    widths = [w.shape[1] for w in weights]
    return pl.pallas_call(
        _inproj_body,
        grid=(t // tm,),
        in_specs=[pl.BlockSpec((tm, D_MODEL), lambda i: (i, 0))] + [_const_spec(w.shape) for w in weights],
        out_specs=[pl.BlockSpec((tm, w), lambda i: (i, 0)) for w in widths],
        out_shape=[jax.ShapeDtypeStruct((t, w), F32) for w in widths],
        compiler_params=_cparams("parallel"),
        name="inproj",
    )(x, *weights)


def _suffix_ones(tk):
    return (_iota((tk, tk), 0) >= _iota((tk, tk), 1)).astype(BF16)


def _sb_weights(z, carry, suffix, mask):
    tk = z.shape[1]
    stop = _softplus(z)
    if mask is not None:
        stop = jnp.where(mask, stop, 0.0)
    cs = jnp.dot(stop.astype(BF16), suffix, preferred_element_type=F32)
    later = carry if tk == 128 else jnp.concatenate([carry] * (tk // 128), axis=1)
    w = jnp.exp(z - cs - later)
    if mask is not None:
        w = jnp.where(mask, w, 0.0)
    total = jnp.sum(stop, axis=1, keepdims=True)
    return w, carry + jnp.broadcast_to(total, carry.shape)


def _sb_prompt_body(q_ref, k_ref, v_ref, bias_ref, o_ref, kb, vb, carry_ref, acc_ref, *, tq):
    qi = pl.program_id(1)
    rows = N_HEADS * tq

    @pl.when(qi == 0)
    def _():
        kb[...] = k_ref[...].astype(BF16)
        vb[...] = v_ref[...].astype(BF16)

    hmask = _head_block_mask(rows, int(math.log2(tq)))
    qs = _stack_heads((q_ref[...] * (HEAD_W ** -0.5)).astype(BF16), hmask)
    suffix = _suffix_ones(tq)

    def block(start, carry, acc, mask):
        z = lax.dot_general(qs, kb[pl.ds(start, tq), :], (((1,), (1,)), ((), ())),
                            preferred_element_type=F32) + bias_ref[...]
        w, carry = _sb_weights(z, carry, suffix, mask)
        acc = acc + jnp.dot(w.astype(BF16), vb[pl.ds(start, tq), :], preferred_element_type=F32)
        return carry, acc

    def block_pair(start):
        k2 = kb[pl.ds(start, 2 * tq), :]
        v2 = vb[pl.ds(start, 2 * tq), :]
        hrows = lambda h: slice(h * tq, (h + 1) * tq)
        logits = lambda h: lax.dot_general(qs[hrows(h)], k2, (((1,), (1,)), ((), ())),
                                           preferred_element_type=F32)
        z2 = [logits(0), logits(1)]
        for h in range(N_HEADS):
            bias = bias_ref[hrows(h), :]
            z = jnp.concatenate([z2[h][:, tq:] + bias, z2[h][:, :tq] + bias], axis=0)
            stop = _softplus(z)
            cs = jnp.dot(stop.astype(BF16), suffix, preferred_element_type=F32)
            if h + 2 < N_HEADS:
                z2.append(logits(h + 2))
            total = jnp.broadcast_to(jnp.sum(stop, axis=1, keepdims=True), (2 * tq, 128))
            carry = carry_ref[hrows(h), :]
            carry2 = jnp.concatenate([carry, carry + total[:tq]], axis=0)
            later = jnp.concatenate([carry2] * (tq // 128), axis=1)
            w = jnp.exp(z - cs - later).astype(BF16)
            w2 = jnp.concatenate([w[tq:], w[:tq]], axis=1)
            acc_ref[hrows(h), :] += jnp.dot(w2, v2, preferred_element_type=F32)
            carry_ref[hrows(h), :] = carry2[tq:] + total[tq:]

    t_idx = _iota((rows, tq), 0) & (tq - 1)
    causal = _iota((rows, tq), 1) < t_idx
    carry, acc = block(pl.multiple_of(qi * tq, tq), jnp.zeros((rows, 128), F32),
                       jnp.zeros((rows, MIX_W), F32), causal)
    carry_ref[...] = carry
    acc_ref[...] = acc

    @pl.when((qi & 1) == 1)
    def _():
        c, a = block(pl.multiple_of((qi - 1) * tq, tq), carry_ref[...], acc_ref[...], None)
        carry_ref[...] = c
        acc_ref[...] = a

    n_pairs = qi >> 1

    def step(jj, _):
        block_pair(pl.multiple_of((n_pairs - 1 - jj) * (2 * tq), 2 * tq))
        return 0

    lax.fori_loop(0, n_pairs, step, 0)
    acc = jnp.where(hmask, acc_ref[...], 0.0)
    o_ref[...] = _unstack_heads(acc, tq)


def _sb_prompt(q, k, v, bias_rows, n, seq, tq):
    nq = seq // tq
    rows = N_HEADS * tq
    return pl.pallas_call(
        functools.partial(_sb_prompt_body, tq=tq),
        grid=(n, nq),
        in_specs=[
            pl.BlockSpec((tq, MIX_W), lambda b, i: (b * nq + i, 0)),
            pl.BlockSpec((seq, MIX_W), lambda b, i: (b, 0)),
            pl.BlockSpec((seq, MIX_W), lambda b, i: (b, 0)),
            _const_spec((rows, tq)),
        ],
        out_specs=pl.BlockSpec((tq, MIX_W), lambda b, i: (b * nq + i, 0)),
        out_shape=jax.ShapeDtypeStruct((n * seq, MIX_W), F32),
        scratch_shapes=[pltpu.VMEM((seq, MIX_W), BF16), pltpu.VMEM((seq, MIX_W), BF16),
                        pltpu.VMEM((rows, 128), F32), pltpu.VMEM((rows, MIX_W), F32)],
        compiler_params=_cparams("parallel", "arbitrary"),
        name="sb_prompt",
    )(q, k, v, bias_rows)


SB_SEQ_SLOTS = 3


def _sb_sample_body(pt_ref, q_ref, kn_ref, vn_ref, ck_ref, cv_ref, bias_ref, o_ref, kbuf, vbuf, sem,
                    *, layer, tq, n_pages):
    b = pl.program_id(0)
    rows = N_HEADS * tq
    n_blk = n_pages + 1

    def seq_copies(s):
        slot = lax.rem(s, SB_SEQ_SLOTS)
        out = []
        for j in range(n_pages):
            page = pt_ref[s, n_pages - 1 - j]
            out.append(pltpu.make_async_copy(ck_ref.at[layer, page], kbuf.at[slot, j], sem.at[slot]))
            out.append(pltpu.make_async_copy(cv_ref.at[layer, page], vbuf.at[slot, j], sem.at[slot]))
        return slot, out

    @pl.when(b == 0)
    def _():
        for s in range(SB_SEQ_SLOTS - 1):
            for cp in seq_copies(jnp.int32(s))[1]:
                cp.start()

    @pl.when(b + (SB_SEQ_SLOTS - 1) < pl.num_programs(0))
    def _():
        for cp in seq_copies(b + (SB_SEQ_SLOTS - 1))[1]:
            cp.start()

    slot, copies = seq_copies(b)
    for cp in copies:
        cp.wait()

    hmask = _head_block_mask(rows, int(math.log2(tq)))
    qs = _stack_heads(q_ref[...] * (HEAD_W ** -0.5), hmask).astype(BF16)
    bias = bias_ref[...]

    pad = jnp.zeros((PAGE_SIZE - tq, MIX_W), F32)
    k_new = jnp.concatenate([kn_ref[...], pad], axis=0).astype(BF16)
    v_new = jnp.concatenate([vn_ref[...], pad], axis=0).astype(BF16)
    z_new = lax.dot_general(qs, k_new, (((1,), (1,)), ((), ())), preferred_element_type=F32)
    k_pages = jnp.concatenate([kbuf[slot, j].astype(BF16) for j in range(n_pages)], axis=1)
    z_pages = jnp.dot(qs, k_pages, preferred_element_type=F32)
    z = jnp.concatenate([z_new] + [z_pages[:, j * PAGE_SIZE:(j + 1) * PAGE_SIZE] for j in range(n_pages)], axis=0)
    z = z + jnp.concatenate([bias] * n_blk, axis=0)

    r_i = _iota((n_blk * rows, PAGE_SIZE), 0)
    valid = (r_i >= rows) | (_iota((n_blk * rows, PAGE_SIZE), 1) < (r_i & (tq - 1)))
    stop = jnp.where(valid, _softplus(z), 0.0)
    cs = jnp.dot(stop.astype(BF16), _suffix_ones(PAGE_SIZE), preferred_element_type=F32)
    total = jnp.broadcast_to(jnp.sum(stop, axis=1, keepdims=True), (n_blk * rows, PAGE_SIZE))
    carry = [jnp.zeros((rows, PAGE_SIZE), F32)]
    for i in range(n_blk - 1):
        carry.append(carry[-1] + total[i * rows:(i + 1) * rows])
    w = jnp.where(valid, jnp.exp(z - cs - jnp.concatenate(carry, axis=0)), 0.0).astype(BF16)

    acc = jnp.dot(w[:rows], v_new, preferred_element_type=F32)
    w_pages = jnp.concatenate([w[(j + 1) * rows:(j + 2) * rows] for j in range(n_pages)], axis=1)
    v_pages = jnp.concatenate([vbuf[slot, j].astype(BF16) for j in range(n_pages)], axis=1)
    acc = acc + lax.dot_general(w_pages, v_pages, (((1,), (1,)), ((), ())), preferred_element_type=F32)
    o_ref[...] = _unstack_heads(jnp.where(hmask, acc, 0.0), tq)


def _sb_sample(q, k, v, cache_kt, cache_vt, page_table, bias_rows, layer, nb, tq):
    n_pages = page_table.shape[1]
    rows = N_HEADS * tq
    buf = (SB_SEQ_SLOTS, n_pages, MIX_W, PAGE_SIZE)
    grid_spec = pltpu.PrefetchScalarGridSpec(
        num_scalar_prefetch=1,
        grid=(nb,),
        in_specs=[
            pl.BlockSpec((tq, MIX_W), lambda b, pt: (b, 0)),
            pl.BlockSpec((tq, MIX_W), lambda b, pt: (b, 0)),
            pl.BlockSpec((tq, MIX_W), lambda b, pt: (b, 0)),
            pl.BlockSpec(memory_space=pl.ANY),
            pl.BlockSpec(memory_space=pl.ANY),
            pl.BlockSpec((rows, PAGE_SIZE), lambda b, pt: (0, 0)),
        ],
        out_specs=pl.BlockSpec((tq, MIX_W), lambda b, pt: (b, 0)),
        scratch_shapes=[pltpu.VMEM(buf, F32), pltpu.VMEM(buf, F32), pltpu.SemaphoreType.DMA((SB_SEQ_SLOTS,))],
    )
    return pl.pallas_call(
        functools.partial(_sb_sample_body, layer=layer, tq=tq, n_pages=n_pages),
        grid_spec=grid_spec,
        out_shape=jax.ShapeDtypeStruct((nb * tq, MIX_W), F32),
        compiler_params=_cparams("arbitrary"),
        name="sb_sample",
    )(page_table, q, k, v, cache_kt, cache_vt, bias_rows)


def _each(fn, *lists):
    return [fn(*args) for args in zip(*lists)]


def _rwkv_problems(ps, p_prevs, s_lists, mu, par, w2p, a2p, g2p, masks, *, nb, c, chain):
    ones_bd, bd, m_strict, m_incl, eye_cat, l_tri, l_all, eye_s = masks
    w0, a0, kk_s, ka_s, rk, gn_g, gn_b = (par[i:i + 1, :] for i in range(7))
    mm = lambda x, y: jnp.dot(x.astype(BF16), y.astype(BF16), preferred_element_type=F32)
    expand = lambda x: _stack_heads(x.astype(BF16), bd)
    left = lambda x: x[:, :MIX_W]
    right = lambda x: x[:, MIX_W:]

    n_p = len(ps)

    def rows_batched(fn, xs):
        out = fn(xs[0] if n_p == 1 else jnp.concatenate(xs, axis=0))
        return [out[i * RW_ROWS:(i + 1) * RW_ROWS] for i in range(n_p)]

    head_sums = lambda xs: rows_batched(lambda x: _sel_dot_r(x, ones_bd), xs)

    xs = _each(lambda p, pp: p + (pp - p) * mu, ps, p_prevs)
    r = [x[:, 0:256] for x in xs]
    k = [x[:, 256:512] for x in xs]
    v = [x[:, 512:768] for x in xs]
    lora = [x[:, 768:896] for x in xs]
    u = rows_batched(lambda x: w0 + mm(jnp.tanh(x), w2p), lora)
    logw = [(-math.exp(-0.5)) * _sigmoid(x) for x in u]
    asig = rows_batched(lambda x: _sigmoid(a0 + mm(x, a2p)), lora)
    gate = rows_batched(lambda x: mm(_sigmoid(x), g2p), lora)
    kk = [x * kk_s for x in k]
    ssq = head_sums([x * x for x in kk])
    kk = _each(lambda x, q: x / jnp.maximum(jnp.sqrt(q), 1e-12), kk, ssq)
    k_eff = _each(lambda x, s: x * (1.0 + (s - 1.0) * ka_s), k, asig)
    rk_sum = head_sums(_each(lambda x, y: x * y * rk, r, k_eff))
    bonus = _each(lambda x, y: x * y, rk_sum, v)
    b = _each(lambda x, s: x * s, kk, asig)

    sums = _sel_dot_l(jnp.concatenate([l_tri, l_all], axis=0), jnp.concatenate(logw, axis=1))
    cum = [sums[:RW_ROWS, i * MIX_W:(i + 1) * MIX_W] for i in range(n_p)]
    cum_all = [sums[RW_ROWS:, i * MIX_W:(i + 1) * MIX_W] for i in range(n_p)]
    rt = _each(lambda x, q: (x * jnp.exp(q)).astype(BF16), r, cum)
    at = _each(lambda x, q, lw: (-x * jnp.exp(q - lw)).astype(BF16), kk, cum, logw)
    ginv = [jnp.exp(-q) for q in cum]
    bt = _each(lambda x, g: x * g, b, ginv)
    kt = _each(lambda x, g: x * g, k_eff, ginv)
    e_tail = _each(lambda qa, q: jnp.exp(qa - q), cum_all, cum)
    bg = _each(lambda x, e: x * e, b, e_tail)
    kg = _each(lambda x, e: x * e, k_eff, e_tail)
    g_end = [jnp.exp(q) for q in cum_all]

    lhs = _each(lambda x, y: jnp.concatenate([x, y], axis=0), at, rt)
    rhs = _each(lambda x, y: jnp.concatenate([expand(x), expand(y)], axis=0), bt, kt)
    a_cat = _each(lambda x, y: lax.dot_general(x, y, (((1,), (1,)), ((), ())), preferred_element_type=F32),
                  lhs, rhs)
    a_ab = [jnp.where(m_strict, left(x[:RW_ROWS]), 0.0) for x in a_cat]
    a_ak = [jnp.where(m_strict, right(x[:RW_ROWS]), 0.0) for x in a_cat]
    a_rb = [jnp.where(m_incl, left(x[RW_ROWS:]), 0.0) for x in a_cat]
    a_rk = [jnp.where(m_incl, right(x[RW_ROWS:]), 0.0) for x in a_cat]

    inv = [eye_cat + x for x in a_ab]
    apow = a_ab
    apow_bd = [expand(x) for x in apow]
    span = 1
    while 2 * span < c:
        apow = _each(mm, apow, apow_bd)
        apow_bd = [expand(x) for x in apow]
        inv = _each(lambda x, y: x + mm(x, y), inv, apow_bd)
        span *= 2

    sv = [expand(x) for x in v]
    w1 = _each(mm, a_ak, sv)
    uu = _each(lambda i, x, y: mm(i, jnp.concatenate([expand(x), expand(y)], axis=1)), inv, at, w1)
    ua = [left(x) for x in uu]
    uv = [right(x) for x in uu]
    qy = _each(lambda m, x, y: mm(m, jnp.concatenate([expand(x), expand(y)], axis=1)), a_rb, ua, uv)
    y0 = _each(lambda q, m, x: right(q) + mm(m, x), qy, a_rk, sv)
    qe = _each(lambda x, q: x.astype(F32) + left(q), rt, qy)

    row_seq = _iota((RW_ROWS, MIX_W), 0) >> int(math.log2(c))
    m_c, n_c = [], []
    for s in range(nb):
        own = lambda x: x if nb == 1 else jnp.where(row_seq == s, x, 0.0)
        bg_s = [own(x) for x in bg]
        kg_s = [own(x) for x in kg]
        m_c.append(_each(lambda x, y, g: jnp.where(bd, _bdot_tn(x, y), 0.0)
                         + jnp.where(eye_s, g[s * c:s * c + 1, :], 0.0), ua, bg_s, g_end))
        n_c.append(_each(lambda x, y, z, w: jnp.where(bd, _bdot_tn(x, y) + _bdot_tn(z, w), 0.0),
                         uv, bg_s, v, kg_s))

    states = [list(sl) for sl in s_lists]
    y_parts = [[] for _ in ps]
    for kk in range(chain):
        idx = [g * chain + kk for g in range(len(s_lists))]
        pick = lambda xs: [xs[i] for i in idx]
        for s in range(nb):
            rows = slice(s * c, (s + 1) * c)
            cur = [st[s] for st in states]
            y_s = _each(lambda q, st, y: _bdot_nt(q[rows], st) + y[rows], pick(qe), cur, pick(y0))
            nxt = _each(lambda st, m, n: _dot3(st, m) + n, cur, pick(m_c[s]), pick(n_c[s]))
            for gi, i in enumerate(idx):
                y_parts[i].append(y_s[gi])
                states[gi][s] = nxt[gi]
    s_new = states
    y = [parts[0] if nb == 1 else jnp.concatenate(parts, axis=0) for parts in y_parts]

    mean = [x * (1.0 / HEAD_W) for x in head_sums(y)]
    d = _each(lambda x, m: x - m, y, mean)
    var = [x * (1.0 / HEAD_W) for x in head_sums([x * x for x in d])]
    out = _each(lambda x, q, bo, g: (x * lax.rsqrt(q + RW_GN_EPS) * gn_g + gn_b + bo) * g, d, var, bonus, gate)
    return out, s_new


def _rwkv_body(p_ref, pe_ref, s0_ref, mu_ref, par_ref, w2_ref, a2_ref, g2_ref, y_ref, so_ref, s_scr, plast_scr,
               *, n_prob, nb, c, chain):
    ci = pl.program_id(1)
    g_rows = RW_ROWS
    r4 = N_HEADS * g_rows
    log_c = int(math.log2(c))

    bd = _head_block_mask(r4, 6)

    @pl.when(ci == 0)
    def _():
        for idx in range(n_prob * nb):
            s_scr[idx] = jnp.where(bd, jnp.concatenate([s0_ref[idx]] * N_HEADS, axis=1), 0.0)
        if nb == 1:
            plast_scr[...] = pe_ref[...]

    ones_bd = bd.astype(BF16)
    t_i = _iota((g_rows, r4), 0)
    s_i = _iota((g_rows, r4), 1) & (g_rows - 1)
    same = (t_i >> log_c) == (s_i >> log_c)
    m_strict = same & (s_i < t_i)
    m_incl = same & (s_i <= t_i)
    eye_cat = (s_i == t_i).astype(F32)
    ig = _iota((g_rows, g_rows), 0)
    jg = _iota((g_rows, g_rows), 1)
    same_g = (ig >> log_c) == (jg >> log_c)
    l_tri = (same_g & (jg <= ig)).astype(BF16)
    l_all = same_g.astype(BF16)
    eye_s = _iota((MIX_W, MIX_W), 0) == _iota((MIX_W, MIX_W), 1)
    masks = (ones_bd, bd, m_strict, m_incl, eye_cat, l_tri, l_all, eye_s)

    blk_rows = chain * g_rows
    row = _iota((blk_rows, RW_COLS), 0)
    blocks = [p_ref[pi] for pi in range(n_prob)]
    rolled = [pltpu.roll(p, 1, 0) for p in blocks]
    if nb == 1:
        prevs = [jnp.where(row == 0, plast_scr[pi][7:8, :], rolled[pi]) for pi in range(n_prob)]
        for pi in range(n_prob):
            plast_scr[pi] = blocks[pi][blk_rows - 8:blk_rows, :]
    else:
        prevs = [jnp.where((row & (c - 1)) == 0, pe_ref[pi], rolled[pi]) for pi in range(n_prob)]
    chunks = lambda xs: [x[kk * g_rows:(kk + 1) * g_rows] for x in xs for kk in range(chain)]
    s_lists = [[s_scr[pi * nb + s] for s in range(nb)] for pi in range(n_prob)]
    ys, s_new = _rwkv_problems(chunks(blocks), chunks(prevs), s_lists, mu_ref[...], par_ref[...], w2_ref[...],
                               a2_ref[...], g2_ref[...], masks, nb=nb, c=c, chain=chain)
    for pi in range(n_prob):
        for kk in range(chain):
            y_ref[pi, kk * g_rows:(kk + 1) * g_rows, :] = ys[pi * chain + kk]
        for s in range(nb):
            s_scr[pi * nb + s] = s_new[pi][s]
            half = s_new[pi][s][:, :2 * HEAD_W] + s_new[pi][s][:, 2 * HEAD_W:]
            so_ref[pi * nb + s] = half[:, :HEAD_W] + half[:, HEAD_W:]


def _rwkv(p3, pe, s0, mu, par, w2p, a2p, g2p, n_prob, nb, c, chain):
    n_grp, lt, _ = p3.shape
    pe_rows = pe.shape[1]
    n_state = n_prob * nb
    blk_rows = chain * RW_ROWS
    return pl.pallas_call(
        functools.partial(_rwkv_body, n_prob=n_prob, nb=nb, c=c, chain=chain),
        grid=(n_grp // n_prob, lt // blk_rows),
        in_specs=[
            pl.BlockSpec((n_prob, blk_rows, RW_COLS), lambda i, j: (i, j, 0)),
            pl.BlockSpec((n_prob, pe_rows, RW_COLS), lambda i, j: (i, 0, 0)),
            pl.BlockSpec((n_state, MIX_W, HEAD_W), lambda i, j: (i, 0, 0)),
            _const_spec(mu.shape), _const_spec(par.shape),
            _const_spec(w2p.shape), _const_spec(a2p.shape), _const_spec(g2p.shape),
        ],
        out_specs=[
            pl.BlockSpec((n_prob, blk_rows, MIX_W), lambda i, j: (i, j, 0)),
            pl.BlockSpec((n_state, MIX_W, HEAD_W), lambda i, j: (i, 0, 0)),
        ],
        out_shape=[jax.ShapeDtypeStruct((n_grp, lt, MIX_W), F32),
                   jax.ShapeDtypeStruct((n_grp * nb, MIX_W, HEAD_W), F32)],
        scratch_shapes=[pltpu.VMEM((n_state, MIX_W, MIX_W), F32), pltpu.VMEM((n_prob, 8, RW_COLS), F32)],
        compiler_params=_cparams("parallel", "arbitrary"),
        name="rwkv7",
    )(p3, pe, s0, mu, par, w2p, a2p, g2p)


def _merge_body(x_ref, ya_ref, yc_ref, hc_ref, cp_ref, e0_ref, e1_ref, hs_ref, cw_ref, sln_ref, wm_ref, sb_ref,
                wg_ref, bg_ref, wb_ref, wo_ref, ln_ref, o_ref, z_ref, sv_ref, *, tm, seg, tiles_per_seq):
    i = pl.program_id(0)
    x = x_ref[...]
    hc = hc_ref[...]
    gb = hc[:, 0:MIX_W]
    z = hc[:, MIX_W:2 * MIX_W] * hc[:, 2 * MIX_W:3 * MIX_W]
    row = _iota((tm, MIX_W), 0)
    z1 = pltpu.roll(z, 1, 0)
    z2 = pltpu.roll(z, 2, 0)
    if seg >= tm:
        cp = cp_ref[...]
        zp = cp[:, MIX_W:2 * MIX_W] * cp[:, 2 * MIX_W:3 * MIX_W]
        zp = jnp.where(i % tiles_per_seq == 0, jnp.zeros_like(zp), zp)
        e1 = zp[7:8, :]
        e0 = zp[6:7, :]
        pos = row
    else:
        e1 = e1_ref[...]
        e0 = e0_ref[...]
        pos = row & (seg - 1)
    z1 = jnp.where(pos == 0, e1, z1)
    z2 = jnp.where(pos == 0, e0, jnp.where(pos == 1, e1, z2))
    cw = cw_ref[...]
    yb = gb * (z2 * cw[0:1, :] + z1 * cw[1:2, :] + z * cw[2:3, :])
    z_ref[...] = z[tm - z_ref.shape[0]:, :]

    hs = _gelu_tanh(hs_ref[...])
    u = hs[:, 0:MIX_W]
    sln = sln_ref[...]
    sv = _ln_rows(hs[:, MIX_W:], sln[0:1, :], sln[1:2, :])
    sv_ref[...] = sv
    t_i = _iota((SGU_CHUNK, N_HEADS * SGU_CHUNK), 0)
    s_i = _iota((SGU_CHUNK, N_HEADS * SGU_CHUNK), 1) & (SGU_CHUNK - 1)
    log_seg = int(math.log2(min(seg, SGU_CHUNK)))
    keep = (s_i <= t_i) & ((s_i >> log_seg) == (t_i >> log_seg))
    wm = jnp.where(keep, wm_ref[...], 0.0).astype(BF16)
    gmask = (_iota((N_HEADS * SGU_CHUNK, MIX_W), 0) >> 7) == (_iota((N_HEADS * SGU_CHUNK, MIX_W), 1) >> 6)
    sbias = sb_ref[...]
    svb = sv.astype(BF16)
    yd_parts = []
    for ck in range(tm // SGU_CHUNK):
        v_c = svb[ck * SGU_CHUNK:(ck + 1) * SGU_CHUNK]
        mixed = jnp.dot(wm, _stack_heads(v_c, gmask), preferred_element_type=F32) + sbias
        yd_parts.append(u[ck * SGU_CHUNK:(ck + 1) * SGU_CHUNK] * mixed)
    yd = jnp.concatenate(yd_parts, axis=0)

    halves = [slice(0, tm // 2), slice(tm // 2, tm)]
    xs = [x[r] for r in halves]
    xbs = [v.astype(BF16) for v in xs]
    branches = [[b[r].astype(BF16) for r in halves] for b in (ya_ref[...], yb, yc_ref[...], yd)]
    mix = [jnp.zeros((tm // 2, D_MODEL), F32) for _ in halves]
    for br in range(4):
        cols = slice(br * D_MODEL, (br + 1) * D_MODEL)
        gate = [_sigmoid(jnp.dot(v, wg_ref[:, cols], preferred_element_type=F32) + bg_ref[:, cols]) for v in xbs]
        proj = [jnp.dot(v, wb_ref[br], preferred_element_type=F32) for v in branches[br]]
        mix = _each(lambda m, g, p: m + g * p, mix, gate, proj)
    ln = ln_ref[...]
    mixed = [jnp.dot(m.astype(BF16), wo_ref[...], preferred_element_type=F32) for m in mix]
    for r, v, m in zip(halves, xs, mixed):
        o_ref[r, :] = _ln_rows(ALPHA * v + m, ln[0:1, :], ln[1:2, :])


def _merge(x, ya, yc, h_conv, e0, e1, h_sgu, conv_w, sgu_ln, wm, sgu_bias, wg, bgate, wb, wo, ln1, tm, seg):
    t = x.shape[0]
    prompt = seg >= tm
    tiles_per_seq = max(seg // tm, 1)
    z_rows = 8 if prompt else tm
    n_seq = t // seg if prompt else 0
    if prompt:
        cp_spec = pl.BlockSpec((8, 3 * MIX_W), lambda i: (jnp.maximum(i * (tm // 8) - 1, 0), 0))
        e_spec = _const_spec(e0.shape)
        z_spec = pl.BlockSpec((8, MIX_W), lambda i: (i // tiles_per_seq, 0))
        z_shape = jax.ShapeDtypeStruct((n_seq * 8, MIX_W), F32)
    else:
        cp_spec = pl.BlockSpec((8, 3 * MIX_W), lambda i: (0, 0))
        e_spec = pl.BlockSpec((tm, MIX_W), lambda i: (i, 0))
        z_spec = pl.BlockSpec((tm, MIX_W), lambda i: (i, 0))
        z_shape = jax.ShapeDtypeStruct((t, MIX_W), F32)
    row = lambda w: pl.BlockSpec((tm, w), lambda i: (i, 0))
    return pl.pallas_call(
        functools.partial(_merge_body, tm=tm, seg=seg, tiles_per_seq=tiles_per_seq),
        grid=(t // tm,),
        in_specs=[row(D_MODEL), row(MIX_W), row(MIX_W), row(3 * MIX_W), cp_spec, e_spec, e_spec, row(2 * MIX_W),
                  _const_spec(conv_w.shape), _const_spec(sgu_ln.shape), _const_spec(wm.shape),
                  _const_spec(sgu_bias.shape), _const_spec(wg.shape), _const_spec(bgate.shape),
                  _const_spec(wb.shape), _const_spec(wo.shape), _const_spec(ln1.shape)],
        out_specs=[row(D_MODEL), z_spec, row(MIX_W)],
        out_shape=[jax.ShapeDtypeStruct((t, D_MODEL), F32), z_shape, jax.ShapeDtypeStruct((t, MIX_W), F32)],
        compiler_params=_cparams("arbitrary"),
        name="merge",
    )(x, ya, yc, h_conv, h_conv, e0, e1, h_sgu, conv_w, sgu_ln, wm, sgu_bias, wg, bgate, wb, wo, ln1)


def _matmul2_body(x_ref, w1_ref, w2_ref, o1_ref, o2_ref):
    xb = x_ref[...].astype(BF16)
    o1_ref[...] = jnp.dot(xb, w1_ref[...], preferred_element_type=F32).astype(o1_ref.dtype)
    o2_ref[...] = jnp.dot(xb, w2_ref[...], preferred_element_type=F32).astype(o2_ref.dtype)


def _matmul2(x, w1, w2, out_dtype, tm):
    t, kdim = x.shape
    n1, n2 = w1.shape[1], w2.shape[1]
    return pl.pallas_call(
        _matmul2_body,
        grid=(t // tm,),
        in_specs=[pl.BlockSpec((tm, kdim), lambda i: (i, 0)), _const_spec(w1.shape), _const_spec(w2.shape)],
        out_specs=[pl.BlockSpec((tm, n1), lambda i: (i, 0)), pl.BlockSpec((tm, n2), lambda i: (i, 0))],
        out_shape=[jax.ShapeDtypeStruct((t, n1), out_dtype), jax.ShapeDtypeStruct((t, n2), out_dtype)],
        compiler_params=_cparams("parallel"),
        name="matmul2",
    )(x, w1, w2)


def _matmul_body(x_ref, w_ref, o_ref):
    o_ref[...] = jnp.dot(x_ref[...].astype(BF16), w_ref[...], preferred_element_type=F32).astype(o_ref.dtype)


def _matmul(x, w, out_dtype, tm):
    t, kdim = x.shape
    n = w.shape[1]
    return pl.pallas_call(
        _matmul_body,
        grid=(t // tm,),
        in_specs=[pl.BlockSpec((tm, kdim), lambda i: (i, 0)), _const_spec(w.shape)],
        out_specs=pl.BlockSpec((tm, n), lambda i: (i, 0)),
        out_shape=jax.ShapeDtypeStruct((t, n), out_dtype),
        compiler_params=_cparams("parallel"),
        name="matmul",
    )(x, w)


def _softmax_rows(sc):
    m = jnp.max(sc, axis=-1, keepdims=True)
    e = jnp.exp(sc - m)
    return e / jnp.sum(e, axis=-1, keepdims=True)


def _xattn_rows_body(q_ref, k_ref, v_ref, o_ref, *, tq, slots):
    rows = X_HEADS * tq
    n_col = N_MEM * X_HEADS
    own = (_iota((rows, n_col), 1) & (X_HEADS - 1)) == (_iota((rows, n_col), 0) >> int(math.log2(tq)))
    for s in range(slots):
        q = q_ref[s * tq:(s + 1) * tq, :]
        qs = jnp.concatenate([q[:, h * X_HD:(h + 1) * X_HD] for h in range(X_HEADS)], axis=0).astype(BF16)
        k_all = k_ref[s].reshape(n_col, X_HD).astype(BF16)
        v_all = v_ref[s].reshape(n_col, X_HD).astype(BF16)
        sc = lax.dot_general(qs, k_all, (((1,), (1,)), ((), ())), preferred_element_type=F32) * (X_HD ** -0.5)
        pr = _softmax_rows(jnp.where(own, sc, -1e30))
        o = jnp.dot(pr.astype(BF16), v_all, preferred_element_type=F32)
        for h in range(X_HEADS):
            o_ref[s * tq:(s + 1) * tq, h * X_HD:(h + 1) * X_HD] = o[h * tq:(h + 1) * tq].astype(o_ref.dtype)


def _xattn_rows(q, mem_k, mem_v, layer, tq, slots):
    t = q.shape[0]
    rows = tq * slots
    mem_spec = pl.BlockSpec((None, slots, N_MEM, X_HEADS, X_HD), lambda i: (layer, i, 0, 0, 0))
    return pl.pallas_call(
        functools.partial(_xattn_rows_body, tq=tq, slots=slots),
        grid=(t // rows,),
        in_specs=[pl.BlockSpec((rows, D_MODEL), lambda i: (i, 0)), mem_spec, mem_spec],
        out_specs=pl.BlockSpec((rows, D_MODEL), lambda i: (i, 0)),
        out_shape=jax.ShapeDtypeStruct((t, D_MODEL), q.dtype),
        compiler_params=_cparams("parallel"),
        name="xattn_rows",
    )(q, mem_k, mem_v)


def _proj_ln_body(y_ref, w_ref, x_ref, ln_ref, o_ref):
    ln = ln_ref[...]
    acc = jnp.dot(y_ref[...].astype(BF16), w_ref[...], preferred_element_type=F32)
    o_ref[...] = _ln_rows(ALPHA * x_ref[...] + acc, ln[0:1, :], ln[1:2, :])


def _proj_ln(y, w, x, ln, tm):
    t = x.shape[0]
    return pl.pallas_call(
        _proj_ln_body,
        grid=(t // tm,),
        in_specs=[pl.BlockSpec((tm, D_MODEL), lambda i: (i, 0)), _const_spec(w.shape),
                  pl.BlockSpec((tm, D_MODEL), lambda i: (i, 0)), _const_spec(ln.shape)],
        out_specs=pl.BlockSpec((tm, D_MODEL), lambda i: (i, 0)),
        out_shape=jax.ShapeDtypeStruct((t, D_MODEL), F32),
        compiler_params=_cparams("parallel"),
        name="proj_ln",
    )(y, w, x, ln)


def _mlp_rows(x, wu_ref, wd_ref, ln):
    xb = x.astype(BF16)
    acc = jnp.zeros(x.shape, F32)
    for j in range(D_FF // D_MODEL):
        cols = slice(j * D_MODEL, (j + 1) * D_MODEL)
        hid = jnp.maximum(jnp.dot(xb, wu_ref[:, cols], preferred_element_type=F32), 0.0)
        acc = acc + jnp.dot((hid * hid).astype(BF16), wd_ref[cols, :], preferred_element_type=F32)
    return _ln_rows(ALPHA * x + acc, ln[0:1, :], ln[1:2, :])


def _mlp_body(x_ref, wu_ref, wd_ref, ln_ref, o_ref):
    o_ref[...] = _mlp_rows(x_ref[...], wu_ref, wd_ref, ln_ref[...])


def _mlp(x, wu, wd, ln, tm):
    t = x.shape[0]
    return pl.pallas_call(
        _mlp_body,
        grid=(t // tm,),
        in_specs=[pl.BlockSpec((tm, D_MODEL), lambda i: (i, 0)), _const_spec(wu.shape), _const_spec(wd.shape),
                  _const_spec(ln.shape)],
        out_specs=pl.BlockSpec((tm, D_MODEL), lambda i: (i, 0)),
        out_shape=jax.ShapeDtypeStruct((t, D_MODEL), F32),
        compiler_params=_cparams("parallel"),
        name="mlp",
    )(x, wu, wd, ln)


def _layer_params(l, w_in, sb_bias, w_gate, b_gate, w_branch, w_o, conv_w, rw_mu, rw_w0, rw_w2, rw_a0, rw_a2, rw_g2,
                  rw_kk, rw_ka, rw_rk, rw_gn_g, rw_gn_b, sgu_ln_g, sgu_ln_b, sgu_ws, sgu_b, w_mq, w_mk, w_mv, w_mo,
                  w_up, w_down, ln1_g, ln1_b, ln2_g, ln2_b, ln3_g, ln3_b):
    wi = w_in[l].astype(BF16)
    off_b, off_c, off_d = 3 * MIX_W, 6 * MIX_W, 6 * MIX_W + RW_COLS
    zpad = lambda w, r0: jnp.zeros((128, MIX_W), F32).at[r0:r0 + w.shape[0]].set(w).astype(BF16)
    par = jnp.zeros((8, MIX_W), F32)
    for i, vec in enumerate((rw_w0[l], rw_a0[l], rw_kk[l], rw_ka[l], rw_rk[l].reshape(MIX_W), rw_gn_g[l], rw_gn_b[l])):
        par = par.at[i].set(vec)
    return dict(
        w_in=(wi[:, :MIX_W], wi[:, MIX_W:2 * MIX_W], wi[:, 2 * MIX_W:off_b], wi[:, off_b:off_c], wi[:, off_c:off_d],
              wi[:, off_d:]),
        sb_bias=sb_bias[l],
        w_gate=w_gate[l].astype(BF16), b_gate=b_gate[l].reshape(1, -1), w_branch=w_branch[l].astype(BF16),
        w_o=w_o[l].astype(BF16), conv_w=jnp.zeros((8, MIX_W), F32).at[:3].set(conv_w[l]),
        rw_mu=rw_mu[l].reshape(1, RW_COLS), rw_par=par,
        rw_w2=zpad(rw_w2[l], 0), rw_a2=zpad(rw_a2[l], 32), rw_g2=zpad(rw_g2[l], 64),
        sgu_ln=jnp.stack([sgu_ln_g[l], sgu_ln_b[l]]), sgu_ws=sgu_ws[l], sgu_b=sgu_b[l],
        w_mq=w_mq[l].astype(BF16), w_mk=w_mk[l].astype(BF16), w_mv=w_mv[l].astype(BF16), w_mo=w_mo[l].astype(BF16),
        w_up=w_up[l].astype(BF16), w_down=w_down[l].astype(BF16),
        ln1=jnp.stack([ln1_g[l], ln1_b[l]]), ln2=jnp.stack([ln2_g[l], ln2_b[l]]), ln3=jnp.stack([ln3_g[l], ln3_b[l]]),
    )


def _sgu_tables(lp, seg):
    ws = lp['sgu_ws']
    sb = lp['sgu_b']
    if seg < SGU_CHUNK:
        reps = SGU_CHUNK // seg
        ws = jnp.tile(ws[:, :seg, :seg], (1, reps, reps))
        sb = jnp.tile(sb[:, :seg], (1, reps))
    wm = jnp.transpose(ws, (1, 0, 2)).reshape(SGU_CHUNK, N_HEADS * SGU_CHUNK)
    bias = jnp.repeat(sb.T, HEAD_W, axis=1)
    return wm, bias


def _sb_bias_rows(bias, tq, width):
    return jnp.broadcast_to(jnp.repeat(bias, tq)[:, None], (N_HEADS * tq, width)).astype(F32)


def _tail_fused_body(x_ref, wq_ref, k_ref, v_ref, wo_ref, ln2_ref, wu_ref, wd_ref, ln3_ref, o_ref):
    tm = x_ref.shape[0]
    halves = [slice(0, tm // 2), slice(tm // 2, tm)]
    dot = lambda a, b: jnp.dot(a, b, preferred_element_type=F32)
    xs = [x_ref[r, :] for r in halves]
    qs = [dot(x.astype(BF16), wq_ref[...]).astype(BF16) for x in xs]
    kb = k_ref[...].astype(BF16)
    vb = v_ref[...].astype(BF16)
    ctx = [[] for _ in halves]
    for h in range(X_HEADS):
        cols = slice(h * X_HD, (h + 1) * X_HD)
        sc = [lax.dot_general(q[:, cols], kb[:, cols], (((1,), (1,)), ((), ())),
                              preferred_element_type=F32) * (X_HD ** -0.5) for q in qs]
        pr = [_softmax_rows(s).astype(BF16) for s in sc]
        for c, p in zip(ctx, pr):
            c.append(dot(p, vb[:, cols]).astype(BF16))
    ln2, ln3 = ln2_ref[...], ln3_ref[...]
    proj = [dot(jnp.concatenate(c, axis=1), wo_ref[...]) for c in ctx]
    x2 = _each(lambda x, p: _ln_rows(ALPHA * x + p, ln2[0:1, :], ln2[1:2, :]), xs, proj)
    x2b = [x.astype(BF16) for x in x2]
    acc = [jnp.zeros((tm // 2, D_MODEL), F32) for _ in halves]
    for j in range(D_FF // D_MODEL):
        cols = slice(j * D_MODEL, (j + 1) * D_MODEL)
        hid = [jnp.maximum(dot(x, wu_ref[:, cols]), 0.0) for x in x2b]
        acc = _each(lambda a, hd: a + dot((hd * hd).astype(BF16), wd_ref[cols, :]), acc, hid)
    for r, x, a in zip(halves, x2, acc):
        o_ref[r, :] = _ln_rows(ALPHA * x + a, ln3[0:1, :], ln3[1:2, :])


def _tail_fused(x, lp, mem_k, mem_v, tm):
    t = x.shape[0]
    tiles_per_mem = t // mem_k.shape[0] // tm
    mem_spec = pl.BlockSpec((None, N_MEM, D_MODEL), lambda i: (i // tiles_per_mem, 0, 0))
    weights = (lp['w_mq'], lp['w_mo'], lp['ln2'], lp['w_up'], lp['w_down'], lp['ln3'])
    wq, wo, ln2, wu, wd, ln3 = (_const_spec(w.shape) for w in weights)
    return pl.pallas_call(
        _tail_fused_body,
        grid=(t // tm,),
        in_specs=[pl.BlockSpec((tm, D_MODEL), lambda i: (i, 0)), wq, mem_spec, mem_spec, wo, ln2, wu, wd, ln3],
        out_specs=pl.BlockSpec((tm, D_MODEL), lambda i: (i, 0)),
        out_shape=jax.ShapeDtypeStruct((t, D_MODEL), F32),
        compiler_params=_cparams("parallel"),
        name="tail",
    )(x, lp['w_mq'], mem_k, mem_v, lp['w_mo'], lp['ln2'], lp['w_up'], lp['w_down'], lp['ln3'])


def _tail(x, lp, attend, q_dtype, tm):
    qm = _matmul(x, lp['w_mq'], q_dtype, tm)
    x = _proj_ln(attend(qm), lp['w_mo'], x, lp['ln2'], tm)
    return _mlp(x, lp['w_up'], lp['w_down'], lp['ln3'], tm)


def _layer_prompt(x, lp, mem_k, mem_v, n, seq):
    t = n * seq
    q, k, v, h_conv, h_rw, h_sgu = _inproj(x, lp['w_in'], 512)
    tq = 256
    ya = _sb_prompt(q, k, v, _sb_bias_rows(lp['sb_bias'], tq, tq), n, seq, tq)
    pe = jnp.zeros((n, 8, RW_COLS), F32)
    s0 = jnp.zeros((n, MIX_W, HEAD_W), F32)
    yc, s_fin = _rwkv(h_rw.reshape(n, seq, RW_COLS), pe, s0, lp['rw_mu'], lp['rw_par'], lp['rw_w2'], lp['rw_a2'],
                      lp['rw_g2'], n_prob=n, nb=1, c=RW_ROWS, chain=2)
    wm, sgu_bias = _sgu_tables(lp, SGU_CHUNK)
    zero_e = jnp.zeros((8, MIX_W), F32)
    x1, z_tail, _ = _merge(x, ya, yc.reshape(t, MIX_W), h_conv, zero_e, zero_e, h_sgu, lp['conv_w'], lp['sgu_ln'],
                           wm, sgu_bias, lp['w_gate'], lp['b_gate'], lp['w_branch'], lp['w_o'], lp['ln1'],
                           tm=512, seg=seq)
    x3 = _tail_fused(x1, lp, mem_k, mem_v, 512)
    k_new = k.reshape(n, seq, N_HEADS, HEAD_W)
    v_new = v.reshape(n, seq, N_HEADS, HEAD_W)
    conv_new = z_tail.reshape(n, 8, MIX_W)[:, 6:8]
    shift_new = h_rw.reshape(n, seq, RW_COLS)[:, -1]
    return x3, k_new, v_new, conv_new, shift_new, s_fin.reshape(n, N_HEADS, HEAD_W, HEAD_W)


def _layer_sample(x, lp, layer, mem_k, mem_v, cache_kt, cache_vt, page_table, state_conv, state_shift, state_wkv,
                  nb, seq):
    t = nb * seq
    q, k_new, v_new, h_conv, h_rw, h_sgu = _inproj(x, lp['w_in'], 512)
    ya = _sb_sample(q, k_new, v_new, cache_kt, cache_vt, page_table, _sb_bias_rows(lp['sb_bias'], seq, PAGE_SIZE),
                    layer, nb, seq)
    per = RW_ROWS // seq
    pe = jnp.repeat(state_shift, seq, axis=0).reshape(nb // per, RW_ROWS, RW_COLS)
    yc, s_fin = _rwkv(h_rw.reshape(nb // per, RW_ROWS, RW_COLS), pe, state_wkv.reshape(nb, MIX_W, HEAD_W), lp['rw_mu'],
                      lp['rw_par'], lp['rw_w2'], lp['rw_a2'], lp['rw_g2'], n_prob=2, nb=per, c=seq, chain=1)
    wm, sgu_bias = _sgu_tables(lp, seq)
    e0 = jnp.repeat(state_conv[:, 0], seq, axis=0)
    e1 = jnp.repeat(state_conv[:, 1], seq, axis=0)
    x1, z_all, sgu_v = _merge(x, ya, yc.reshape(t, MIX_W), h_conv, e0, e1, h_sgu, lp['conv_w'], lp['sgu_ln'], wm,
                              sgu_bias, lp['w_gate'], lp['b_gate'], lp['w_branch'], lp['w_o'], lp['ln1'],
                              tm=512, seg=seq)
    x3 = _tail(x1, lp, lambda qm: _xattn_rows(qm, mem_k, mem_v, layer, seq, 4), F32, 512)
    conv_new = z_all.reshape(nb, seq, MIX_W)[:, seq - 2:]
    shift_new = h_rw.reshape(nb, seq, RW_COLS)[:, -1]
    return (x3, k_new.reshape(nb, seq, N_HEADS, HEAD_W), v_new.reshape(nb, seq, N_HEADS, HEAD_W), conv_new,
            shift_new, s_fin.reshape(nb, N_HEADS, HEAD_W, HEAD_W), sgu_v.reshape(nb, seq, MIX_W))


def kernel(x_prompt, x_sample, mem_prompt, cache_k, cache_v, page_table, cache_mem_k, cache_mem_v, state_conv,
           state_wkv, state_shift, w_in, sb_bias, w_gate, b_gate, w_branch, w_o, conv_w, rw_mu, rw_w0, rw_w2, rw_a0,
           rw_a2, rw_g2, rw_kk, rw_ka, rw_rk, rw_gn_g, rw_gn_b, sgu_ln_g, sgu_ln_b, sgu_ws, sgu_b, w_mq, w_mk, w_mv,
           w_mo, w_up, w_down, ln1_g, ln1_b, ln2_g, ln2_b, ln3_g, ln3_b):
    n_p, seq_p, _ = x_prompt.shape
    n_s, seq_s, _ = x_sample.shape
    n_phys = cache_k.shape[1]
    xp = x_prompt.reshape(n_p * seq_p, D_MODEL)
    xs = x_sample.reshape(n_s * seq_s, D_MODEL)
    mem2d = mem_prompt.reshape(n_p * N_MEM, D_MODEL)
    cache_kt = jnp.transpose(cache_k, (0, 1, 3, 4, 2)).reshape(DEPTH, n_phys, MIX_W, PAGE_SIZE)
    cache_vt = jnp.transpose(cache_v, (0, 1, 3, 4, 2)).reshape(DEPTH, n_phys, MIX_W, PAGE_SIZE)
    outs = [[] for _ in range(13)]
    for l in range(DEPTH):
        lp = _layer_params(l, w_in, sb_bias, w_gate, b_gate, w_branch, w_o, conv_w, rw_mu, rw_w0, rw_w2, rw_a0,
                           rw_a2, rw_g2, rw_kk, rw_ka, rw_rk, rw_gn_g, rw_gn_b, sgu_ln_g, sgu_ln_b, sgu_ws, sgu_b,
                           w_mq, w_mk, w_mv, w_mo, w_up, w_down, ln1_g, ln1_b, ln2_g, ln2_b, ln3_g, ln3_b)
        mk, mv = _matmul2(mem2d, lp['w_mk'], lp['w_mv'], F32, 512)
        mk4 = mk.reshape(n_p, N_MEM, X_HEADS, X_HD)
        mv4 = mv.reshape(n_p, N_MEM, X_HEADS, X_HD)
        xp, pk, pv, pc, psh, pst = _layer_prompt(xp, lp, mk.reshape(n_p, N_MEM, D_MODEL),
                                                 mv.reshape(n_p, N_MEM, D_MODEL), n_p, seq_p)
        xs, sk, sv, sc, ssh, sst, scv = _layer_sample(
            xs, lp, l, cache_mem_k, cache_mem_v, cache_kt, cache_vt, page_table,
            state_conv[l], state_shift[l], state_wkv[l], n_s, seq_s)
        for lst, val in zip(outs, (pk, pv, mk4, mv4, pc, pst, psh, sk, sv, sc, sst, ssh, scv)):
            lst.append(val)
    return (xp.reshape(n_p, seq_p, D_MODEL), xs.reshape(n_s, seq_s, D_MODEL)) + tuple(jnp.stack(o) for o in outs)
```

```python
import functools
import math

import jax
import jax.numpy as jnp
from jax import lax
from jax.experimental import pallas as pl
from jax.experimental.pallas import tpu as pltpu

F32 = jnp.float32
BF16 = jnp.bfloat16

D_MODEL = 1024
DEPTH = 2
MIX_W = 256
HEAD_W = 64
N_HEADS = MIX_W // HEAD_W
RW_COLS = 896
PAGE_SIZE = 128
N_MEM = 256
X_HEADS = 4
X_HD = D_MODEL // X_HEADS
D_FF = 4 * D_MODEL
ALPHA = (2 * DEPTH) ** 0.25
LN_EPS = 1e-5
RW_GN_EPS = 64e-5
SGU_CHUNK = 128
RW_ROWS = 64
ROW_TILE = 512
SB_BLOCK = 256
XATTN_SLOTS = 4
VMEM_LIMIT = 56 * 1024 * 1024


def _cparams(*sem):
    return pltpu.CompilerParams(dimension_semantics=sem, vmem_limit_bytes=VMEM_LIMIT)


def _const_spec(shape):
    nd = len(shape)
    return pl.BlockSpec(shape, lambda *_: (0,) * nd, pipeline_mode=pl.Buffered(1))


def _bdot(a, b):
    return jnp.dot(a.astype(BF16), b.astype(BF16), preferred_element_type=F32)


def _bdot_nt(a, b):
    return lax.dot_general(a.astype(BF16), b.astype(BF16), (((1,), (1,)), ((), ())),
                           preferred_element_type=F32)


def _bdot_tn(a, b):
    return lax.dot_general(a.astype(BF16), b.astype(BF16), (((0,), (0,)), ((), ())),
                           preferred_element_type=F32)


def _split2(x):
    hi = x.astype(BF16)
    lo = (x - hi.astype(F32)).astype(BF16)
    return hi, lo


def _split3(x):
    hi = x.astype(BF16)
    r1 = x - hi.astype(F32)
    mid = r1.astype(BF16)
    lo = (r1 - mid.astype(F32)).astype(BF16)
    return hi, mid, lo


def _sel_dot_l(sel_b, x):
    return sum(jnp.dot(sel_b, part, preferred_element_type=F32) for part in _split3(x))


def _sel_dot_r(x, sel_b):
    return sum(jnp.dot(part, sel_b, preferred_element_type=F32) for part in _split2(x))


def _dot3(a, b):
    ah, al = _split2(a)
    bh, bl = _split2(b)
    d = lambda x, y: jnp.dot(x, y, preferred_element_type=F32)
    return d(ah, bh) + d(ah, bl) + d(al, bh)


def _sigmoid(x):
    return 1.0 / (1.0 + jnp.exp(-x))


def _softplus(x):
    return jnp.maximum(x, 0.0) + jnp.log(1.0 + jnp.exp2(jnp.abs(x) * (-1.0 / math.log(2.0))))


def _gelu_tanh(x):
    return 0.5 * x * (1.0 + jnp.tanh(0.7978845608028654 * (x + 0.044715 * (x * x * x))))


def _ln_rows(x, g, b, eps=LN_EPS):
    mu = jnp.mean(x, axis=-1, keepdims=True)
    xc = x - mu
    var = jnp.mean(xc * xc, axis=-1, keepdims=True)
    return xc * lax.rsqrt(var + eps) * g + b


def _iota(shape, dim):
    return lax.broadcasted_iota(jnp.int32, shape, dim)


def _head_block_mask(rows, row_shift):
    return (_iota((rows, MIX_W), 0) >> row_shift) == (_iota((rows, MIX_W), 1) >> 6)


def _stack_heads(x, mask):
    t = jnp.concatenate([x] * N_HEADS, axis=0)
    return jnp.where(mask, t, jnp.zeros_like(t))


def _unstack_heads(x, rows):
    return x[0:rows] + x[rows:2 * rows] + x[2 * rows:3 * rows] + x[3 * rows:4 * rows]


def _inproj_body(x_ref, *refs):
    n = len(refs) // 2
    xb = x_ref[...].astype(BF16)
    for w_ref, o_ref in zip(refs[:n], refs[n:]):
        o_ref[...] = jnp.dot(xb, w_ref[...], preferred_element_type=F32)


def _inproj(x, weights, tm):
    t = x.shape[0]
    widths = [w.shape[1] for w in weights]
    return pl.pallas_call(
        _inproj_body,
        grid=(t // tm,),
        in_specs=[pl.BlockSpec((tm, D_MODEL), lambda i: (i, 0))] + [_const_spec(w.shape) for w in weights],
        out_specs=[pl.BlockSpec((tm, w), lambda i: (i, 0)) for w in widths],
        out_shape=[jax.ShapeDtypeStruct((t, w), F32) for w in widths],
        compiler_params=_cparams("parallel"),
        name="inproj",
    )(x, *weights)


def _suffix_ones(tk):
    return (_iota((tk, tk), 0) >= _iota((tk, tk), 1)).astype(BF16)


def _sb_weights(z, carry, suffix, mask):
    tk = z.shape[1]
    stop = _softplus(z)
    if mask is not None:
        stop = jnp.where(mask, stop, 0.0)
    cs = jnp.dot(stop.astype(BF16), suffix, preferred_element_type=F32)
    later = carry if tk == 128 else jnp.concatenate([carry] * (tk // 128), axis=1)
    w = jnp.exp(z - cs - later)
    if mask is not None:
        w = jnp.where(mask, w, 0.0)
    total = jnp.sum(stop, axis=1, keepdims=True)
    return w, carry + jnp.broadcast_to(total, carry.shape)


def _sb_prompt_body(q_ref, k_ref, v_ref, bias_ref, o_ref, kb, vb, carry_ref, acc_ref, *, tq):
    qi = pl.program_id(1)
    rows = N_HEADS * tq

    @pl.when(qi == 0)
    def _():
        kb[...] = k_ref[...].astype(BF16)
        vb[...] = v_ref[...].astype(BF16)

    hmask = _head_block_mask(rows, int(math.log2(tq)))
    qs = _stack_heads((q_ref[...] * (HEAD_W ** -0.5)).astype(BF16), hmask)
    suffix = _suffix_ones(tq)

    def block(start, carry, acc, mask):
        z = lax.dot_general(qs, kb[pl.ds(start, tq), :], (((1,), (1,)), ((), ())),
                            preferred_element_type=F32) + bias_ref[...]
        w, carry = _sb_weights(z, carry, suffix, mask)
        acc = acc + jnp.dot(w.astype(BF16), vb[pl.ds(start, tq), :], preferred_element_type=F32)
        return carry, acc

    def block_pair(start):
        k2 = kb[pl.ds(start, 2 * tq), :]
        v2 = vb[pl.ds(start, 2 * tq), :]
        hrows = lambda h: slice(h * tq, (h + 1) * tq)
        logits = lambda h: lax.dot_general(qs[hrows(h)], k2, (((1,), (1,)), ((), ())),
                                           preferred_element_type=F32)
        z2 = [logits(0), logits(1)]
        for h in range(N_HEADS):
            bias = bias_ref[hrows(h), :]
            z = jnp.concatenate([z2[h][:, tq:] + bias, z2[h][:, :tq] + bias], axis=0)
            stop = _softplus(z)
            cs = jnp.dot(stop.astype(BF16), suffix, preferred_element_type=F32)
            if h + 2 < N_HEADS:
                z2.append(logits(h + 2))
            total = jnp.broadcast_to(jnp.sum(stop, axis=1, keepdims=True), (2 * tq, 128))
            carry = carry_ref[hrows(h), :]
            carry2 = jnp.concatenate([carry, carry + total[:tq]], axis=0)
            later = jnp.concatenate([carry2] * (tq // 128), axis=1)
            w = jnp.exp(z - cs - later).astype(BF16)
            w2 = jnp.concatenate([w[tq:], w[:tq]], axis=1)
            acc_ref[hrows(h), :] += jnp.dot(w2, v2, preferred_element_type=F32)
            carry_ref[hrows(h), :] = carry2[tq:] + total[tq:]

    t_idx = _iota((rows, tq), 0) & (tq - 1)
    causal = _iota((rows, tq), 1) < t_idx
    carry, acc = block(pl.multiple_of(qi * tq, tq), jnp.zeros((rows, 128), F32),
                       jnp.zeros((rows, MIX_W), F32), causal)
    carry_ref[...] = carry
    acc_ref[...] = acc

    @pl.when((qi & 1) == 1)
    def _():
        c, a = block(pl.multiple_of((qi - 1) * tq, tq), carry_ref[...], acc_ref[...], None)
        carry_ref[...] = c
        acc_ref[...] = a

    n_pairs = qi >> 1

    def step(jj, _):
        block_pair(pl.multiple_of((n_pairs - 1 - jj) * (2 * tq), 2 * tq))
        return 0

    lax.fori_loop(0, n_pairs, step, 0)
    acc = jnp.where(hmask, acc_ref[...], 0.0)
    o_ref[...] = _unstack_heads(acc, tq)


def _sb_prompt(q, k, v, bias_rows, n, seq, tq):
    nq = seq // tq
    rows = N_HEADS * tq
    return pl.pallas_call(
        functools.partial(_sb_prompt_body, tq=tq),
        grid=(n, nq),
        in_specs=[
            pl.BlockSpec((tq, MIX_W), lambda b, i: (b * nq + i, 0)),
            pl.BlockSpec((seq, MIX_W), lambda b, i: (b, 0)),
            pl.BlockSpec((seq, MIX_W), lambda b, i: (b, 0)),
            _const_spec((rows, tq)),
        ],
        out_specs=pl.BlockSpec((tq, MIX_W), lambda b, i: (b * nq + i, 0)),
        out_shape=jax.ShapeDtypeStruct((n * seq, MIX_W), F32),
        scratch_shapes=[pltpu.VMEM((seq, MIX_W), BF16), pltpu.VMEM((seq, MIX_W), BF16),
                        pltpu.VMEM((rows, 128), F32), pltpu.VMEM((rows, MIX_W), F32)],
        compiler_params=_cparams("parallel", "arbitrary"),
        name="sb_prompt",
    )(q, k, v, bias_rows)


SB_SEQ_SLOTS = 3


def _sb_sample_body(pt_ref, q_ref, kn_ref, vn_ref, ck_ref, cv_ref, bias_ref, o_ref, kbuf, vbuf, sem,
                    *, layer, tq, n_pages):
    b = pl.program_id(0)
    rows = N_HEADS * tq
    n_blk = n_pages + 1

    def seq_copies(s):
        slot = lax.rem(s, SB_SEQ_SLOTS)
        out = []
        for j in range(n_pages):
            page = pt_ref[s, n_pages - 1 - j]
            out.append(pltpu.make_async_copy(ck_ref.at[layer, page], kbuf.at[slot, j], sem.at[slot]))
            out.append(pltpu.make_async_copy(cv_ref.at[layer, page], vbuf.at[slot, j], sem.at[slot]))
        return slot, out

    @pl.when(b == 0)
    def _():
        for s in range(SB_SEQ_SLOTS - 1):
            for cp in seq_copies(jnp.int32(s))[1]:
                cp.start()

    @pl.when(b + (SB_SEQ_SLOTS - 1) < pl.num_programs(0))
    def _():
        for cp in seq_copies(b + (SB_SEQ_SLOTS - 1))[1]:
            cp.start()

    slot, copies = seq_copies(b)
    for cp in copies:
        cp.wait()

    hmask = _head_block_mask(rows, int(math.log2(tq)))
    qs = _stack_heads(q_ref[...] * (HEAD_W ** -0.5), hmask).astype(BF16)
    bias = bias_ref[...]

    pad = jnp.zeros((PAGE_SIZE - tq, MIX_W), F32)
    k_new = jnp.concatenate([kn_ref[...], pad], axis=0).astype(BF16)
    v_new = jnp.concatenate([vn_ref[...], pad], axis=0).astype(BF16)
    z_new = lax.dot_general(qs, k_new, (((1,), (1,)), ((), ())), preferred_element_type=F32)
    k_pages = jnp.concatenate([kbuf[slot, j].astype(BF16) for j in range(n_pages)], axis=1)
    z_pages = jnp.dot(qs, k_pages, preferred_element_type=F32)
    z = jnp.concatenate([z_new] + [z_pages[:, j * PAGE_SIZE:(j + 1) * PAGE_SIZE] for j in range(n_pages)], axis=0)
    z = z + jnp.concatenate([bias] * n_blk, axis=0)

    r_i = _iota((n_blk * rows, PAGE_SIZE), 0)
    valid = (r_i >= rows) | (_iota((n_blk * rows, PAGE_SIZE), 1) < (r_i & (tq - 1)))
    stop = jnp.where(valid, _softplus(z), 0.0)
    cs = jnp.dot(stop.astype(BF16), _suffix_ones(PAGE_SIZE), preferred_element_type=F32)
    total = jnp.broadcast_to(jnp.sum(stop, axis=1, keepdims=True), (n_blk * rows, PAGE_SIZE))
    carry = [jnp.zeros((rows, PAGE_SIZE), F32)]
    for i in range(n_blk - 1):
        carry.append(carry[-1] + total[i * rows:(i + 1) * rows])
    w = jnp.where(valid, jnp.exp(z - cs - jnp.concatenate(carry, axis=0)), 0.0).astype(BF16)

    acc = jnp.dot(w[:rows], v_new, preferred_element_type=F32)
    w_pages = jnp.concatenate([w[(j + 1) * rows:(j + 2) * rows] for j in range(n_pages)], axis=1)
    v_pages = jnp.concatenate([vbuf[slot, j].astype(BF16) for j in range(n_pages)], axis=1)
    acc = acc + lax.dot_general(w_pages, v_pages, (((1,), (1,)), ((), ())), preferred_element_type=F32)
    o_ref[...] = _unstack_heads(jnp.where(hmask, acc, 0.0), tq)


def _sb_sample(q, k, v, cache_kt, cache_vt, page_table, bias_rows, layer, nb, tq):
    n_pages = page_table.shape[1]
    rows = N_HEADS * tq
    buf = (SB_SEQ_SLOTS, n_pages, MIX_W, PAGE_SIZE)
    grid_spec = pltpu.PrefetchScalarGridSpec(
        num_scalar_prefetch=1,
        grid=(nb,),
        in_specs=[
            pl.BlockSpec((tq, MIX_W), lambda b, pt: (b, 0)),
            pl.BlockSpec((tq, MIX_W), lambda b, pt: (b, 0)),
            pl.BlockSpec((tq, MIX_W), lambda b, pt: (b, 0)),
            pl.BlockSpec(memory_space=pl.ANY),
            pl.BlockSpec(memory_space=pl.ANY),
            pl.BlockSpec((rows, PAGE_SIZE), lambda b, pt: (0, 0)),
        ],
        out_specs=pl.BlockSpec((tq, MIX_W), lambda b, pt: (b, 0)),
        scratch_shapes=[pltpu.VMEM(buf, F32), pltpu.VMEM(buf, F32), pltpu.SemaphoreType.DMA((SB_SEQ_SLOTS,))],
    )
    return pl.pallas_call(
        functools.partial(_sb_sample_body, layer=layer, tq=tq, n_pages=n_pages),
        grid_spec=grid_spec,
        out_shape=jax.ShapeDtypeStruct((nb * tq, MIX_W), F32),
        compiler_params=_cparams("arbitrary"),
        name="sb_sample",
    )(page_table, q, k, v, cache_kt, cache_vt, bias_rows)


def _each(fn, *lists):
    return [fn(*args) for args in zip(*lists)]


def _rwkv_problems(ps, p_prevs, s_lists, mu, par, w2p, a2p, g2p, masks, *, nb, c, chain):
    ones_bd, bd, m_strict, m_incl, eye_cat, l_tri, l_all, eye_s = masks
    w0, a0, kk_s, ka_s, rk, gn_g, gn_b = (par[i:i + 1, :] for i in range(7))
    mm = lambda x, y: jnp.dot(x.astype(BF16), y.astype(BF16), preferred_element_type=F32)
    expand = lambda x: _stack_heads(x.astype(BF16), bd)
    left = lambda x: x[:, :MIX_W]
    right = lambda x: x[:, MIX_W:]

    n_p = len(ps)

    def rows_batched(fn, xs):
        out = fn(xs[0] if n_p == 1 else jnp.concatenate(xs, axis=0))
        return [out[i * RW_ROWS:(i + 1) * RW_ROWS] for i in range(n_p)]

    head_sums = lambda xs: rows_batched(lambda x: _sel_dot_r(x, ones_bd), xs)

    xs = _each(lambda p, pp: p + (pp - p) * mu, ps, p_prevs)
    r = [x[:, 0:256] for x in xs]
    k = [x[:, 256:512] for x in xs]
    v = [x[:, 512:768] for x in xs]
    lora = [x[:, 768:896] for x in xs]
    u = rows_batched(lambda x: w0 + mm(jnp.tanh(x), w2p), lora)
    logw = [(-math.exp(-0.5)) * _sigmoid(x) for x in u]
    asig = rows_batched(lambda x: _sigmoid(a0 + mm(x, a2p)), lora)
    gate = rows_batched(lambda x: mm(_sigmoid(x), g2p), lora)
    kk = [x * kk_s for x in k]
    ssq = head_sums([x * x for x in kk])
    kk = _each(lambda x, q: x / jnp.maximum(jnp.sqrt(q), 1e-12), kk, ssq)
    k_eff = _each(lambda x, s: x * (1.0 + (s - 1.0) * ka_s), k, asig)
    rk_sum = head_sums(_each(lambda x, y: x * y * rk, r, k_eff))
    bonus = _each(lambda x, y: x * y, rk_sum, v)
    b = _each(lambda x, s: x * s, kk, asig)

    sums = _sel_dot_l(jnp.concatenate([l_tri, l_all], axis=0), jnp.concatenate(logw, axis=1))
    cum = [sums[:RW_ROWS, i * MIX_W:(i + 1) * MIX_W] for i in range(n_p)]
    cum_all = [sums[RW_ROWS:, i * MIX_W:(i + 1) * MIX_W] for i in range(n_p)]
    rt = _each(lambda x, q: (x * jnp.exp(q)).astype(BF16), r, cum)
    at = _each(lambda x, q, lw: (-x * jnp.exp(q - lw)).astype(BF16), kk, cum, logw)
    ginv = [jnp.exp(-q) for q in cum]
    bt = _each(lambda x, g: x * g, b, ginv)
    kt = _each(lambda x, g: x * g, k_eff, ginv)
    e_tail = _each(lambda qa, q: jnp.exp(qa - q), cum_all, cum)
    bg = _each(lambda x, e: x * e, b, e_tail)
    kg = _each(lambda x, e: x * e, k_eff, e_tail)
    g_end = [jnp.exp(q) for q in cum_all]

    lhs = _each(lambda x, y: jnp.concatenate([x, y], axis=0), at, rt)
    rhs = _each(lambda x, y: jnp.concatenate([expand(x), expand(y)], axis=0), bt, kt)
    a_cat = _each(lambda x, y: lax.dot_general(x, y, (((1,), (1,)), ((), ())), preferred_element_type=F32),
                  lhs, rhs)
    a_ab = [jnp.where(m_strict, left(x[:RW_ROWS]), 0.0) for x in a_cat]
    a_ak = [jnp.where(m_strict, right(x[:RW_ROWS]), 0.0) for x in a_cat]
    a_rb = [jnp.where(m_incl, left(x[RW_ROWS:]), 0.0) for x in a_cat]
    a_rk = [jnp.where(m_incl, right(x[RW_ROWS:]), 0.0) for x in a_cat]

    inv = [eye_cat + x for x in a_ab]
    apow = a_ab
    apow_bd = [expand(x) for x in apow]
    span = 1
    while 2 * span < c:
        apow = _each(mm, apow, apow_bd)
        apow_bd = [expand(x) for x in apow]
        inv = _each(lambda x, y: x + mm(x, y), inv, apow_bd)
        span *= 2

    sv = [expand(x) for x in v]
    w1 = _each(mm, a_ak, sv)
    uu = _each(lambda i, x, y: mm(i, jnp.concatenate([expand(x), expand(y)], axis=1)), inv, at, w1)
    ua = [left(x) for x in uu]
    uv = [right(x) for x in uu]
    qy = _each(lambda m, x, y: mm(m, jnp.concatenate([expand(x), expand(y)], axis=1)), a_rb, ua, uv)
    y0 = _each(lambda q, m, x: right(q) + mm(m, x), qy, a_rk, sv)
    qe = _each(lambda x, q: x.astype(F32) + left(q), rt, qy)

    row_seq = _iota((RW_ROWS, MIX_W), 0) >> int(math.log2(c))
    m_c, n_c = [], []
    for s in range(nb):
        own = lambda x: x if nb == 1 else jnp.where(row_seq == s, x, 0.0)
        bg_s = [own(x) for x in bg]
        kg_s = [own(x) for x in kg]
        m_c.append(_each(lambda x, y, g: jnp.where(bd, _bdot_tn(x, y), 0.0)
                         + jnp.where(eye_s, g[s * c:s * c + 1, :], 0.0), ua, bg_s, g_end))
        n_c.append(_each(lambda x, y, z, w: jnp.where(bd, _bdot_tn(x, y) + _bdot_tn(z, w), 0.0),
                         uv, bg_s, v, kg_s))

    states = [list(sl) for sl in s_lists]
    y_parts = [[] for _ in ps]
    for kk in range(chain):
        idx = [g * chain + kk for g in range(len(s_lists))]
        pick = lambda xs: [xs[i] for i in idx]
        for s in range(nb):
            rows = slice(s * c, (s + 1) * c)
            cur = [st[s] for st in states]
            y_s = _each(lambda q, st, y: _bdot_nt(q[rows], st) + y[rows], pick(qe), cur, pick(y0))
            nxt = _each(lambda st, m, n: _dot3(st, m) + n, cur, pick(m_c[s]), pick(n_c[s]))
            for gi, i in enumerate(idx):
                y_parts[i].append(y_s[gi])
                states[gi][s] = nxt[gi]
    s_new = states
    y = [parts[0] if nb == 1 else jnp.concatenate(parts, axis=0) for parts in y_parts]

    mean = [x * (1.0 / HEAD_W) for x in head_sums(y)]
    d = _each(lambda x, m: x - m, y, mean)
    var = [x * (1.0 / HEAD_W) for x in head_sums([x * x for x in d])]
    out = _each(lambda x, q, bo, g: (x * lax.rsqrt(q + RW_GN_EPS) * gn_g + gn_b + bo) * g, d, var, bonus, gate)
    return out, s_new


def _rwkv_body(p_ref, pe_ref, s0_ref, mu_ref, par_ref, w2_ref, a2_ref, g2_ref, y_ref, so_ref, s_scr, plast_scr,
               *, n_prob, nb, c, chain):
    ci = pl.program_id(1)
    g_rows = RW_ROWS
    r4 = N_HEADS * g_rows
    log_c = int(math.log2(c))

    bd = _head_block_mask(r4, 6)

    @pl.when(ci == 0)
    def _():
        for idx in range(n_prob * nb):
            s_scr[idx] = jnp.where(bd, jnp.concatenate([s0_ref[idx]] * N_HEADS, axis=1), 0.0)
        if nb == 1:
            plast_scr[...] = pe_ref[...]

    ones_bd = bd.astype(BF16)
    t_i = _iota((g_rows, r4), 0)
    s_i = _iota((g_rows, r4), 1) & (g_rows - 1)
    same = (t_i >> log_c) == (s_i >> log_c)
    m_strict = same & (s_i < t_i)
    m_incl = same & (s_i <= t_i)
    eye_cat = (s_i == t_i).astype(F32)
    ig = _iota((g_rows, g_rows), 0)
    jg = _iota((g_rows, g_rows), 1)
    same_g = (ig >> log_c) == (jg >> log_c)
    l_tri = (same_g & (jg <= ig)).astype(BF16)
    l_all = same_g.astype(BF16)
    eye_s = _iota((MIX_W, MIX_W), 0) == _iota((MIX_W, MIX_W), 1)
    masks = (ones_bd, bd, m_strict, m_incl, eye_cat, l_tri, l_all, eye_s)

    blk_rows = chain * g_rows
    row = _iota((blk_rows, RW_COLS), 0)
    blocks = [p_ref[pi] for pi in range(n_prob)]
    rolled = [pltpu.roll(p, 1, 0) for p in blocks]
    if nb == 1:
        prevs = [jnp.where(row == 0, plast_scr[pi][7:8, :], rolled[pi]) for pi in range(n_prob)]
        for pi in range(n_prob):
            plast_scr[pi] = blocks[pi][blk_rows - 8:blk_rows, :]
    else:
        prevs = [jnp.where((row & (c - 1)) == 0, pe_ref[pi], rolled[pi]) for pi in range(n_prob)]
    chunks = lambda xs: [x[kk * g_rows:(kk + 1) * g_rows] for x in xs for kk in range(chain)]
    s_lists = [[s_scr[pi * nb + s] for s in range(nb)] for pi in range(n_prob)]
    ys, s_new = _rwkv_problems(chunks(blocks), chunks(prevs), s_lists, mu_ref[...], par_ref[...], w2_ref[...],
                               a2_ref[...], g2_ref[...], masks, nb=nb, c=c, chain=chain)
    for pi in range(n_prob):
        for kk in range(chain):
            y_ref[pi, kk * g_rows:(kk + 1) * g_rows, :] = ys[pi * chain + kk]
        for s in range(nb):
            s_scr[pi * nb + s] = s_new[pi][s]
            half = s_new[pi][s][:, :2 * HEAD_W] + s_new[pi][s][:, 2 * HEAD_W:]
            so_ref[pi * nb + s] = half[:, :HEAD_W] + half[:, HEAD_W:]


def _rwkv(p3, pe, s0, mu, par, w2p, a2p, g2p, n_prob, nb, c, chain):
    n_grp, lt, _ = p3.shape
    pe_rows = pe.shape[1]
    n_state = n_prob * nb
    blk_rows = chain * RW_ROWS
    return pl.pallas_call(
        functools.partial(_rwkv_body, n_prob=n_prob, nb=nb, c=c, chain=chain),
        grid=(n_grp // n_prob, lt // blk_rows),
        in_specs=[
            pl.BlockSpec((n_prob, blk_rows, RW_COLS), lambda i, j: (i, j, 0)),
            pl.BlockSpec((n_prob, pe_rows, RW_COLS), lambda i, j: (i, 0, 0)),
            pl.BlockSpec((n_state, MIX_W, HEAD_W), lambda i, j: (i, 0, 0)),
            _const_spec(mu.shape), _const_spec(par.shape),
            _const_spec(w2p.shape), _const_spec(a2p.shape), _const_spec(g2p.shape),
        ],
        out_specs=[
            pl.BlockSpec((n_prob, blk_rows, MIX_W), lambda i, j: (i, j, 0)),
            pl.BlockSpec((n_state, MIX_W, HEAD_W), lambda i, j: (i, 0, 0)),
        ],
        out_shape=[jax.ShapeDtypeStruct((n_grp, lt, MIX_W), F32),
                   jax.ShapeDtypeStruct((n_grp * nb, MIX_W, HEAD_W), F32)],
        scratch_shapes=[pltpu.VMEM((n_state, MIX_W, MIX_W), F32), pltpu.VMEM((n_prob, 8, RW_COLS), F32)],
        compiler_params=_cparams("parallel", "arbitrary"),
        name="rwkv7",
    )(p3, pe, s0, mu, par, w2p, a2p, g2p)


def _merge_body(x_ref, ya_ref, yc_ref, hc_ref, cp_ref, e0_ref, e1_ref, hs_ref, cw_ref, sln_ref, wm_ref, sb_ref,
                wg_ref, bg_ref, wb_ref, wo_ref, ln_ref, o_ref, z_ref, sv_ref, *, tm, seg, tiles_per_seq):
    i = pl.program_id(0)
    x = x_ref[...]
    hc = hc_ref[...]
    gb = hc[:, 0:MIX_W]
    z = hc[:, MIX_W:2 * MIX_W] * hc[:, 2 * MIX_W:3 * MIX_W]
    row = _iota((tm, MIX_W), 0)
    z1 = pltpu.roll(z, 1, 0)
    z2 = pltpu.roll(z, 2, 0)
    if seg >= tm:
        cp = cp_ref[...]
        zp = cp[:, MIX_W:2 * MIX_W] * cp[:, 2 * MIX_W:3 * MIX_W]
        zp = jnp.where(i % tiles_per_seq == 0, jnp.zeros_like(zp), zp)
        e1 = zp[7:8, :]
        e0 = zp[6:7, :]
        pos = row
    else:
        e1 = e1_ref[...]
        e0 = e0_ref[...]
        pos = row & (seg - 1)
    z1 = jnp.where(pos == 0, e1, z1)
    z2 = jnp.where(pos == 0, e0, jnp.where(pos == 1, e1, z2))
    cw = cw_ref[...]
    yb = gb * (z2 * cw[0:1, :] + z1 * cw[1:2, :] + z * cw[2:3, :])
    z_ref[...] = z[tm - z_ref.shape[0]:, :]

    hs = _gelu_tanh(hs_ref[...])
    u = hs[:, 0:MIX_W]
    sln = sln_ref[...]
    sv = _ln_rows(hs[:, MIX_W:], sln[0:1, :], sln[1:2, :])
    sv_ref[...] = sv[tm - sv_ref.shape[0]:, :]
    t_i = _iota((SGU_CHUNK, N_HEADS * SGU_CHUNK), 0)
    s_i = _iota((SGU_CHUNK, N_HEADS * SGU_CHUNK), 1) & (SGU_CHUNK - 1)
    log_seg = int(math.log2(min(seg, SGU_CHUNK)))
    keep = (s_i <= t_i) & ((s_i >> log_seg) == (t_i >> log_seg))
    wm = jnp.where(keep, wm_ref[...], 0.0).astype(BF16)
    gmask = (_iota((N_HEADS * SGU_CHUNK, MIX_W), 0) >> 7) == (_iota((N_HEADS * SGU_CHUNK, MIX_W), 1) >> 6)
    sbias = sb_ref[...]
    svb = sv.astype(BF16)
    yd_parts = []
    for ck in range(tm // SGU_CHUNK):
        v_c = svb[ck * SGU_CHUNK:(ck + 1) * SGU_CHUNK]
        mixed = jnp.dot(wm, _stack_heads(v_c, gmask), preferred_element_type=F32) + sbias
        yd_parts.append(u[ck * SGU_CHUNK:(ck + 1) * SGU_CHUNK] * mixed)
    yd = jnp.concatenate(yd_parts, axis=0)

    halves = [slice(0, tm // 2), slice(tm // 2, tm)]
    xs = [x[r] for r in halves]
    xbs = [v.astype(BF16) for v in xs]
    branches = [[b[r].astype(BF16) for r in halves] for b in (ya_ref[...], yb, yc_ref[...], yd)]
    mix = [jnp.zeros((tm // 2, D_MODEL), F32) for _ in halves]
    for br in range(4):
        cols = slice(br * D_MODEL, (br + 1) * D_MODEL)
        gate = [_sigmoid(jnp.dot(v, wg_ref[:, cols], preferred_element_type=F32) + bg_ref[:, cols]) for v in xbs]
        proj = [jnp.dot(v, wb_ref[br], preferred_element_type=F32) for v in branches[br]]
        mix = _each(lambda m, g, p: m + g * p, mix, gate, proj)
    ln = ln_ref[...]
    mixed = [jnp.dot(m.astype(BF16), wo_ref[...], preferred_element_type=F32) for m in mix]
    for r, v, m in zip(halves, xs, mixed):
        o_ref[r, :] = _ln_rows(ALPHA * v + m, ln[0:1, :], ln[1:2, :])


def _merge(x, ya, yc, h_conv, e0, e1, h_sgu, conv_w, sgu_ln, wm, sgu_bias, wg, bgate, wb, wo, ln1, tm, seg):
    t = x.shape[0]
    prompt = seg >= tm
    tiles_per_seq = max(seg // tm, 1)
    z_rows = 8 if prompt else tm
    n_seq = t // seg if prompt else 0
    if prompt:
        cp_spec = pl.BlockSpec((8, 3 * MIX_W), lambda i: (jnp.maximum(i * (tm // 8) - 1, 0), 0))
        e_spec = _const_spec(e0.shape)
        z_spec = pl.BlockSpec((8, MIX_W), lambda i: (i // tiles_per_seq, 0))
        z_shape = jax.ShapeDtypeStruct((n_seq * 8, MIX_W), F32)
    else:
        cp_spec = pl.BlockSpec((8, 3 * MIX_W), lambda i: (0, 0))
        e_spec = pl.BlockSpec((tm, MIX_W), lambda i: (i, 0))
        z_spec = pl.BlockSpec((tm, MIX_W), lambda i: (i, 0))
        z_shape = jax.ShapeDtypeStruct((t, MIX_W), F32)
    row = lambda w: pl.BlockSpec((tm, w), lambda i: (i, 0))
    return pl.pallas_call(
        functools.partial(_merge_body, tm=tm, seg=seg, tiles_per_seq=tiles_per_seq),
        grid=(t // tm,),
        in_specs=[row(D_MODEL), row(MIX_W), row(MIX_W), row(3 * MIX_W), cp_spec, e_spec, e_spec, row(2 * MIX_W),
                  _const_spec(conv_w.shape), _const_spec(sgu_ln.shape), _const_spec(wm.shape),
                  _const_spec(sgu_bias.shape), _const_spec(wg.shape), _const_spec(bgate.shape),
                  _const_spec(wb.shape), _const_spec(wo.shape), _const_spec(ln1.shape)],
        out_specs=[row(D_MODEL), z_spec, z_spec],
        out_shape=[jax.ShapeDtypeStruct((t, D_MODEL), F32), z_shape, z_shape],
        compiler_params=_cparams("arbitrary"),
        name="merge",
    )(x, ya, yc, h_conv, h_conv, e0, e1, h_sgu, conv_w, sgu_ln, wm, sgu_bias, wg, bgate, wb, wo, ln1)


def _matmul2_body(x_ref, w1_ref, w2_ref, o1_ref, o2_ref):
    xb = x_ref[...].astype(BF16)
    o1_ref[...] = jnp.dot(xb, w1_ref[...], preferred_element_type=F32)
    o2_ref[...] = jnp.dot(xb, w2_ref[...], preferred_element_type=F32)


def _matmul2_layers(x, w1, w2, tm):
    t, kdim = x.shape
    n_layers, _, n = w1.shape
    w_spec = pl.BlockSpec((None, kdim, n), lambda l, i: (l, 0, 0))
    o_spec = pl.BlockSpec((None, tm, n), lambda l, i: (l, i, 0))
    o_shape = jax.ShapeDtypeStruct((n_layers, t, n), F32)
    return pl.pallas_call(
        _matmul2_body,
        grid=(n_layers, t // tm),
        in_specs=[pl.BlockSpec((tm, kdim), lambda l, i: (i, 0)), w_spec, w_spec],
        out_specs=[o_spec, o_spec],
        out_shape=[o_shape, o_shape],
        compiler_params=_cparams("parallel", "parallel"),
        name="matmul2",
    )(x, w1, w2)


def _matmul_body(x_ref, w_ref, o_ref):
    o_ref[...] = jnp.dot(x_ref[...].astype(BF16), w_ref[...], preferred_element_type=F32).astype(o_ref.dtype)


def _matmul(x, w, out_dtype, tm):
    t, kdim = x.shape
    n = w.shape[1]
    return pl.pallas_call(
        _matmul_body,
        grid=(t // tm,),
        in_specs=[pl.BlockSpec((tm, kdim), lambda i: (i, 0)), _const_spec(w.shape)],
        out_specs=pl.BlockSpec((tm, n), lambda i: (i, 0)),
        out_shape=jax.ShapeDtypeStruct((t, n), out_dtype),
        compiler_params=_cparams("parallel"),
        name="matmul",
    )(x, w)


def _softmax_rows(sc):
    m = jnp.max(sc, axis=-1, keepdims=True)
    e = jnp.exp(sc - m)
    return e / jnp.sum(e, axis=-1, keepdims=True)


def _xattn_rows_body(q_ref, k_ref, v_ref, o_ref, *, tq, slots):
    rows = X_HEADS * tq
    n_col = N_MEM * X_HEADS
    own = (_iota((rows, n_col), 1) & (X_HEADS - 1)) == (_iota((rows, n_col), 0) >> int(math.log2(tq)))
    for s in range(slots):
        q = q_ref[s * tq:(s + 1) * tq, :]
        qs = jnp.concatenate([q[:, h * X_HD:(h + 1) * X_HD] for h in range(X_HEADS)], axis=0).astype(BF16)
        k_all = k_ref[s].reshape(n_col, X_HD).astype(BF16)
        v_all = v_ref[s].reshape(n_col, X_HD).astype(BF16)
        sc = lax.dot_general(qs, k_all, (((1,), (1,)), ((), ())), preferred_element_type=F32) * (X_HD ** -0.5)
        pr = _softmax_rows(jnp.where(own, sc, -1e30))
        o = jnp.dot(pr.astype(BF16), v_all, preferred_element_type=F32)
        for h in range(X_HEADS):
            o_ref[s * tq:(s + 1) * tq, h * X_HD:(h + 1) * X_HD] = o[h * tq:(h + 1) * tq].astype(o_ref.dtype)


def _xattn_rows(q, mem_k, mem_v, layer, tq, slots):
    t = q.shape[0]
    rows = tq * slots
    mem_spec = pl.BlockSpec((None, slots, N_MEM, X_HEADS, X_HD), lambda i: (layer, i, 0, 0, 0))
    return pl.pallas_call(
        functools.partial(_xattn_rows_body, tq=tq, slots=slots),
        grid=(t // rows,),
        in_specs=[pl.BlockSpec((rows, D_MODEL), lambda i: (i, 0)), mem_spec, mem_spec],
        out_specs=pl.BlockSpec((rows, D_MODEL), lambda i: (i, 0)),
        out_shape=jax.ShapeDtypeStruct((t, D_MODEL), q.dtype),
        compiler_params=_cparams("parallel"),
        name="xattn_rows",
    )(q, mem_k, mem_v)


def _proj_ln_body(y_ref, w_ref, x_ref, ln_ref, o_ref):
    ln = ln_ref[...]
    acc = jnp.dot(y_ref[...].astype(BF16), w_ref[...], preferred_element_type=F32)
    o_ref[...] = _ln_rows(ALPHA * x_ref[...] + acc, ln[0:1, :], ln[1:2, :])


def _proj_ln(y, w, x, ln, tm):
    t = x.shape[0]
    return pl.pallas_call(
        _proj_ln_body,
        grid=(t // tm,),
        in_specs=[pl.BlockSpec((tm, D_MODEL), lambda i: (i, 0)), _const_spec(w.shape),
                  pl.BlockSpec((tm, D_MODEL), lambda i: (i, 0)), _const_spec(ln.shape)],
        out_specs=pl.BlockSpec((tm, D_MODEL), lambda i: (i, 0)),
        out_shape=jax.ShapeDtypeStruct((t, D_MODEL), F32),
        compiler_params=_cparams("parallel"),
        name="proj_ln",
    )(y, w, x, ln)


def _mlp_rows(x, wu_ref, wd_ref, ln):
    xb = x.astype(BF16)
    acc = jnp.zeros(x.shape, F32)
    for j in range(D_FF // D_MODEL):
        cols = slice(j * D_MODEL, (j + 1) * D_MODEL)
        hid = jnp.maximum(jnp.dot(xb, wu_ref[:, cols], preferred_element_type=F32), 0.0)
        acc = acc + jnp.dot((hid * hid).astype(BF16), wd_ref[cols, :], preferred_element_type=F32)
    return _ln_rows(ALPHA * x + acc, ln[0:1, :], ln[1:2, :])


def _mlp_body(x_ref, wu_ref, wd_ref, ln_ref, o_ref):
    o_ref[...] = _mlp_rows(x_ref[...], wu_ref, wd_ref, ln_ref[...])


def _mlp(x, wu, wd, ln, tm):
    t = x.shape[0]
    return pl.pallas_call(
        _mlp_body,
        grid=(t // tm,),
        in_specs=[pl.BlockSpec((tm, D_MODEL), lambda i: (i, 0)), _const_spec(wu.shape), _const_spec(wd.shape),
                  _const_spec(ln.shape)],
        out_specs=pl.BlockSpec((tm, D_MODEL), lambda i: (i, 0)),
        out_shape=jax.ShapeDtypeStruct((t, D_MODEL), F32),
        compiler_params=_cparams("parallel"),
        name="mlp",
    )(x, wu, wd, ln)


def _layer_params(l, w_in, sb_bias, w_gate, b_gate, w_branch, w_o, conv_w, rw_mu, rw_w0, rw_w2, rw_a0, rw_a2, rw_g2,
                  rw_kk, rw_ka, rw_rk, rw_gn_g, rw_gn_b, sgu_ln_g, sgu_ln_b, sgu_ws, sgu_b, w_mq, w_mo,
                  w_up, w_down, ln1_g, ln1_b, ln2_g, ln2_b, ln3_g, ln3_b):
    wi = w_in[l].astype(BF16)
    off_b, off_c, off_d = 3 * MIX_W, 6 * MIX_W, 6 * MIX_W + RW_COLS
    zpad = lambda w, r0: jnp.zeros((128, MIX_W), F32).at[r0:r0 + w.shape[0]].set(w).astype(BF16)
    par = jnp.zeros((8, MIX_W), F32)
    for i, vec in enumerate((rw_w0[l], rw_a0[l], rw_kk[l], rw_ka[l], rw_rk[l].reshape(MIX_W), rw_gn_g[l], rw_gn_b[l])):
        par = par.at[i].set(vec)
    return dict(
        w_in=(wi[:, :MIX_W], wi[:, MIX_W:2 * MIX_W], wi[:, 2 * MIX_W:off_b], wi[:, off_b:off_c], wi[:, off_c:off_d],
              wi[:, off_d:]),
        sb_bias=sb_bias[l],
        w_gate=w_gate[l].astype(BF16), b_gate=b_gate[l].reshape(1, -1), w_branch=w_branch[l].astype(BF16),
        w_o=w_o[l].astype(BF16), conv_w=jnp.zeros((8, MIX_W), F32).at[:3].set(conv_w[l]),
        rw_mu=rw_mu[l].reshape(1, RW_COLS), rw_par=par,
        rw_w2=zpad(rw_w2[l], 0), rw_a2=zpad(rw_a2[l], 32), rw_g2=zpad(rw_g2[l], 64),
        sgu_ln=jnp.stack([sgu_ln_g[l], sgu_ln_b[l]]), sgu_ws=sgu_ws[l], sgu_b=sgu_b[l],
        w_mq=w_mq[l].astype(BF16), w_mo=w_mo[l].astype(BF16),
        w_up=w_up[l].astype(BF16), w_down=w_down[l].astype(BF16),
        ln1=jnp.stack([ln1_g[l], ln1_b[l]]), ln2=jnp.stack([ln2_g[l], ln2_b[l]]), ln3=jnp.stack([ln3_g[l], ln3_b[l]]),
    )


def _sgu_tables(lp, seg):
    ws = lp['sgu_ws']
    sb = lp['sgu_b']
    if seg < SGU_CHUNK:
        reps = SGU_CHUNK // seg
        ws = jnp.tile(ws[:, :seg, :seg], (1, reps, reps))
        sb = jnp.tile(sb[:, :seg], (1, reps))
    wm = jnp.transpose(ws, (1, 0, 2)).reshape(SGU_CHUNK, N_HEADS * SGU_CHUNK)
    bias = jnp.repeat(sb.T, HEAD_W, axis=1)
    return wm, bias


def _sb_bias_rows(bias, tq, width):
    return jnp.broadcast_to(jnp.repeat(bias, tq)[:, None], (N_HEADS * tq, width)).astype(F32)


def _tail_fused_body(x_ref, wq_ref, k_ref, v_ref, wo_ref, ln2_ref, wu_ref, wd_ref, ln3_ref, o_ref):
    x = x_ref[...]
    q = jnp.dot(x.astype(BF16), wq_ref[...], preferred_element_type=F32).astype(BF16)
    kb = k_ref[...].astype(BF16)
    vb = v_ref[...].astype(BF16)
    ctx = []
    for h in range(X_HEADS):
        cols = slice(h * X_HD, (h + 1) * X_HD)
        sc = lax.dot_general(q[:, cols], kb[:, cols], (((1,), (1,)), ((), ())),
                             preferred_element_type=F32) * (X_HD ** -0.5)
        ctx.append(jnp.dot(_softmax_rows(sc).astype(BF16), vb[:, cols], preferred_element_type=F32).astype(BF16))
    att = jnp.concatenate(ctx, axis=1)
    ln2 = ln2_ref[...]
    x2 = _ln_rows(ALPHA * x + jnp.dot(att, wo_ref[...], preferred_element_type=F32), ln2[0:1, :], ln2[1:2, :])
    o_ref[...] = _mlp_rows(x2, wu_ref, wd_ref, ln3_ref[...])


def _tail_fused(x, lp, mem_k, mem_v, layer, tm):
    t = x.shape[0]
    tiles_per_mem = t // (mem_k.shape[1] // N_MEM) // tm
    mem_spec = pl.BlockSpec((None, N_MEM, D_MODEL), lambda i: (layer, i // tiles_per_mem, 0))
    weights = (lp['w_mq'], lp['w_mo'], lp['ln2'], lp['w_up'], lp['w_down'], lp['ln3'])
    wq, wo, ln2, wu, wd, ln3 = (_const_spec(w.shape) for w in weights)
    return pl.pallas_call(
        _tail_fused_body,
        grid=(t // tm,),
        in_specs=[pl.BlockSpec((tm, D_MODEL), lambda i: (i, 0)), wq, mem_spec, mem_spec, wo, ln2, wu, wd, ln3],
        out_specs=pl.BlockSpec((tm, D_MODEL), lambda i: (i, 0)),
        out_shape=jax.ShapeDtypeStruct((t, D_MODEL), F32),
        compiler_params=_cparams("parallel"),
        name="tail",
    )(x, lp['w_mq'], mem_k, mem_v, lp['w_mo'], lp['ln2'], lp['w_up'], lp['w_down'], lp['ln3'])


def _tail(x, lp, attend, q_dtype, tm):
    qm = _matmul(x, lp['w_mq'], q_dtype, tm)
    x = _proj_ln(attend(qm), lp['w_mo'], x, lp['ln2'], tm)
    return _mlp(x, lp['w_up'], lp['w_down'], lp['ln3'], tm)


def _layer_prompt(x, lp, mem_k, mem_v, layer, n, seq):
    t = n * seq
    q, k, v, h_conv, h_rw, h_sgu = _inproj(x, lp['w_in'], ROW_TILE)
    tq = SB_BLOCK
    ya = _sb_prompt(q, k, v, _sb_bias_rows(lp['sb_bias'], tq, tq), n, seq, tq)
    pe = jnp.zeros((n, 8, RW_COLS), F32)
    s0 = jnp.zeros((n, MIX_W, HEAD_W), F32)
    yc, s_fin = _rwkv(h_rw.reshape(n, seq, RW_COLS), pe, s0, lp['rw_mu'], lp['rw_par'], lp['rw_w2'], lp['rw_a2'],
                      lp['rw_g2'], n_prob=n, nb=1, c=RW_ROWS, chain=2)
    wm, sgu_bias = _sgu_tables(lp, SGU_CHUNK)
    zero_e = jnp.zeros((8, MIX_W), F32)
    x1, z_tail, _ = _merge(x, ya, yc.reshape(t, MIX_W), h_conv, zero_e, zero_e, h_sgu, lp['conv_w'], lp['sgu_ln'],
                           wm, sgu_bias, lp['w_gate'], lp['b_gate'], lp['w_branch'], lp['w_o'], lp['ln1'],
                           tm=ROW_TILE, seg=seq)
    x3 = _tail_fused(x1, lp, mem_k, mem_v, layer, ROW_TILE)
    k_new = k.reshape(n, seq, N_HEADS, HEAD_W)
    v_new = v.reshape(n, seq, N_HEADS, HEAD_W)
    conv_new = z_tail.reshape(n, 8, MIX_W)[:, 6:8]
    shift_new = h_rw.reshape(n, seq, RW_COLS)[:, -1]
    return x3, k_new, v_new, conv_new, shift_new, s_fin.reshape(n, N_HEADS, HEAD_W, HEAD_W)


def _layer_sample(x, lp, layer, mem_k, mem_v, cache_kt, cache_vt, page_table, state_conv, state_shift, state_wkv,
                  nb, seq):
    t = nb * seq
    q, k_new, v_new, h_conv, h_rw, h_sgu = _inproj(x, lp['w_in'], ROW_TILE)
    ya = _sb_sample(q, k_new, v_new, cache_kt, cache_vt, page_table, _sb_bias_rows(lp['sb_bias'], seq, PAGE_SIZE),
                    layer, nb, seq)
    per = RW_ROWS // seq
    pe = jnp.repeat(state_shift, seq, axis=0).reshape(nb // per, RW_ROWS, RW_COLS)
    yc, s_fin = _rwkv(h_rw.reshape(nb // per, RW_ROWS, RW_COLS), pe, state_wkv.reshape(nb, MIX_W, HEAD_W), lp['rw_mu'],
                      lp['rw_par'], lp['rw_w2'], lp['rw_a2'], lp['rw_g2'], n_prob=2, nb=per, c=seq, chain=1)
    wm, sgu_bias = _sgu_tables(lp, seq)
    e0 = jnp.repeat(state_conv[:, 0], seq, axis=0)
    e1 = jnp.repeat(state_conv[:, 1], seq, axis=0)
    x1, z_all, sgu_v = _merge(x, ya, yc.reshape(t, MIX_W), h_conv, e0, e1, h_sgu, lp['conv_w'], lp['sgu_ln'], wm,
                              sgu_bias, lp['w_gate'], lp['b_gate'], lp['w_branch'], lp['w_o'], lp['ln1'],
                              tm=ROW_TILE, seg=seq)
    x3 = _tail(x1, lp, lambda qm: _xattn_rows(qm, mem_k, mem_v, layer, seq, XATTN_SLOTS), F32, ROW_TILE)
    conv_new = z_all.reshape(nb, seq, MIX_W)[:, seq - 2:]
    shift_new = h_rw.reshape(nb, seq, RW_COLS)[:, -1]
    return (x3, k_new.reshape(nb, seq, N_HEADS, HEAD_W), v_new.reshape(nb, seq, N_HEADS, HEAD_W), conv_new,
            shift_new, s_fin.reshape(nb, N_HEADS, HEAD_W, HEAD_W), sgu_v.reshape(nb, seq, MIX_W))


def kernel(x_prompt, x_sample, mem_prompt, cache_k, cache_v, page_table, cache_mem_k, cache_mem_v, state_conv,
           state_wkv, state_shift, w_in, sb_bias, w_gate, b_gate, w_branch, w_o, conv_w, rw_mu, rw_w0, rw_w2, rw_a0,
           rw_a2, rw_g2, rw_kk, rw_ka, rw_rk, rw_gn_g, rw_gn_b, sgu_ln_g, sgu_ln_b, sgu_ws, sgu_b, w_mq, w_mk, w_mv,
           w_mo, w_up, w_down, ln1_g, ln1_b, ln2_g, ln2_b, ln3_g, ln3_b):
    n_p, seq_p, _ = x_prompt.shape
    n_s, seq_s, _ = x_sample.shape
    n_phys = cache_k.shape[1]
    xp = x_prompt.reshape(n_p * seq_p, D_MODEL)
    xs = x_sample.reshape(n_s * seq_s, D_MODEL)
    mem2d = mem_prompt.reshape(n_p * N_MEM, D_MODEL)
    cache_kt = jnp.transpose(cache_k, (0, 1, 3, 4, 2)).reshape(DEPTH, n_phys, MIX_W, PAGE_SIZE)
    cache_vt = jnp.transpose(cache_v, (0, 1, 3, 4, 2)).reshape(DEPTH, n_phys, MIX_W, PAGE_SIZE)
    mk, mv = _matmul2_layers(mem2d, w_mk.astype(BF16), w_mv.astype(BF16), ROW_TILE)
    outs = [[] for _ in range(11)]
    for l in range(DEPTH):
        lp = _layer_params(l, w_in, sb_bias, w_gate, b_gate, w_branch, w_o, conv_w, rw_mu, rw_w0, rw_w2, rw_a0,
                           rw_a2, rw_g2, rw_kk, rw_ka, rw_rk, rw_gn_g, rw_gn_b, sgu_ln_g, sgu_ln_b, sgu_ws, sgu_b,
                           w_mq, w_mo, w_up, w_down, ln1_g, ln1_b, ln2_g, ln2_b, ln3_g, ln3_b)
        xp, pk, pv, pc, psh, pst = _layer_prompt(xp, lp, mk, mv, l, n_p, seq_p)
        xs, sk, sv, sc, ssh, sst, scv = _layer_sample(
            xs, lp, l, cache_mem_k, cache_mem_v, cache_kt, cache_vt, page_table,
            state_conv[l], state_shift[l], state_wkv[l], n_s, seq_s)
        for lst, val in zip(outs, (pk, pv, pc, pst, psh, sk, sv, sc, sst, ssh, scv)):
            lst.append(val)
    p_k, p_v, p_conv, p_wkv, p_shift, s_k, s_v, s_conv, s_wkv, s_shift, s_chunk = (jnp.stack(o) for o in outs)
    mem_shape = (DEPTH, n_p, N_MEM, X_HEADS, X_HD)
    return (xp.reshape(n_p, seq_p, D_MODEL), xs.reshape(n_s, seq_s, D_MODEL), p_k, p_v, mk.reshape(mem_shape),
            mv.reshape(mem_shape), p_conv, p_wkv, p_shift, s_k, s_v, s_conv, s_wkv, s_shift, s_chunk)
```

```python
import functools
import math

import jax
import jax.numpy as jnp
from jax import lax
from jax.experimental import pallas as pl
from jax.experimental.pallas import tpu as pltpu

F32 = jnp.float32
BF16 = jnp.bfloat16

D_MODEL = 1024
DEPTH = 2
MIX_W = 256
HEAD_W = 64
N_HEADS = MIX_W // HEAD_W
RW_COLS = 896
PAGE_SIZE = 128
N_MEM = 256
X_HEADS = 4
X_HD = D_MODEL // X_HEADS
D_FF = 4 * D_MODEL
ALPHA = (2 * DEPTH) ** 0.25
LN_EPS = 1e-5
RW_GN_EPS = 64e-5
SGU_CHUNK = 128
RW_ROWS = 64
ROW_TILE = 512
SB_BLOCK = 256
XATTN_SLOTS = 8
VMEM_LIMIT = 56 * 1024 * 1024


def _cparams(*sem):
    return pltpu.CompilerParams(dimension_semantics=sem, vmem_limit_bytes=VMEM_LIMIT)


def _const_spec(shape):
    nd = len(shape)
    return pl.BlockSpec(shape, lambda *_: (0,) * nd, pipeline_mode=pl.Buffered(1))


def _bdot(a, b):
    return jnp.dot(a.astype(BF16), b.astype(BF16), preferred_element_type=F32)


def _bdot_nt(a, b):
    return lax.dot_general(a.astype(BF16), b.astype(BF16), (((1,), (1,)), ((), ())),
                           preferred_element_type=F32)


def _bdot_tn(a, b):
    return lax.dot_general(a.astype(BF16), b.astype(BF16), (((0,), (0,)), ((), ())),
                           preferred_element_type=F32)


def _split2(x):
    hi = x.astype(BF16)
    lo = (x - hi.astype(F32)).astype(BF16)
    return hi, lo


def _split3(x):
    hi = x.astype(BF16)
    r1 = x - hi.astype(F32)
    mid = r1.astype(BF16)
    lo = (r1 - mid.astype(F32)).astype(BF16)
    return hi, mid, lo


def _sel_dot_l(sel_b, x):
    return sum(jnp.dot(sel_b, part, preferred_element_type=F32) for part in _split3(x))


def _sel_dot_r(x, sel_b):
    return sum(jnp.dot(part, sel_b, preferred_element_type=F32) for part in _split2(x))


def _dot3(a, b):
    ah, al = _split2(a)
    bh, bl = _split2(b)
    d = lambda x, y: jnp.dot(x, y, preferred_element_type=F32)
    return d(ah, bh) + d(ah, bl) + d(al, bh)


def _sigmoid(x):
    return 1.0 / (1.0 + jnp.exp(-x))


def _softplus(x):
    return jnp.maximum(x, 0.0) + jnp.log(1.0 + jnp.exp2(jnp.abs(x) * (-1.0 / math.log(2.0))))


def _gelu_tanh(x):
    return 0.5 * x * (1.0 + jnp.tanh(0.7978845608028654 * (x + 0.044715 * (x * x * x))))


def _ln_rows(x, g, b, eps=LN_EPS):
    mu = jnp.mean(x, axis=-1, keepdims=True)
    xc = x - mu
    var = jnp.mean(xc * xc, axis=-1, keepdims=True)
    return xc * lax.rsqrt(var + eps) * g + b


def _iota(shape, dim):
    return lax.broadcasted_iota(jnp.int32, shape, dim)


def _head_block_mask(rows, row_shift):
    return (_iota((rows, MIX_W), 0) >> row_shift) == (_iota((rows, MIX_W), 1) >> 6)


def _stack_heads(x, mask):
    t = jnp.concatenate([x] * N_HEADS, axis=0)
    return jnp.where(mask, t, jnp.zeros_like(t))


def _unstack_heads(x, rows):
    return x[0:rows] + x[rows:2 * rows] + x[2 * rows:3 * rows] + x[3 * rows:4 * rows]


def _inproj_body(x_ref, *refs):
    n = len(refs) // 2
    xb = x_ref[...].astype(BF16)
    for w_ref, o_ref in zip(refs[:n], refs[n:]):
        o_ref[...] = jnp.dot(xb, w_ref[...], preferred_element_type=F32)


def _inproj(x, weights, tm):
    t = x.shape[0]
    widths = [w.shape[1] for w in weights]
    return pl.pallas_call(
        _inproj_body,
        grid=(t // tm,),
        in_specs=[pl.BlockSpec((tm, D_MODEL), lambda i: (i, 0))] + [_const_spec(w.shape) for w in weights],
        out_specs=[pl.BlockSpec((tm, w), lambda i: (i, 0)) for w in widths],
        out_shape=[jax.ShapeDtypeStruct((t, w), F32) for w in widths],
        compiler_params=_cparams("parallel"),
        name="inproj",
    )(x, *weights)


def _suffix_ones(tk):
    return (_iota((tk, tk), 0) >= _iota((tk, tk), 1)).astype(BF16)


def _sb_weights(z, carry, suffix, mask):
    tk = z.shape[1]
    stop = _softplus(z)
    if mask is not None:
        stop = jnp.where(mask, stop, 0.0)
    cs = jnp.dot(stop.astype(BF16), suffix, preferred_element_type=F32)
    later = carry if tk == 128 else jnp.concatenate([carry] * (tk // 128), axis=1)
    w = jnp.exp(z - cs - later)
    if mask is not None:
        w = jnp.where(mask, w, 0.0)
    total = jnp.sum(stop, axis=1, keepdims=True)
    return w, carry + jnp.broadcast_to(total, carry.shape)


def _sb_prompt_body(q_ref, k_ref, v_ref, bias_ref, o_ref, kb, vb, carry_ref, acc_ref, *, tq):
    qi = pl.program_id(1)
    rows = N_HEADS * tq

    @pl.when(qi == 0)
    def _():
        kb[...] = k_ref[...].astype(BF16)
        vb[...] = v_ref[...].astype(BF16)

    hmask = _head_block_mask(rows, int(math.log2(tq)))
    qs = _stack_heads((q_ref[...] * (HEAD_W ** -0.5)).astype(BF16), hmask)
    suffix = _suffix_ones(tq)

    def block(start, carry, acc, mask):
        z = lax.dot_general(qs, kb[pl.ds(start, tq), :], (((1,), (1,)), ((), ())),
                            preferred_element_type=F32) + bias_ref[...]
        w, carry = _sb_weights(z, carry, suffix, mask)
        acc = acc + jnp.dot(w.astype(BF16), vb[pl.ds(start, tq), :], preferred_element_type=F32)
        return carry, acc

    def block_pair(start):
        k2 = kb[pl.ds(start, 2 * tq), :]
        v2 = vb[pl.ds(start, 2 * tq), :]
        hrows = lambda h: slice(h * tq, (h + 1) * tq)
        logits = lambda h: lax.dot_general(qs[hrows(h)], k2, (((1,), (1,)), ((), ())),
                                           preferred_element_type=F32)
        z2 = [logits(0), logits(1)]
        for h in range(N_HEADS):
            bias = bias_ref[hrows(h), :]
            z = jnp.concatenate([z2[h][:, tq:] + bias, z2[h][:, :tq] + bias], axis=0)
            stop = _softplus(z)
            cs = jnp.dot(stop.astype(BF16), suffix, preferred_element_type=F32)
            if h + 2 < N_HEADS:
                z2.append(logits(h + 2))
            total = jnp.broadcast_to(jnp.sum(stop, axis=1, keepdims=True), (2 * tq, 128))
            carry = carry_ref[hrows(h), :]
            carry2 = jnp.concatenate([carry, carry + total[:tq]], axis=0)
            later = jnp.concatenate([carry2] * (tq // 128), axis=1)
            w = jnp.exp(z - cs - later).astype(BF16)
            w2 = jnp.concatenate([w[tq:], w[:tq]], axis=1)
            acc_ref[hrows(h), :] += jnp.dot(w2, v2, preferred_element_type=F32)
            carry_ref[hrows(h), :] = carry2[tq:] + total[tq:]

    t_idx = _iota((rows, tq), 0) & (tq - 1)
    causal = _iota((rows, tq), 1) < t_idx
    carry, acc = block(pl.multiple_of(qi * tq, tq), jnp.zeros((rows, 128), F32),
                       jnp.zeros((rows, MIX_W), F32), causal)
    carry_ref[...] = carry
    acc_ref[...] = acc

    @pl.when((qi & 1) == 1)
    def _():
        c, a = block(pl.multiple_of((qi - 1) * tq, tq), carry_ref[...], acc_ref[...], None)
        carry_ref[...] = c
        acc_ref[...] = a

    n_pairs = qi >> 1

    def step(jj, _):
        block_pair(pl.multiple_of((n_pairs - 1 - jj) * (2 * tq), 2 * tq))
        return 0

    lax.fori_loop(0, n_pairs, step, 0)
    acc = jnp.where(hmask, acc_ref[...], 0.0)
    o_ref[...] = _unstack_heads(acc, tq)


def _sb_prompt(q, k, v, bias_rows, n, seq, tq):
    nq = seq // tq
    rows = N_HEADS * tq
    return pl.pallas_call(
        functools.partial(_sb_prompt_body, tq=tq),
        grid=(n, nq),
        in_specs=[
            pl.BlockSpec((tq, MIX_W), lambda b, i: (b * nq + i, 0)),
            pl.BlockSpec((seq, MIX_W), lambda b, i: (b, 0)),
            pl.BlockSpec((seq, MIX_W), lambda b, i: (b, 0)),
            _const_spec((rows, tq)),
        ],
        out_specs=pl.BlockSpec((tq, MIX_W), lambda b, i: (b * nq + i, 0)),
        out_shape=jax.ShapeDtypeStruct((n * seq, MIX_W), F32),
        scratch_shapes=[pltpu.VMEM((seq, MIX_W), BF16), pltpu.VMEM((seq, MIX_W), BF16),
                        pltpu.VMEM((rows, 128), F32), pltpu.VMEM((rows, MIX_W), F32)],
        compiler_params=_cparams("parallel", "arbitrary"),
        name="sb_prompt",
    )(q, k, v, bias_rows)


SB_SEQ_SLOTS = 4


def _sb_sample_body(pt_ref, q_ref, kn_ref, vn_ref, ck_ref, cv_ref, bias_ref, o_ref, kbuf, vbuf, sem,
                    *, layer, tq, n_pages):
    b = pl.program_id(0)
    rows = N_HEADS * tq
    n_blk = n_pages + 1

    def seq_copies(s):
        slot = lax.rem(s, SB_SEQ_SLOTS)
        out = []
        for j in range(n_pages):
            page = pt_ref[s, n_pages - 1 - j]
            out.append(pltpu.make_async_copy(ck_ref.at[layer, page], kbuf.at[slot, j], sem.at[slot]))
            out.append(pltpu.make_async_copy(cv_ref.at[layer, page], vbuf.at[slot, j], sem.at[slot]))
        return slot, out

    @pl.when(b == 0)
    def _():
        for s in range(SB_SEQ_SLOTS - 1):
            for cp in seq_copies(jnp.int32(s))[1]:
                cp.start()

    @pl.when(b + (SB_SEQ_SLOTS - 1) < pl.num_programs(0))
    def _():
        for cp in seq_copies(b + (SB_SEQ_SLOTS - 1))[1]:
            cp.start()

    slot, copies = seq_copies(b)
    for cp in copies:
        cp.wait()

    hmask = _head_block_mask(rows, int(math.log2(tq)))
    qs = _stack_heads(q_ref[...] * (HEAD_W ** -0.5), hmask).astype(BF16)
    bias = bias_ref[...]

    pad = jnp.zeros((PAGE_SIZE - tq, MIX_W), F32)
    k_new = jnp.concatenate([kn_ref[...], pad], axis=0).astype(BF16)
    v_new = jnp.concatenate([vn_ref[...], pad], axis=0).astype(BF16)
    z_new = lax.dot_general(qs, k_new, (((1,), (1,)), ((), ())), preferred_element_type=F32)
    k_pages = jnp.concatenate([kbuf[slot, j].astype(BF16) for j in range(n_pages)], axis=1)
    z_pages = jnp.dot(qs, k_pages, preferred_element_type=F32)
    z = jnp.concatenate([z_new] + [z_pages[:, j * PAGE_SIZE:(j + 1) * PAGE_SIZE] for j in range(n_pages)], axis=0)
    z = z + jnp.concatenate([bias] * n_blk, axis=0)

    r_i = _iota((n_blk * rows, PAGE_SIZE), 0)
    valid = (r_i >= rows) | (_iota((n_blk * rows, PAGE_SIZE), 1) < (r_i & (tq - 1)))
    stop = jnp.where(valid, _softplus(z), 0.0)
    cs = jnp.dot(stop.astype(BF16), _suffix_ones(PAGE_SIZE), preferred_element_type=F32)
    total = jnp.broadcast_to(jnp.sum(stop, axis=1, keepdims=True), (n_blk * rows, PAGE_SIZE))
    carry = [jnp.zeros((rows, PAGE_SIZE), F32)]
    for i in range(n_blk - 1):
        carry.append(carry[-1] + total[i * rows:(i + 1) * rows])
    w = jnp.where(valid, jnp.exp(z - cs - jnp.concatenate(carry, axis=0)), 0.0).astype(BF16)

    acc = jnp.dot(w[:rows], v_new, preferred_element_type=F32)
    w_pages = jnp.concatenate([w[(j + 1) * rows:(j + 2) * rows] for j in range(n_pages)], axis=1)
    v_pages = jnp.concatenate([vbuf[slot, j].astype(BF16) for j in range(n_pages)], axis=1)
    acc = acc + lax.dot_general(w_pages, v_pages, (((1,), (1,)), ((), ())), preferred_element_type=F32)
    o_ref[...] = _unstack_heads(jnp.where(hmask, acc, 0.0), tq)


def _sb_sample(q, k, v, cache_kt, cache_vt, page_table, bias_rows, layer, nb, tq):
    n_pages = page_table.shape[1]
    rows = N_HEADS * tq
    buf = (SB_SEQ_SLOTS, n_pages, MIX_W, PAGE_SIZE)
    grid_spec = pltpu.PrefetchScalarGridSpec(
        num_scalar_prefetch=1,
        grid=(nb,),
        in_specs=[
            pl.BlockSpec((tq, MIX_W), lambda b, pt: (b, 0)),
            pl.BlockSpec((tq, MIX_W), lambda b, pt: (b, 0)),
            pl.BlockSpec((tq, MIX_W), lambda b, pt: (b, 0)),
            pl.BlockSpec(memory_space=pl.ANY),
            pl.BlockSpec(memory_space=pl.ANY),
            pl.BlockSpec((rows, PAGE_SIZE), lambda b, pt: (0, 0)),
        ],
        out_specs=pl.BlockSpec((tq, MIX_W), lambda b, pt: (b, 0)),
        scratch_shapes=[pltpu.VMEM(buf, F32), pltpu.VMEM(buf, F32), pltpu.SemaphoreType.DMA((SB_SEQ_SLOTS,))],
    )
    return pl.pallas_call(
        functools.partial(_sb_sample_body, layer=layer, tq=tq, n_pages=n_pages),
        grid_spec=grid_spec,
        out_shape=jax.ShapeDtypeStruct((nb * tq, MIX_W), F32),
        compiler_params=_cparams("arbitrary"),
        name="sb_sample",
    )(page_table, q, k, v, cache_kt, cache_vt, bias_rows)


def _each(fn, *lists):
    return [fn(*args) for args in zip(*lists)]


def _rwkv_problems(ps, p_prevs, s_lists, mu, par, w2p, a2p, g2p, masks, *, nb, c, chain):
    ones_bd, bd, m_strict, m_incl, eye_cat, l_tri, l_all, eye_s = masks
    w0, a0, kk_s, ka_s, rk, gn_g, gn_b = (par[i:i + 1, :] for i in range(7))
    mm = lambda x, y: jnp.dot(x.astype(BF16), y.astype(BF16), preferred_element_type=F32)
    expand = lambda x: _stack_heads(x.astype(BF16), bd)
    left = lambda x: x[:, :MIX_W]
    right = lambda x: x[:, MIX_W:]

    n_p = len(ps)

    def rows_batched(fn, xs):
        out = fn(xs[0] if n_p == 1 else jnp.concatenate(xs, axis=0))
        return [out[i * RW_ROWS:(i + 1) * RW_ROWS] for i in range(n_p)]

    head_sums = lambda xs: rows_batched(lambda x: _sel_dot_r(x, ones_bd), xs)

    xs = _each(lambda p, pp: p + (pp - p) * mu, ps, p_prevs)
    r = [x[:, 0:256] for x in xs]
    k = [x[:, 256:512] for x in xs]
    v = [x[:, 512:768] for x in xs]
    lora = [x[:, 768:896] for x in xs]
    u = rows_batched(lambda x: w0 + mm(jnp.tanh(x), w2p), lora)
    logw = [(-math.exp(-0.5)) * _sigmoid(x) for x in u]
    asig = rows_batched(lambda x: _sigmoid(a0 + mm(x, a2p)), lora)
    gate = rows_batched(lambda x: mm(_sigmoid(x), g2p), lora)
    kk = [x * kk_s for x in k]
    ssq = head_sums([x * x for x in kk])
    kk = _each(lambda x, q: x / jnp.maximum(jnp.sqrt(q), 1e-12), kk, ssq)
    k_eff = _each(lambda x, s: x * (1.0 + (s - 1.0) * ka_s), k, asig)
    rk_sum = head_sums(_each(lambda x, y: x * y * rk, r, k_eff))
    bonus = _each(lambda x, y: x * y, rk_sum, v)
    b = _each(lambda x, s: x * s, kk, asig)

    sums = _sel_dot_l(jnp.concatenate([l_tri, l_all], axis=0), jnp.concatenate(logw, axis=1))
    cum = [sums[:RW_ROWS, i * MIX_W:(i + 1) * MIX_W] for i in range(n_p)]
    cum_all = [sums[RW_ROWS:, i * MIX_W:(i + 1) * MIX_W] for i in range(n_p)]
    rt = _each(lambda x, q: (x * jnp.exp(q)).astype(BF16), r, cum)
    at = _each(lambda x, q, lw: (-x * jnp.exp(q - lw)).astype(BF16), kk, cum, logw)
    ginv = [jnp.exp(-q) for q in cum]
    bt = _each(lambda x, g: x * g, b, ginv)
    kt = _each(lambda x, g: x * g, k_eff, ginv)
    e_tail = _each(lambda qa, q: jnp.exp(qa - q), cum_all, cum)
    bg = _each(lambda x, e: x * e, b, e_tail)
    kg = _each(lambda x, e: x * e, k_eff, e_tail)
    g_end = [jnp.exp(q) for q in cum_all]

    lhs = _each(lambda x, y: jnp.concatenate([x, y], axis=0), at, rt)
    rhs = _each(lambda x, y: jnp.concatenate([expand(x), expand(y)], axis=0), bt, kt)
    a_cat = _each(lambda x, y: lax.dot_general(x, y, (((1,), (1,)), ((), ())), preferred_element_type=F32),
                  lhs, rhs)
    a_ab = [jnp.where(m_strict, left(x[:RW_ROWS]), 0.0) for x in a_cat]
    a_ak = [jnp.where(m_strict, right(x[:RW_ROWS]), 0.0) for x in a_cat]
    a_rb = [jnp.where(m_incl, left(x[RW_ROWS:]), 0.0) for x in a_cat]
    a_rk = [jnp.where(m_incl, right(x[RW_ROWS:]), 0.0) for x in a_cat]

    inv = [eye_cat + x for x in a_ab]
    apow = a_ab
    apow_bd = [expand(x) for x in apow]
    span = 1
    while 2 * span < c:
        apow = _each(mm, apow, apow_bd)
        apow_bd = [expand(x) for x in apow]
        inv = _each(lambda x, y: x + mm(x, y), inv, apow_bd)
        span *= 2

    sv = [expand(x) for x in v]
    w1 = _each(mm, a_ak, sv)
    uu = _each(lambda i, x, y: mm(i, jnp.concatenate([expand(x), expand(y)], axis=1)), inv, at, w1)
    ua = [left(x) for x in uu]
    uv = [right(x) for x in uu]
    qy = _each(lambda m, x, y: mm(m, jnp.concatenate([expand(x), expand(y)], axis=1)), a_rb, ua, uv)
    y0 = _each(lambda q, m, x: right(q) + mm(m, x), qy, a_rk, sv)
    qe = _each(lambda x, q: x.astype(F32) + left(q), rt, qy)

    row_seq = _iota((RW_ROWS, MIX_W), 0) >> int(math.log2(c))
    m_c, n_c = [], []
    for s in range(nb):
        own = lambda x: x if nb == 1 else jnp.where(row_seq == s, x, 0.0)
        bg_s = [own(x) for x in bg]
        kg_s = [own(x) for x in kg]
        m_c.append(_each(lambda x, y, g: jnp.where(bd, _bdot_tn(x, y), 0.0)
                         + jnp.where(eye_s, g[s * c:s * c + 1, :], 0.0), ua, bg_s, g_end))
        n_c.append(_each(lambda x, y, z, w: jnp.where(bd, _bdot_tn(x, y) + _bdot_tn(z, w), 0.0),
                         uv, bg_s, v, kg_s))

    states = [list(sl) for sl in s_lists]
    y_parts = [[] for _ in ps]
    for kk in range(chain):
        idx = [g * chain + kk for g in range(len(s_lists))]
        pick = lambda xs: [xs[i] for i in idx]
        for s in range(nb):
            rows = slice(s * c, (s + 1) * c)
            cur = [st[s] for st in states]
            y_s = _each(lambda q, st, y: _bdot_nt(q[rows], st) + y[rows], pick(qe), cur, pick(y0))
            nxt = _each(lambda st, m, n: _dot3(st, m) + n, cur, pick(m_c[s]), pick(n_c[s]))
            for gi, i in enumerate(idx):
                y_parts[i].append(y_s[gi])
                states[gi][s] = nxt[gi]
    s_new = states
    y = [parts[0] if nb == 1 else jnp.concatenate(parts, axis=0) for parts in y_parts]

    mean = [x * (1.0 / HEAD_W) for x in head_sums(y)]
    d = _each(lambda x, m: x - m, y, mean)
    var = [x * (1.0 / HEAD_W) for x in head_sums([x * x for x in d])]
    out = _each(lambda x, q, bo, g: (x * lax.rsqrt(q + RW_GN_EPS) * gn_g + gn_b + bo) * g, d, var, bonus, gate)
    return out, s_new


def _rwkv_body(p_ref, pe_ref, s0_ref, mu_ref, par_ref, w2_ref, a2_ref, g2_ref, y_ref, so_ref, s_scr, plast_scr,
               *, n_prob, nb, c, chain):
    ci = pl.program_id(1)
    g_rows = RW_ROWS
    r4 = N_HEADS * g_rows
    log_c = int(math.log2(c))

    bd = _head_block_mask(r4, 6)

    @pl.when(ci == 0)
    def _():
        for idx in range(n_prob * nb):
            s_scr[idx] = jnp.where(bd, jnp.concatenate([s0_ref[idx]] * N_HEADS, axis=1), 0.0)
        if nb == 1:
            plast_scr[...] = pe_ref[...]

    ones_bd = bd.astype(BF16)
    t_i = _iota((g_rows, r4), 0)
    s_i = _iota((g_rows, r4), 1) & (g_rows - 1)
    same = (t_i >> log_c) == (s_i >> log_c)
    m_strict = same & (s_i < t_i)
    m_incl = same & (s_i <= t_i)
    eye_cat = (s_i == t_i).astype(F32)
    ig = _iota((g_rows, g_rows), 0)
    jg = _iota((g_rows, g_rows), 1)
    same_g = (ig >> log_c) == (jg >> log_c)
    l_tri = (same_g & (jg <= ig)).astype(BF16)
    l_all = same_g.astype(BF16)
    eye_s = _iota((MIX_W, MIX_W), 0) == _iota((MIX_W, MIX_W), 1)
    masks = (ones_bd, bd, m_strict, m_incl, eye_cat, l_tri, l_all, eye_s)

    blk_rows = chain * g_rows
    row = _iota((blk_rows, RW_COLS), 0)
    blocks = [p_ref[pi] for pi in range(n_prob)]
    rolled = [pltpu.roll(p, 1, 0) for p in blocks]
    if nb == 1:
        prevs = [jnp.where(row == 0, plast_scr[pi][7:8, :], rolled[pi]) for pi in range(n_prob)]
        for pi in range(n_prob):
            plast_scr[pi] = blocks[pi][blk_rows - 8:blk_rows, :]
    else:
        prevs = [jnp.where((row & (c - 1)) == 0, pe_ref[pi], rolled[pi]) for pi in range(n_prob)]
    chunks = lambda xs: [x[kk * g_rows:(kk + 1) * g_rows] for x in xs for kk in range(chain)]
    s_lists = [[s_scr[pi * nb + s] for s in range(nb)] for pi in range(n_prob)]
    ys, s_new = _rwkv_problems(chunks(blocks), chunks(prevs), s_lists, mu_ref[...], par_ref[...], w2_ref[...],
                               a2_ref[...], g2_ref[...], masks, nb=nb, c=c, chain=chain)
    for pi in range(n_prob):
        for kk in range(chain):
            y_ref[pi, kk * g_rows:(kk + 1) * g_rows, :] = ys[pi * chain + kk]
        for s in range(nb):
            s_scr[pi * nb + s] = s_new[pi][s]
            half = s_new[pi][s][:, :2 * HEAD_W] + s_new[pi][s][:, 2 * HEAD_W:]
            so_ref[pi * nb + s] = half[:, :HEAD_W] + half[:, HEAD_W:]


def _rwkv(p3, pe, s0, mu, par, w2p, a2p, g2p, n_prob, nb, c, chain):
    n_grp, lt, _ = p3.shape
    pe_rows = pe.shape[1]
    n_state = n_prob * nb
    blk_rows = chain * RW_ROWS
    return pl.pallas_call(
        functools.partial(_rwkv_body, n_prob=n_prob, nb=nb, c=c, chain=chain),
        grid=(n_grp // n_prob, lt // blk_rows),
        in_specs=[
            pl.BlockSpec((n_prob, blk_rows, RW_COLS), lambda i, j: (i, j, 0)),
            pl.BlockSpec((n_prob, pe_rows, RW_COLS), lambda i, j: (i, 0, 0)),
            pl.BlockSpec((n_state, MIX_W, HEAD_W), lambda i, j: (i, 0, 0)),
            _const_spec(mu.shape), _const_spec(par.shape),
            _const_spec(w2p.shape), _const_spec(a2p.shape), _const_spec(g2p.shape),
        ],
        out_specs=[
            pl.BlockSpec((n_prob, blk_rows, MIX_W), lambda i, j: (i, j, 0)),
            pl.BlockSpec((n_state, MIX_W, HEAD_W), lambda i, j: (i, 0, 0)),
        ],
        out_shape=[jax.ShapeDtypeStruct((n_grp, lt, MIX_W), F32),
                   jax.ShapeDtypeStruct((n_grp * nb, MIX_W, HEAD_W), F32)],
        scratch_shapes=[pltpu.VMEM((n_state, MIX_W, MIX_W), F32), pltpu.VMEM((n_prob, 8, RW_COLS), F32)],
        compiler_params=_cparams("parallel", "arbitrary"),
        name="rwkv7",
    )(p3, pe, s0, mu, par, w2p, a2p, g2p)


def _merge_body(x_ref, ya_ref, yc_ref, hc_ref, cp_ref, e0_ref, e1_ref, hs_ref, cw_ref, sln_ref, wm_ref, sb_ref,
                wg_ref, bg_ref, wb_ref, wo_ref, ln_ref, o_ref, z_ref, sv_ref, *, tm, seg, tiles_per_seq):
    i = pl.program_id(0)
    x = x_ref[...]
    hc = hc_ref[...]
    gb = hc[:, 0:MIX_W]
    z = hc[:, MIX_W:2 * MIX_W] * hc[:, 2 * MIX_W:3 * MIX_W]
    row = _iota((tm, MIX_W), 0)
    z1 = pltpu.roll(z, 1, 0)
    z2 = pltpu.roll(z, 2, 0)
    if seg >= tm:
        cp = cp_ref[...]
        zp = cp[:, MIX_W:2 * MIX_W] * cp[:, 2 * MIX_W:3 * MIX_W]
        zp = jnp.where(i % tiles_per_seq == 0, jnp.zeros_like(zp), zp)
        e1 = zp[7:8, :]
        e0 = zp[6:7, :]
        pos = row
    else:
        e1 = e1_ref[...]
        e0 = e0_ref[...]
        pos = row & (seg - 1)
    z1 = jnp.where(pos == 0, e1, z1)
    z2 = jnp.where(pos == 0, e0, jnp.where(pos == 1, e1, z2))
    cw = cw_ref[...]
    yb = gb * (z2 * cw[0:1, :] + z1 * cw[1:2, :] + z * cw[2:3, :])
    z_ref[...] = z[tm - z_ref.shape[0]:, :]

    hs = _gelu_tanh(hs_ref[...])
    u = hs[:, 0:MIX_W]
    sln = sln_ref[...]
    sv = _ln_rows(hs[:, MIX_W:], sln[0:1, :], sln[1:2, :])
    sv_ref[...] = sv[tm - sv_ref.shape[0]:, :]
    t_i = _iota((SGU_CHUNK, N_HEADS * SGU_CHUNK), 0)
    s_i = _iota((SGU_CHUNK, N_HEADS * SGU_CHUNK), 1) & (SGU_CHUNK - 1)
    log_seg = int(math.log2(min(seg, SGU_CHUNK)))
    keep = (s_i <= t_i) & ((s_i >> log_seg) == (t_i >> log_seg))
    wm = jnp.where(keep, wm_ref[...], 0.0).astype(BF16)
    gmask = (_iota((N_HEADS * SGU_CHUNK, MIX_W), 0) >> 7) == (_iota((N_HEADS * SGU_CHUNK, MIX_W), 1) >> 6)
    sbias = sb_ref[...]
    svb = sv.astype(BF16)
    yd_parts = []
    for ck in range(tm // SGU_CHUNK):
        v_c = svb[ck * SGU_CHUNK:(ck + 1) * SGU_CHUNK]
        mixed = jnp.dot(wm, _stack_heads(v_c, gmask), preferred_element_type=F32) + sbias
        yd_parts.append(u[ck * SGU_CHUNK:(ck + 1) * SGU_CHUNK] * mixed)
    yd = jnp.concatenate(yd_parts, axis=0)

    halves = [slice(0, tm // 2), slice(tm // 2, tm)]
    xs = [x[r] for r in halves]
    xbs = [v.astype(BF16) for v in xs]
    branches = [[b[r].astype(BF16) for r in halves] for b in (ya_ref[...], yb, yc_ref[...], yd)]
    mix = [jnp.zeros((tm // 2, D_MODEL), F32) for _ in halves]
    for br in range(4):
        cols = slice(br * D_MODEL, (br + 1) * D_MODEL)
        gate = [_sigmoid(jnp.dot(v, wg_ref[:, cols], preferred_element_type=F32) + bg_ref[:, cols]) for v in xbs]
        proj = [jnp.dot(v, wb_ref[br], preferred_element_type=F32) for v in branches[br]]
        mix = _each(lambda m, g, p: m + g * p, mix, gate, proj)
    ln = ln_ref[...]
    mixed = [jnp.dot(m.astype(BF16), wo_ref[...], preferred_element_type=F32) for m in mix]
    for r, v, m in zip(halves, xs, mixed):
        o_ref[r, :] = _ln_rows(ALPHA * v + m, ln[0:1, :], ln[1:2, :])


def _merge(x, ya, yc, h_conv, e0, e1, h_sgu, conv_w, sgu_ln, wm, sgu_bias, wg, bgate, wb, wo, ln1, tm, seg):
    t = x.shape[0]
    prompt = seg >= tm
    tiles_per_seq = max(seg // tm, 1)
    z_rows = 8 if prompt else tm
    n_seq = t // seg if prompt else 0
    if prompt:
        cp_spec = pl.BlockSpec((8, 3 * MIX_W), lambda i: (jnp.maximum(i * (tm // 8) - 1, 0), 0))
        e_spec = _const_spec(e0.shape)
        z_spec = pl.BlockSpec((8, MIX_W), lambda i: (i // tiles_per_seq, 0))
        z_shape = jax.ShapeDtypeStruct((n_seq * 8, MIX_W), F32)
    else:
        cp_spec = pl.BlockSpec((8, 3 * MIX_W), lambda i: (0, 0))
        e_spec = pl.BlockSpec((tm, MIX_W), lambda i: (i, 0))
        z_spec = pl.BlockSpec((tm, MIX_W), lambda i: (i, 0))
        z_shape = jax.ShapeDtypeStruct((t, MIX_W), F32)
    row = lambda w: pl.BlockSpec((tm, w), lambda i: (i, 0))
    return pl.pallas_call(
        functools.partial(_merge_body, tm=tm, seg=seg, tiles_per_seq=tiles_per_seq),
        grid=(t // tm,),
        in_specs=[row(D_MODEL), row(MIX_W), row(MIX_W), row(3 * MIX_W), cp_spec, e_spec, e_spec, row(2 * MIX_W),
                  _const_spec(conv_w.shape), _const_spec(sgu_ln.shape), _const_spec(wm.shape),
                  _const_spec(sgu_bias.shape), _const_spec(wg.shape), _const_spec(bgate.shape),
                  _const_spec(wb.shape), _const_spec(wo.shape), _const_spec(ln1.shape)],
        out_specs=[row(D_MODEL), z_spec, z_spec],
        out_shape=[jax.ShapeDtypeStruct((t, D_MODEL), F32), z_shape, z_shape],
        compiler_params=_cparams("arbitrary"),
        name="merge",
    )(x, ya, yc, h_conv, h_conv, e0, e1, h_sgu, conv_w, sgu_ln, wm, sgu_bias, wg, bgate, wb, wo, ln1)


def _matmul2_body(x_ref, w1_ref, w2_ref, o1_ref, o2_ref):
    xb = x_ref[...].astype(BF16)
    o1_ref[...] = jnp.dot(xb, w1_ref[...], preferred_element_type=F32)
    o2_ref[...] = jnp.dot(xb, w2_ref[...], preferred_element_type=F32)


def _matmul2_layers(x, w1, w2, tm):
    t, kdim = x.shape
    n_layers, _, n = w1.shape
    w_spec = pl.BlockSpec((None, kdim, n), lambda l, i: (l, 0, 0))
    o_spec = pl.BlockSpec((None, tm, n), lambda l, i: (l, i, 0))
    o_shape = jax.ShapeDtypeStruct((n_layers, t, n), F32)
    return pl.pallas_call(
        _matmul2_body,
        grid=(n_layers, t // tm),
        in_specs=[pl.BlockSpec((tm, kdim), lambda l, i: (i, 0)), w_spec, w_spec],
        out_specs=[o_spec, o_spec],
        out_shape=[o_shape, o_shape],
        compiler_params=_cparams("parallel", "parallel"),
        name="matmul2",
    )(x, w1, w2)


def _matmul_body(x_ref, w_ref, o_ref):
    o_ref[...] = jnp.dot(x_ref[...].astype(BF16), w_ref[...], preferred_element_type=F32).astype(o_ref.dtype)


def _matmul(x, w, out_dtype, tm):
    t, kdim = x.shape
    n = w.shape[1]
    return pl.pallas_call(
        _matmul_body,
        grid=(t // tm,),
        in_specs=[pl.BlockSpec((tm, kdim), lambda i: (i, 0)), _const_spec(w.shape)],
        out_specs=pl.BlockSpec((tm, n), lambda i: (i, 0)),
        out_shape=jax.ShapeDtypeStruct((t, n), out_dtype),
        compiler_params=_cparams("parallel"),
        name="matmul",
    )(x, w)


def _softmax_rows(sc):
    m = jnp.max(sc, axis=-1, keepdims=True)
    e = jnp.exp(sc - m)
    return e / jnp.sum(e, axis=-1, keepdims=True)


def _xattn_rows_body(q_ref, k_ref, v_ref, o_ref, *, tq, slots):
    rows = X_HEADS * tq
    n_col = N_MEM * X_HEADS
    own = (_iota((rows, n_col), 1) & (X_HEADS - 1)) == (_iota((rows, n_col), 0) >> int(math.log2(tq)))
    for s in range(slots):
        q = q_ref[s * tq:(s + 1) * tq, :]
        qs = jnp.concatenate([q[:, h * X_HD:(h + 1) * X_HD] for h in range(X_HEADS)], axis=0).astype(BF16)
        k_all = k_ref[s].reshape(n_col, X_HD).astype(BF16)
        v_all = v_ref[s].reshape(n_col, X_HD).astype(BF16)
        sc = lax.dot_general(qs, k_all, (((1,), (1,)), ((), ())), preferred_element_type=F32) * (X_HD ** -0.5)
        pr = _softmax_rows(jnp.where(own, sc, -1e30))
        o = jnp.dot(pr.astype(BF16), v_all, preferred_element_type=F32)
        for h in range(X_HEADS):
            o_ref[s * tq:(s + 1) * tq, h * X_HD:(h + 1) * X_HD] = o[h * tq:(h + 1) * tq].astype(o_ref.dtype)


def _xattn_rows(q, mem_k, mem_v, layer, tq, slots):
    t = q.shape[0]
    rows = tq * slots
    mem_spec = pl.BlockSpec((None, slots, N_MEM, X_HEADS, X_HD), lambda i: (layer, i, 0, 0, 0))
    return pl.pallas_call(
        functools.partial(_xattn_rows_body, tq=tq, slots=slots),
        grid=(t // rows,),
        in_specs=[pl.BlockSpec((rows, D_MODEL), lambda i: (i, 0)), mem_spec, mem_spec],
        out_specs=pl.BlockSpec((rows, D_MODEL), lambda i: (i, 0)),
        out_shape=jax.ShapeDtypeStruct((t, D_MODEL), q.dtype),
        compiler_params=_cparams("parallel"),
        name="xattn_rows",
    )(q, mem_k, mem_v)


def _proj_ln_body(y_ref, w_ref, x_ref, ln_ref, o_ref):
    ln = ln_ref[...]
    acc = jnp.dot(y_ref[...].astype(BF16), w_ref[...], preferred_element_type=F32)
    o_ref[...] = _ln_rows(ALPHA * x_ref[...] + acc, ln[0:1, :], ln[1:2, :])


def _proj_ln(y, w, x, ln, tm):
    t = x.shape[0]
    return pl.pallas_call(
        _proj_ln_body,
        grid=(t // tm,),
        in_specs=[pl.BlockSpec((tm, D_MODEL), lambda i: (i, 0)), _const_spec(w.shape),
                  pl.BlockSpec((tm, D_MODEL), lambda i: (i, 0)), _const_spec(ln.shape)],
        out_specs=pl.BlockSpec((tm, D_MODEL), lambda i: (i, 0)),
        out_shape=jax.ShapeDtypeStruct((t, D_MODEL), F32),
        compiler_params=_cparams("parallel"),
        name="proj_ln",
    )(y, w, x, ln)


def _mlp_rows(x, wu_ref, wd_ref, ln):
    xb = x.astype(BF16)
    acc = jnp.zeros(x.shape, F32)
    for j in range(D_FF // D_MODEL):
        cols = slice(j * D_MODEL, (j + 1) * D_MODEL)
        hid = jnp.maximum(jnp.dot(xb, wu_ref[:, cols], preferred_element_type=F32), 0.0)
        acc = acc + jnp.dot((hid * hid).astype(BF16), wd_ref[cols, :], preferred_element_type=F32)
    return _ln_rows(ALPHA * x + acc, ln[0:1, :], ln[1:2, :])


def _mlp_body(x_ref, wu_ref, wd_ref, ln_ref, o_ref):
    o_ref[...] = _mlp_rows(x_ref[...], wu_ref, wd_ref, ln_ref[...])


def _mlp(x, wu, wd, ln, tm):
    t = x.shape[0]
    return pl.pallas_call(
        _mlp_body,
        grid=(t // tm,),
        in_specs=[pl.BlockSpec((tm, D_MODEL), lambda i: (i, 0)), _const_spec(wu.shape), _const_spec(wd.shape),
                  _const_spec(ln.shape)],
        out_specs=pl.BlockSpec((tm, D_MODEL), lambda i: (i, 0)),
        out_shape=jax.ShapeDtypeStruct((t, D_MODEL), F32),
        compiler_params=_cparams("parallel"),
        name="mlp",
    )(x, wu, wd, ln)


def _layer_params(l, w_in, sb_bias, w_gate, b_gate, w_branch, w_o, conv_w, rw_mu, rw_w0, rw_w2, rw_a0, rw_a2, rw_g2,
                  rw_kk, rw_ka, rw_rk, rw_gn_g, rw_gn_b, sgu_ln_g, sgu_ln_b, sgu_ws, sgu_b, w_mq, w_mo,
                  w_up, w_down, ln1_g, ln1_b, ln2_g, ln2_b, ln3_g, ln3_b):
    wi = w_in[l].astype(BF16)
    off_b, off_c, off_d = 3 * MIX_W, 6 * MIX_W, 6 * MIX_W + RW_COLS
    zpad = lambda w, r0: jnp.zeros((128, MIX_W), F32).at[r0:r0 + w.shape[0]].set(w).astype(BF16)
    par = jnp.zeros((8, MIX_W), F32)
    for i, vec in enumerate((rw_w0[l], rw_a0[l], rw_kk[l], rw_ka[l], rw_rk[l].reshape(MIX_W), rw_gn_g[l], rw_gn_b[l])):
        par = par.at[i].set(vec)
    return dict(
        w_in=(wi[:, :MIX_W], wi[:, MIX_W:2 * MIX_W], wi[:, 2 * MIX_W:off_b], wi[:, off_b:off_c], wi[:, off_c:off_d],
              wi[:, off_d:]),
        sb_bias=sb_bias[l],
        w_gate=w_gate[l].astype(BF16), b_gate=b_gate[l].reshape(1, -1), w_branch=w_branch[l].astype(BF16),
        w_o=w_o[l].astype(BF16), conv_w=jnp.zeros((8, MIX_W), F32).at[:3].set(conv_w[l]),
        rw_mu=rw_mu[l].reshape(1, RW_COLS), rw_par=par,
        rw_w2=zpad(rw_w2[l], 0), rw_a2=zpad(rw_a2[l], 32), rw_g2=zpad(rw_g2[l], 64),
        sgu_ln=jnp.stack([sgu_ln_g[l], sgu_ln_b[l]]), sgu_ws=sgu_ws[l], sgu_b=sgu_b[l],
        w_mq=w_mq[l].astype(BF16), w_mo=w_mo[l].astype(BF16),
        w_up=w_up[l].astype(BF16), w_down=w_down[l].astype(BF16),
        ln1=jnp.stack([ln1_g[l], ln1_b[l]]), ln2=jnp.stack([ln2_g[l], ln2_b[l]]), ln3=jnp.stack([ln3_g[l], ln3_b[l]]),
    )


def _sgu_tables(lp, seg):
    ws = lp['sgu_ws']
    sb = lp['sgu_b']
    if seg < SGU_CHUNK:
        reps = SGU_CHUNK // seg
        ws = jnp.tile(ws[:, :seg, :seg], (1, reps, reps))
        sb = jnp.tile(sb[:, :seg], (1, reps))
    wm = jnp.transpose(ws, (1, 0, 2)).reshape(SGU_CHUNK, N_HEADS * SGU_CHUNK)
    bias = jnp.repeat(sb.T, HEAD_W, axis=1)
    return wm, bias


def _sb_bias_rows(bias, tq, width):
    return jnp.broadcast_to(jnp.repeat(bias, tq)[:, None], (N_HEADS * tq, width)).astype(F32)


def _tail_fused_body(x_ref, wq_ref, k_ref, v_ref, wo_ref, ln2_ref, wu_ref, wd_ref, ln3_ref, o_ref):
    x = x_ref[...]
    q = jnp.dot(x.astype(BF16), wq_ref[...], preferred_element_type=F32).astype(BF16)
    kb = k_ref[...].astype(BF16)
    vb = v_ref[...].astype(BF16)
    ctx = []
    for h in range(X_HEADS):
        cols = slice(h * X_HD, (h + 1) * X_HD)
        sc = lax.dot_general(q[:, cols], kb[:, cols], (((1,), (1,)), ((), ())),
                             preferred_element_type=F32) * (X_HD ** -0.5)
        ctx.append(jnp.dot(_softmax_rows(sc).astype(BF16), vb[:, cols], preferred_element_type=F32).astype(BF16))
    att = jnp.concatenate(ctx, axis=1)
    ln2 = ln2_ref[...]
    x2 = _ln_rows(ALPHA * x + jnp.dot(att, wo_ref[...], preferred_element_type=F32), ln2[0:1, :], ln2[1:2, :])
    o_ref[...] = _mlp_rows(x2, wu_ref, wd_ref, ln3_ref[...])


def _tail_fused(x, lp, mem_k, mem_v, layer, tm):
    t = x.shape[0]
    tiles_per_mem = t // (mem_k.shape[1] // N_MEM) // tm
    mem_spec = pl.BlockSpec((None, N_MEM, D_MODEL), lambda i: (layer, i // tiles_per_mem, 0))
    weights = (lp['w_mq'], lp['w_mo'], lp['ln2'], lp['w_up'], lp['w_down'], lp['ln3'])
    wq, wo, ln2, wu, wd, ln3 = (_const_spec(w.shape) for w in weights)
    return pl.pallas_call(
        _tail_fused_body,
        grid=(t // tm,),
        in_specs=[pl.BlockSpec((tm, D_MODEL), lambda i: (i, 0)), wq, mem_spec, mem_spec, wo, ln2, wu, wd, ln3],
        out_specs=pl.BlockSpec((tm, D_MODEL), lambda i: (i, 0)),
        out_shape=jax.ShapeDtypeStruct((t, D_MODEL), F32),
        compiler_params=_cparams("parallel"),
        name="tail",
    )(x, lp['w_mq'], mem_k, mem_v, lp['w_mo'], lp['ln2'], lp['w_up'], lp['w_down'], lp['ln3'])


def _tail(x, lp, attend, q_dtype, tm):
    qm = _matmul(x, lp['w_mq'], q_dtype, tm)
    x = _proj_ln(attend(qm), lp['w_mo'], x, lp['ln2'], tm)
    return _mlp(x, lp['w_up'], lp['w_down'], lp['ln3'], tm)


def _layer_prompt(x, lp, mem_k, mem_v, layer, n, seq):
    t = n * seq
    q, k, v, h_conv, h_rw, h_sgu = _inproj(x, lp['w_in'], ROW_TILE)
    tq = SB_BLOCK
    ya = _sb_prompt(q, k, v, _sb_bias_rows(lp['sb_bias'], tq, tq), n, seq, tq)
    pe = jnp.zeros((n, 8, RW_COLS), F32)
    s0 = jnp.zeros((n, MIX_W, HEAD_W), F32)
    yc, s_fin = _rwkv(h_rw.reshape(n, seq, RW_COLS), pe, s0, lp['rw_mu'], lp['rw_par'], lp['rw_w2'], lp['rw_a2'],
                      lp['rw_g2'], n_prob=n, nb=1, c=RW_ROWS, chain=2)
    wm, sgu_bias = _sgu_tables(lp, SGU_CHUNK)
    zero_e = jnp.zeros((8, MIX_W), F32)
    x1, z_tail, _ = _merge(x, ya, yc.reshape(t, MIX_W), h_conv, zero_e, zero_e, h_sgu, lp['conv_w'], lp['sgu_ln'],
                           wm, sgu_bias, lp['w_gate'], lp['b_gate'], lp['w_branch'], lp['w_o'], lp['ln1'],
                           tm=ROW_TILE, seg=seq)
    x3 = _tail_fused(x1, lp, mem_k, mem_v, layer, ROW_TILE)
    k_new = k.reshape(n, seq, N_HEADS, HEAD_W)
    v_new = v.reshape(n, seq, N_HEADS, HEAD_W)
    conv_new = z_tail.reshape(n, 8, MIX_W)[:, 6:8]
    shift_new = h_rw.reshape(n, seq, RW_COLS)[:, -1]
    return x3, k_new, v_new, conv_new, shift_new, s_fin.reshape(n, N_HEADS, HEAD_W, HEAD_W)


def _layer_sample(x, lp, layer, mem_k, mem_v, cache_kt, cache_vt, page_table, state_conv, state_shift, state_wkv,
                  nb, seq):
    t = nb * seq
    q, k_new, v_new, h_conv, h_rw, h_sgu = _inproj(x, lp['w_in'], ROW_TILE)
    ya = _sb_sample(q, k_new, v_new, cache_kt, cache_vt, page_table, _sb_bias_rows(lp['sb_bias'], seq, PAGE_SIZE),
                    layer, nb, seq)
    per = RW_ROWS // seq
    pe = jnp.repeat(state_shift, seq, axis=0).reshape(nb // per, RW_ROWS, RW_COLS)
    yc, s_fin = _rwkv(h_rw.reshape(nb // per, RW_ROWS, RW_COLS), pe, state_wkv.reshape(nb, MIX_W, HEAD_W), lp['rw_mu'],
                      lp['rw_par'], lp['rw_w2'], lp['rw_a2'], lp['rw_g2'], n_prob=2, nb=per, c=seq, chain=1)
    wm, sgu_bias = _sgu_tables(lp, seq)
    e0 = jnp.repeat(state_conv[:, 0], seq, axis=0)
    e1 = jnp.repeat(state_conv[:, 1], seq, axis=0)
    x1, z_all, sgu_v = _merge(x, ya, yc.reshape(t, MIX_W), h_conv, e0, e1, h_sgu, lp['conv_w'], lp['sgu_ln'], wm,
                              sgu_bias, lp['w_gate'], lp['b_gate'], lp['w_branch'], lp['w_o'], lp['ln1'],
                              tm=ROW_TILE, seg=seq)
    x3 = _tail(x1, lp, lambda qm: _xattn_rows(qm, mem_k, mem_v, layer, seq, XATTN_SLOTS), F32, ROW_TILE)
    conv_new = z_all.reshape(nb, seq, MIX_W)[:, seq - 2:]
    shift_new = h_rw.reshape(nb, seq, RW_COLS)[:, -1]
    return (x3, k_new.reshape(nb, seq, N_HEADS, HEAD_W), v_new.reshape(nb, seq, N_HEADS, HEAD_W), conv_new,
            shift_new, s_fin.reshape(nb, N_HEADS, HEAD_W, HEAD_W), sgu_v.reshape(nb, seq, MIX_W))


def kernel(x_prompt, x_sample, mem_prompt, cache_k, cache_v, page_table, cache_mem_k, cache_mem_v, state_conv,
           state_wkv, state_shift, w_in, sb_bias, w_gate, b_gate, w_branch, w_o, conv_w, rw_mu, rw_w0, rw_w2, rw_a0,
           rw_a2, rw_g2, rw_kk, rw_ka, rw_rk, rw_gn_g, rw_gn_b, sgu_ln_g, sgu_ln_b, sgu_ws, sgu_b, w_mq, w_mk, w_mv,
           w_mo, w_up, w_down, ln1_g, ln1_b, ln2_g, ln2_b, ln3_g, ln3_b):
    n_p, seq_p, _ = x_prompt.shape
    n_s, seq_s, _ = x_sample.shape
    n_phys = cache_k.shape[1]
    xp = x_prompt.reshape(n_p * seq_p, D_MODEL)
    xs = x_sample.reshape(n_s * seq_s, D_MODEL)
    mem2d = mem_prompt.reshape(n_p * N_MEM, D_MODEL)
    cache_kt = jnp.transpose(cache_k, (0, 1, 3, 4, 2)).reshape(DEPTH, n_phys, MIX_W, PAGE_SIZE)
    cache_vt = jnp.transpose(cache_v, (0, 1, 3, 4, 2)).reshape(DEPTH, n_phys, MIX_W, PAGE_SIZE)
    mk, mv = _matmul2_layers(mem2d, w_mk.astype(BF16), w_mv.astype(BF16), ROW_TILE)
    outs = [[] for _ in range(11)]
    for l in range(DEPTH):
        lp = _layer_params(l, w_in, sb_bias, w_gate, b_gate, w_branch, w_o, conv_w, rw_mu, rw_w0, rw_w2, rw_a0,
                           rw_a2, rw_g2, rw_kk, rw_ka, rw_rk, rw_gn_g, rw_gn_b, sgu_ln_g, sgu_ln_b, sgu_ws, sgu_b,
                           w_mq, w_mo, w_up, w_down, ln1_g, ln1_b, ln2_g, ln2_b, ln3_g, ln3_b)
        xp, pk, pv, pc, psh, pst = _layer_prompt(xp, lp, mk, mv, l, n_p, seq_p)
        xs, sk, sv, sc, ssh, sst, scv = _layer_sample(
            xs, lp, l, cache_mem_k, cache_mem_v, cache_kt, cache_vt, page_table,
            state_conv[l], state_shift[l], state_wkv[l], n_s, seq_s)
        for lst, val in zip(outs, (pk, pv, pc, pst, psh, sk, sv, sc, sst, ssh, scv)):
            lst.append(val)
    p_k, p_v, p_conv, p_wkv, p_shift, s_k, s_v, s_conv, s_wkv, s_shift, s_chunk = (jnp.stack(o) for o in outs)
    mem_shape = (DEPTH, n_p, N_MEM, X_HEADS, X_HD)
    return (xp.reshape(n_p, seq_p, D_MODEL), xs.reshape(n_s, seq_s, D_MODEL), p_k, p_v, mk.reshape(mem_shape),
            mv.reshape(mem_shape), p_conv, p_wkv, p_shift, s_k, s_v, s_conv, s_wkv, s_shift, s_chunk)
```

```python
import functools
import math

import jax
import jax.numpy as jnp
from jax import lax
from jax.experimental import pallas as pl
from jax.experimental.pallas import tpu as pltpu

F32 = jnp.float32
BF16 = jnp.bfloat16

D_MODEL = 1024
DEPTH = 2
MIX_W = 256
HEAD_W = 64
N_HEADS = MIX_W // HEAD_W
RW_COLS = 896
PAGE_SIZE = 128
N_MEM = 256
X_HEADS = 4
X_HD = D_MODEL // X_HEADS
D_FF = 4 * D_MODEL
ALPHA = (2 * DEPTH) ** 0.25
LN_EPS = 1e-5
RW_GN_EPS = 64e-5
SGU_CHUNK = 128
RW_ROWS = 64
ROW_TILE = 1024
SB_BLOCK = 256
XATTN_SLOTS = 8
VMEM_LIMIT = 56 * 1024 * 1024


def _cparams(*sem):
    return pltpu.CompilerParams(dimension_semantics=sem, vmem_limit_bytes=VMEM_LIMIT)


def _const_spec(shape):
    nd = len(shape)
    return pl.BlockSpec(shape, lambda *_: (0,) * nd, pipeline_mode=pl.Buffered(1))


def _bdot(a, b):
    return jnp.dot(a.astype(BF16), b.astype(BF16), preferred_element_type=F32)


def _bdot_nt(a, b):
    return lax.dot_general(a.astype(BF16), b.astype(BF16), (((1,), (1,)), ((), ())),
                           preferred_element_type=F32)


def _bdot_tn(a, b):
    return lax.dot_general(a.astype(BF16), b.astype(BF16), (((0,), (0,)), ((), ())),
                           preferred_element_type=F32)


def _split2(x):
    hi = x.astype(BF16)
    lo = (x - hi.astype(F32)).astype(BF16)
    return hi, lo


def _split3(x):
    hi = x.astype(BF16)
    r1 = x - hi.astype(F32)
    mid = r1.astype(BF16)
    lo = (r1 - mid.astype(F32)).astype(BF16)
    return hi, mid, lo


def _sel_dot_l(sel_b, x):
    return sum(jnp.dot(sel_b, part, preferred_element_type=F32) for part in _split3(x))


def _sel_dot_r(x, sel_b):
    return sum(jnp.dot(part, sel_b, preferred_element_type=F32) for part in _split2(x))


def _dot3(a, b):
    ah, al = _split2(a)
    bh, bl = _split2(b)
    d = lambda x, y: jnp.dot(x, y, preferred_element_type=F32)
    return d(ah, bh) + d(ah, bl) + d(al, bh)


def _sigmoid(x):
    return 1.0 / (1.0 + jnp.exp(-x))


def _softplus(x):
    return jnp.maximum(x, 0.0) + jnp.log(1.0 + jnp.exp2(jnp.abs(x) * (-1.0 / math.log(2.0))))


def _gelu_tanh(x):
    return 0.5 * x * (1.0 + jnp.tanh(0.7978845608028654 * (x + 0.044715 * (x * x * x))))


def _ln_rows(x, g, b, eps=LN_EPS):
    mu = jnp.mean(x, axis=-1, keepdims=True)
    xc = x - mu
    var = jnp.mean(xc * xc, axis=-1, keepdims=True)
    return xc * lax.rsqrt(var + eps) * g + b


def _iota(shape, dim):
    return lax.broadcasted_iota(jnp.int32, shape, dim)


def _head_block_mask(rows, row_shift):
    return (_iota((rows, MIX_W), 0) >> row_shift) == (_iota((rows, MIX_W), 1) >> 6)


def _stack_heads(x, mask):
    t = jnp.concatenate([x] * N_HEADS, axis=0)
    return jnp.where(mask, t, jnp.zeros_like(t))


def _unstack_heads(x, rows):
    return x[0:rows] + x[rows:2 * rows] + x[2 * rows:3 * rows] + x[3 * rows:4 * rows]


def _inproj_body(x_ref, *refs):
    n = len(refs) // 2
    xb = x_ref[...].astype(BF16)
    for w_ref, o_ref in zip(refs[:n], refs[n:]):
        o_ref[...] = jnp.dot(xb, w_ref[...], preferred_element_type=F32)


def _inproj(x, weights, tm):
    t = x.shape[0]
    widths = [w.shape[1] for w in weights]
    return pl.pallas_call(
        _inproj_body,
        grid=(t // tm,),
        in_specs=[pl.BlockSpec((tm, D_MODEL), lambda i: (i, 0))] + [_const_spec(w.shape) for w in weights],
        out_specs=[pl.BlockSpec((tm, w), lambda i: (i, 0)) for w in widths],
        out_shape=[jax.ShapeDtypeStruct((t, w), F32) for w in widths],
        compiler_params=_cparams("parallel"),
        name="inproj",
    )(x, *weights)


def _suffix_ones(tk):
    return (_iota((tk, tk), 0) >= _iota((tk, tk), 1)).astype(BF16)


def _sb_weights(z, carry, suffix, mask):
    tk = z.shape[1]
    stop = _softplus(z)
    if mask is not None:
        stop = jnp.where(mask, stop, 0.0)
    cs = jnp.dot(stop.astype(BF16), suffix, preferred_element_type=F32)
    later = carry if tk == 128 else jnp.concatenate([carry] * (tk // 128), axis=1)
    w = jnp.exp(z - cs - later)
    if mask is not None:
        w = jnp.where(mask, w, 0.0)
    total = jnp.sum(stop, axis=1, keepdims=True)
    return w, carry + jnp.broadcast_to(total, carry.shape)


def _sb_prompt_body(q_ref, k_ref, v_ref, bias_ref, o_ref, kb, vb, carry_ref, acc_ref, *, tq):
    qi = pl.program_id(1)
    rows = N_HEADS * tq

    @pl.when(qi == 0)
    def _():
        kb[...] = k_ref[...].astype(BF16)
        vb[...] = v_ref[...].astype(BF16)

    hmask = _head_block_mask(rows, int(math.log2(tq)))
    qs = _stack_heads((q_ref[...] * (HEAD_W ** -0.5)).astype(BF16), hmask)
    suffix = _suffix_ones(tq)

    def block(start, carry, acc, mask):
        z = lax.dot_general(qs, kb[pl.ds(start, tq), :], (((1,), (1,)), ((), ())),
                            preferred_element_type=F32) + bias_ref[...]
        w, carry = _sb_weights(z, carry, suffix, mask)
        acc = acc + jnp.dot(w.astype(BF16), vb[pl.ds(start, tq), :], preferred_element_type=F32)
        return carry, acc

    def block_pair(start):
        k2 = kb[pl.ds(start, 2 * tq), :]
        v2 = vb[pl.ds(start, 2 * tq), :]
        hrows = lambda h: slice(h * tq, (h + 1) * tq)
        logits = lambda h: lax.dot_general(qs[hrows(h)], k2, (((1,), (1,)), ((), ())),
                                           preferred_element_type=F32)
        z2 = [logits(0), logits(1)]
        for h in range(N_HEADS):
            bias = bias_ref[hrows(h), :]
            z = jnp.concatenate([z2[h][:, tq:] + bias, z2[h][:, :tq] + bias], axis=0)
            stop = _softplus(z)
            cs = jnp.dot(stop.astype(BF16), suffix, preferred_element_type=F32)
            if h + 2 < N_HEADS:
                z2.append(logits(h + 2))
            total = jnp.broadcast_to(jnp.sum(stop, axis=1, keepdims=True), (2 * tq, 128))
            carry = carry_ref[hrows(h), :]
            carry2 = jnp.concatenate([carry, carry + total[:tq]], axis=0)
            later = jnp.concatenate([carry2] * (tq // 128), axis=1)
            w = jnp.exp(z - cs - later).astype(BF16)
            w2 = jnp.concatenate([w[tq:], w[:tq]], axis=1)
            acc_ref[hrows(h), :] += jnp.dot(w2, v2, preferred_element_type=F32)
            carry_ref[hrows(h), :] = carry2[tq:] + total[tq:]

    t_idx = _iota((rows, tq), 0) & (tq - 1)
    causal = _iota((rows, tq), 1) < t_idx
    carry, acc = block(pl.multiple_of(qi * tq, tq), jnp.zeros((rows, 128), F32),
                       jnp.zeros((rows, MIX_W), F32), causal)
    carry_ref[...] = carry
    acc_ref[...] = acc

    @pl.when((qi & 1) == 1)
    def _():
        c, a = block(pl.multiple_of((qi - 1) * tq, tq), carry_ref[...], acc_ref[...], None)
        carry_ref[...] = c
        acc_ref[...] = a

    n_pairs = qi >> 1

    def step(jj, _):
        block_pair(pl.multiple_of((n_pairs - 1 - jj) * (2 * tq), 2 * tq))
        return 0

    lax.fori_loop(0, n_pairs, step, 0)
    acc = jnp.where(hmask, acc_ref[...], 0.0)
    o_ref[...] = _unstack_heads(acc, tq)


def _sb_prompt(q, k, v, bias_rows, n, seq, tq):
    nq = seq // tq
    rows = N_HEADS * tq
    return pl.pallas_call(
        functools.partial(_sb_prompt_body, tq=tq),
        grid=(n, nq),
        in_specs=[
            pl.BlockSpec((tq, MIX_W), lambda b, i: (b * nq + i, 0)),
            pl.BlockSpec((seq, MIX_W), lambda b, i: (b, 0)),
            pl.BlockSpec((seq, MIX_W), lambda b, i: (b, 0)),
            _const_spec((rows, tq)),
        ],
        out_specs=pl.BlockSpec((tq, MIX_W), lambda b, i: (b * nq + i, 0)),
        out_shape=jax.ShapeDtypeStruct((n * seq, MIX_W), F32),
        scratch_shapes=[pltpu.VMEM((seq, MIX_W), BF16), pltpu.VMEM((seq, MIX_W), BF16),
                        pltpu.VMEM((rows, 128), F32), pltpu.VMEM((rows, MIX_W), F32)],
        compiler_params=_cparams("parallel", "arbitrary"),
        name="sb_prompt",
    )(q, k, v, bias_rows)


SB_SEQ_SLOTS = 4


def _sb_sample_body(pt_ref, q_ref, kn_ref, vn_ref, ck_ref, cv_ref, bias_ref, o_ref, kbuf, vbuf, sem,
                    *, layer, tq, n_pages):
    b = pl.program_id(0)
    rows = N_HEADS * tq
    n_blk = n_pages + 1

    def seq_copies(s):
        slot = lax.rem(s, SB_SEQ_SLOTS)
        out = []
        for j in range(n_pages):
            page = pt_ref[s, n_pages - 1 - j]
            out.append(pltpu.make_async_copy(ck_ref.at[layer, page], kbuf.at[slot, j], sem.at[slot]))
            out.append(pltpu.make_async_copy(cv_ref.at[layer, page], vbuf.at[slot, j], sem.at[slot]))
        return slot, out

    @pl.when(b == 0)
    def _():
        for s in range(SB_SEQ_SLOTS - 1):
            for cp in seq_copies(jnp.int32(s))[1]:
                cp.start()

    @pl.when(b + (SB_SEQ_SLOTS - 1) < pl.num_programs(0))
    def _():
        for cp in seq_copies(b + (SB_SEQ_SLOTS - 1))[1]:
            cp.start()

    slot, copies = seq_copies(b)
    for cp in copies:
        cp.wait()

    hmask = _head_block_mask(rows, int(math.log2(tq)))
    qs = _stack_heads(q_ref[...] * (HEAD_W ** -0.5), hmask).astype(BF16)
    bias = bias_ref[...]

    pad = jnp.zeros((PAGE_SIZE - tq, MIX_W), F32)
    k_new = jnp.concatenate([kn_ref[...], pad], axis=0).astype(BF16)
    v_new = jnp.concatenate([vn_ref[...], pad], axis=0).astype(BF16)
    z_new = lax.dot_general(qs, k_new, (((1,), (1,)), ((), ())), preferred_element_type=F32)
    k_pages = jnp.concatenate([kbuf[slot, j].astype(BF16) for j in range(n_pages)], axis=1)
    z_pages = jnp.dot(qs, k_pages, preferred_element_type=F32)
    z = jnp.concatenate([z_new] + [z_pages[:, j * PAGE_SIZE:(j + 1) * PAGE_SIZE] for j in range(n_pages)], axis=0)
    z = z + jnp.concatenate([bias] * n_blk, axis=0)

    r_i = _iota((n_blk * rows, PAGE_SIZE), 0)
    valid = (r_i >= rows) | (_iota((n_blk * rows, PAGE_SIZE), 1) < (r_i & (tq - 1)))
    stop = jnp.where(valid, _softplus(z), 0.0)
    cs = jnp.dot(stop.astype(BF16), _suffix_ones(PAGE_SIZE), preferred_element_type=F32)
    total = jnp.broadcast_to(jnp.sum(stop, axis=1, keepdims=True), (n_blk * rows, PAGE_SIZE))
    carry = [jnp.zeros((rows, PAGE_SIZE), F32)]
    for i in range(n_blk - 1):
        carry.append(carry[-1] + total[i * rows:(i + 1) * rows])
    w = jnp.where(valid, jnp.exp(z - cs - jnp.concatenate(carry, axis=0)), 0.0).astype(BF16)

    acc = jnp.dot(w[:rows], v_new, preferred_element_type=F32)
    w_pages = jnp.concatenate([w[(j + 1) * rows:(j + 2) * rows] for j in range(n_pages)], axis=1)
    v_pages = jnp.concatenate([vbuf[slot, j].astype(BF16) for j in range(n_pages)], axis=1)
    acc = acc + lax.dot_general(w_pages, v_pages, (((1,), (1,)), ((), ())), preferred_element_type=F32)
    o_ref[...] = _unstack_heads(jnp.where(hmask, acc, 0.0), tq)


def _sb_sample(q, k, v, cache_kt, cache_vt, page_table, bias_rows, layer, nb, tq):
    n_pages = page_table.shape[1]
    rows = N_HEADS * tq
    buf = (SB_SEQ_SLOTS, n_pages, MIX_W, PAGE_SIZE)
    grid_spec = pltpu.PrefetchScalarGridSpec(
        num_scalar_prefetch=1,
        grid=(nb,),
        in_specs=[
            pl.BlockSpec((tq, MIX_W), lambda b, pt: (b, 0)),
            pl.BlockSpec((tq, MIX_W), lambda b, pt: (b, 0)),
            pl.BlockSpec((tq, MIX_W), lambda b, pt: (b, 0)),
            pl.BlockSpec(memory_space=pl.ANY),
            pl.BlockSpec(memory_space=pl.ANY),
            pl.BlockSpec((rows, PAGE_SIZE), lambda b, pt: (0, 0)),
        ],
        out_specs=pl.BlockSpec((tq, MIX_W), lambda b, pt: (b, 0)),
        scratch_shapes=[pltpu.VMEM(buf, F32), pltpu.VMEM(buf, F32), pltpu.SemaphoreType.DMA((SB_SEQ_SLOTS,))],
    )
    return pl.pallas_call(
        functools.partial(_sb_sample_body, layer=layer, tq=tq, n_pages=n_pages),
        grid_spec=grid_spec,
        out_shape=jax.ShapeDtypeStruct((nb * tq, MIX_W), F32),
        compiler_params=_cparams("arbitrary"),
        name="sb_sample",
    )(page_table, q, k, v, cache_kt, cache_vt, bias_rows)


def _each(fn, *lists):
    return [fn(*args) for args in zip(*lists)]


def _rwkv_problems(ps, p_prevs, s_lists, mu, par, w2p, a2p, g2p, masks, *, nb, c, chain):
    ones_bd, bd, m_strict, m_incl, eye_cat, l_tri, l_all, eye_s = masks
    w0, a0, kk_s, ka_s, rk, gn_g, gn_b = (par[i:i + 1, :] for i in range(7))
    mm = lambda x, y: jnp.dot(x.astype(BF16), y.astype(BF16), preferred_element_type=F32)
    expand = lambda x: _stack_heads(x.astype(BF16), bd)
    left = lambda x: x[:, :MIX_W]
    right = lambda x: x[:, MIX_W:]

    n_p = len(ps)

    def rows_batched(fn, xs):
        out = fn(xs[0] if n_p == 1 else jnp.concatenate(xs, axis=0))
        return [out[i * RW_ROWS:(i + 1) * RW_ROWS] for i in range(n_p)]

    head_sums = lambda xs: rows_batched(lambda x: _sel_dot_r(x, ones_bd), xs)

    xs = _each(lambda p, pp: p + (pp - p) * mu, ps, p_prevs)
    r = [x[:, 0:256] for x in xs]
    k = [x[:, 256:512] for x in xs]
    v = [x[:, 512:768] for x in xs]
    lora = [x[:, 768:896] for x in xs]
    u = rows_batched(lambda x: w0 + mm(jnp.tanh(x), w2p), lora)
    logw = [(-math.exp(-0.5)) * _sigmoid(x) for x in u]
    asig = rows_batched(lambda x: _sigmoid(a0 + mm(x, a2p)), lora)
    gate = rows_batched(lambda x: mm(_sigmoid(x), g2p), lora)
    kk = [x * kk_s for x in k]
    ssq = head_sums([x * x for x in kk])
    kk = _each(lambda x, q: x / jnp.maximum(jnp.sqrt(q), 1e-12), kk, ssq)
    k_eff = _each(lambda x, s: x * (1.0 + (s - 1.0) * ka_s), k, asig)
    rk_sum = head_sums(_each(lambda x, y: x * y * rk, r, k_eff))
    bonus = _each(lambda x, y: x * y, rk_sum, v)
    b = _each(lambda x, s: x * s, kk, asig)

    sums = _sel_dot_l(jnp.concatenate([l_tri, l_all], axis=0), jnp.concatenate(logw, axis=1))
    cum = [sums[:RW_ROWS, i * MIX_W:(i + 1) * MIX_W] for i in range(n_p)]
    cum_all = [sums[RW_ROWS:, i * MIX_W:(i + 1) * MIX_W] for i in range(n_p)]
    rt = _each(lambda x, q: (x * jnp.exp(q)).astype(BF16), r, cum)
    at = _each(lambda x, q, lw: (-x * jnp.exp(q - lw)).astype(BF16), kk, cum, logw)
    ginv = [jnp.exp(-q) for q in cum]
    bt = _each(lambda x, g: x * g, b, ginv)
    kt = _each(lambda x, g: x * g, k_eff, ginv)
    e_tail = _each(lambda qa, q: jnp.exp(qa - q), cum_all, cum)
    bg = _each(lambda x, e: x * e, b, e_tail)
    kg = _each(lambda x, e: x * e, k_eff, e_tail)
    g_end = [jnp.exp(q) for q in cum_all]

    lhs = _each(lambda x, y: jnp.concatenate([x, y], axis=0), at, rt)
    rhs = _each(lambda x, y: jnp.concatenate([expand(x), expand(y)], axis=0), bt, kt)
    a_cat = _each(lambda x, y: lax.dot_general(x, y, (((1,), (1,)), ((), ())), preferred_element_type=F32),
                  lhs, rhs)
    a_ab = [jnp.where(m_strict, left(x[:RW_ROWS]), 0.0) for x in a_cat]
    a_ak = [jnp.where(m_strict, right(x[:RW_ROWS]), 0.0) for x in a_cat]
    a_rb = [jnp.where(m_incl, left(x[RW_ROWS:]), 0.0) for x in a_cat]
    a_rk = [jnp.where(m_incl, right(x[RW_ROWS:]), 0.0) for x in a_cat]

    inv = [eye_cat + x for x in a_ab]
    apow = a_ab
    apow_bd = [expand(x) for x in apow]
    span = 1
    while 2 * span < c:
        apow = _each(mm, apow, apow_bd)
        apow_bd = [expand(x) for x in apow]
        inv = _each(lambda x, y: x + mm(x, y), inv, apow_bd)
        span *= 2

    sv = [expand(x) for x in v]
    w1 = _each(mm, a_ak, sv)
    uu = _each(lambda i, x, y: mm(i, jnp.concatenate([expand(x), expand(y)], axis=1)), inv, at, w1)
    ua = [left(x) for x in uu]
    uv = [right(x) for x in uu]
    qy = _each(lambda m, x, y: mm(m, jnp.concatenate([expand(x), expand(y)], axis=1)), a_rb, ua, uv)
    y0 = _each(lambda q, m, x: right(q) + mm(m, x), qy, a_rk, sv)
    qe = _each(lambda x, q: x.astype(F32) + left(q), rt, qy)

    row_seq = _iota((RW_ROWS, MIX_W), 0) >> int(math.log2(c))
    m_c, n_c = [], []
    for s in range(nb):
        own = lambda x: x if nb == 1 else jnp.where(row_seq == s, x, 0.0)
        bg_s = [own(x) for x in bg]
        kg_s = [own(x) for x in kg]
        m_c.append(_each(lambda x, y, g: jnp.where(bd, _bdot_tn(x, y), 0.0)
                         + jnp.where(eye_s, g[s * c:s * c + 1, :], 0.0), ua, bg_s, g_end))
        n_c.append(_each(lambda x, y, z, w: jnp.where(bd, _bdot_tn(x, y) + _bdot_tn(z, w), 0.0),
                         uv, bg_s, v, kg_s))

    states = [list(sl) for sl in s_lists]
    y_parts = [[] for _ in ps]
    for kk in range(chain):
        idx = [g * chain + kk for g in range(len(s_lists))]
        pick = lambda xs: [xs[i] for i in idx]
        for s in range(nb):
            rows = slice(s * c, (s + 1) * c)
            cur = [st[s] for st in states]
            y_s = _each(lambda q, st, y: _bdot_nt(q[rows], st) + y[rows], pick(qe), cur, pick(y0))
            nxt = _each(lambda st, m, n: _dot3(st, m) + n, cur, pick(m_c[s]), pick(n_c[s]))
            for gi, i in enumerate(idx):
                y_parts[i].append(y_s[gi])
                states[gi][s] = nxt[gi]
    s_new = states
    y = [parts[0] if nb == 1 else jnp.concatenate(parts, axis=0) for parts in y_parts]

    mean = [x * (1.0 / HEAD_W) for x in head_sums(y)]
    d = _each(lambda x, m: x - m, y, mean)
    var = [x * (1.0 / HEAD_W) for x in head_sums([x * x for x in d])]
    out = _each(lambda x, q, bo, g: (x * lax.rsqrt(q + RW_GN_EPS) * gn_g + gn_b + bo) * g, d, var, bonus, gate)
    return out, s_new


def _rwkv_body(p_ref, pe_ref, s0_ref, mu_ref, par_ref, w2_ref, a2_ref, g2_ref, y_ref, so_ref, s_scr, plast_scr,
               *, n_prob, nb, c, chain):
    ci = pl.program_id(1)
    g_rows = RW_ROWS
    r4 = N_HEADS * g_rows
    log_c = int(math.log2(c))

    bd = _head_block_mask(r4, 6)

    @pl.when(ci == 0)
    def _():
        for idx in range(n_prob * nb):
            s_scr[idx] = jnp.where(bd, jnp.concatenate([s0_ref[idx]] * N_HEADS, axis=1), 0.0)
        if nb == 1:
            plast_scr[...] = pe_ref[...]

    ones_bd = bd.astype(BF16)
    t_i = _iota((g_rows, r4), 0)
    s_i = _iota((g_rows, r4), 1) & (g_rows - 1)
    same = (t_i >> log_c) == (s_i >> log_c)
    m_strict = same & (s_i < t_i)
    m_incl = same & (s_i <= t_i)
    eye_cat = (s_i == t_i).astype(F32)
    ig = _iota((g_rows, g_rows), 0)
    jg = _iota((g_rows, g_rows), 1)
    same_g = (ig >> log_c) == (jg >> log_c)
    l_tri = (same_g & (jg <= ig)).astype(BF16)
    l_all = same_g.astype(BF16)
    eye_s = _iota((MIX_W, MIX_W), 0) == _iota((MIX_W, MIX_W), 1)
    masks = (ones_bd, bd, m_strict, m_incl, eye_cat, l_tri, l_all, eye_s)

    blk_rows = chain * g_rows
    row = _iota((blk_rows, RW_COLS), 0)
    blocks = [p_ref[pi] for pi in range(n_prob)]
    rolled = [pltpu.roll(p, 1, 0) for p in blocks]
    if nb == 1:
        prevs = [jnp.where(row == 0, plast_scr[pi][7:8, :], rolled[pi]) for pi in range(n_prob)]
        for pi in range(n_prob):
            plast_scr[pi] = blocks[pi][blk_rows - 8:blk_rows, :]
    else:
        prevs = [jnp.where((row & (c - 1)) == 0, pe_ref[pi], rolled[pi]) for pi in range(n_prob)]
    chunks = lambda xs: [x[kk * g_rows:(kk + 1) * g_rows] for x in xs for kk in range(chain)]
    s_lists = [[s_scr[pi * nb + s] for s in range(nb)] for pi in range(n_prob)]
    ys, s_new = _rwkv_problems(chunks(blocks), chunks(prevs), s_lists, mu_ref[...], par_ref[...], w2_ref[...],
                               a2_ref[...], g2_ref[...], masks, nb=nb, c=c, chain=chain)
    for pi in range(n_prob):
        for kk in range(chain):
            y_ref[pi, kk * g_rows:(kk + 1) * g_rows, :] = ys[pi * chain + kk]
        for s in range(nb):
            s_scr[pi * nb + s] = s_new[pi][s]
            half = s_new[pi][s][:, :2 * HEAD_W] + s_new[pi][s][:, 2 * HEAD_W:]
            so_ref[pi * nb + s] = half[:, :HEAD_W] + half[:, HEAD_W:]


def _rwkv(p3, pe, s0, mu, par, w2p, a2p, g2p, n_prob, nb, c, chain):
    n_grp, lt, _ = p3.shape
    pe_rows = pe.shape[1]
    n_state = n_prob * nb
    blk_rows = chain * RW_ROWS
    return pl.pallas_call(
        functools.partial(_rwkv_body, n_prob=n_prob, nb=nb, c=c, chain=chain),
        grid=(n_grp // n_prob, lt // blk_rows),
        in_specs=[
            pl.BlockSpec((n_prob, blk_rows, RW_COLS), lambda i, j: (i, j, 0)),
            pl.BlockSpec((n_prob, pe_rows, RW_COLS), lambda i, j: (i, 0, 0)),
            pl.BlockSpec((n_state, MIX_W, HEAD_W), lambda i, j: (i, 0, 0)),
            _const_spec(mu.shape), _const_spec(par.shape),
            _const_spec(w2p.shape), _const_spec(a2p.shape), _const_spec(g2p.shape),
        ],
        out_specs=[
            pl.BlockSpec((n_prob, blk_rows, MIX_W), lambda i, j: (i, j, 0)),
            pl.BlockSpec((n_state, MIX_W, HEAD_W), lambda i, j: (i, 0, 0)),
        ],
        out_shape=[jax.ShapeDtypeStruct((n_grp, lt, MIX_W), F32),
                   jax.ShapeDtypeStruct((n_grp * nb, MIX_W, HEAD_W), F32)],
        scratch_shapes=[pltpu.VMEM((n_state, MIX_W, MIX_W), F32), pltpu.VMEM((n_prob, 8, RW_COLS), F32)],
        compiler_params=_cparams("parallel", "arbitrary"),
        name="rwkv7",
    )(p3, pe, s0, mu, par, w2p, a2p, g2p)


def _merge_body(x_ref, ya_ref, yc_ref, hc_ref, cp_ref, e0_ref, e1_ref, hs_ref, cw_ref, sln_ref, wm_ref, sb_ref,
                wg_ref, bg_ref, wb_ref, wo_ref, ln_ref, o_ref, z_ref, sv_ref, *, tm, seg, tiles_per_seq):
    i = pl.program_id(0)
    x = x_ref[...]
    hc = hc_ref[...]
    gb = hc[:, 0:MIX_W]
    z = hc[:, MIX_W:2 * MIX_W] * hc[:, 2 * MIX_W:3 * MIX_W]
    row = _iota((tm, MIX_W), 0)
    z1 = pltpu.roll(z, 1, 0)
    z2 = pltpu.roll(z, 2, 0)
    if seg >= tm:
        cp = cp_ref[...]
        zp = cp[:, MIX_W:2 * MIX_W] * cp[:, 2 * MIX_W:3 * MIX_W]
        zp = jnp.where(i % tiles_per_seq == 0, jnp.zeros_like(zp), zp)
        e1 = zp[7:8, :]
        e0 = zp[6:7, :]
        pos = row
    else:
        e1 = e1_ref[...]
        e0 = e0_ref[...]
        pos = row & (seg - 1)
    z1 = jnp.where(pos == 0, e1, z1)
    z2 = jnp.where(pos == 0, e0, jnp.where(pos == 1, e1, z2))
    cw = cw_ref[...]
    yb = gb * (z2 * cw[0:1, :] + z1 * cw[1:2, :] + z * cw[2:3, :])
    z_ref[...] = z[tm - z_ref.shape[0]:, :]

    hs = _gelu_tanh(hs_ref[...])
    u = hs[:, 0:MIX_W]
    sln = sln_ref[...]
    sv = _ln_rows(hs[:, MIX_W:], sln[0:1, :], sln[1:2, :])
    sv_ref[...] = sv[tm - sv_ref.shape[0]:, :]
    t_i = _iota((SGU_CHUNK, N_HEADS * SGU_CHUNK), 0)
    s_i = _iota((SGU_CHUNK, N_HEADS * SGU_CHUNK), 1) & (SGU_CHUNK - 1)
    log_seg = int(math.log2(min(seg, SGU_CHUNK)))
    keep = (s_i <= t_i) & ((s_i >> log_seg) == (t_i >> log_seg))
    wm = jnp.where(keep, wm_ref[...], 0.0).astype(BF16)
    gmask = (_iota((N_HEADS * SGU_CHUNK, MIX_W), 0) >> 7) == (_iota((N_HEADS * SGU_CHUNK, MIX_W), 1) >> 6)
    sbias = sb_ref[...]
    svb = sv.astype(BF16)
    yd_parts = []
    for ck in range(tm // SGU_CHUNK):
        v_c = svb[ck * SGU_CHUNK:(ck + 1) * SGU_CHUNK]
        mixed = jnp.dot(wm, _stack_heads(v_c, gmask), preferred_element_type=F32) + sbias
        yd_parts.append(u[ck * SGU_CHUNK:(ck + 1) * SGU_CHUNK] * mixed)
    yd = jnp.concatenate(yd_parts, axis=0)

    halves = [slice(0, tm // 2), slice(tm // 2, tm)]
    xs = [x[r] for r in halves]
    xbs = [v.astype(BF16) for v in xs]
    branches = [[b[r].astype(BF16) for r in halves] for b in (ya_ref[...], yb, yc_ref[...], yd)]
    mix = [jnp.zeros((tm // 2, D_MODEL), F32) for _ in halves]
    for br in range(4):
        cols = slice(br * D_MODEL, (br + 1) * D_MODEL)
        gate = [_sigmoid(jnp.dot(v, wg_ref[:, cols], preferred_element_type=F32) + bg_ref[:, cols]) for v in xbs]
        proj = [jnp.dot(v, wb_ref[br], preferred_element_type=F32) for v in branches[br]]
        mix = _each(lambda m, g, p: m + g * p, mix, gate, proj)
    ln = ln_ref[...]
    mixed = [jnp.dot(m.astype(BF16), wo_ref[...], preferred_element_type=F32) for m in mix]
    for r, v, m in zip(halves, xs, mixed):
        o_ref[r, :] = _ln_rows(ALPHA * v + m, ln[0:1, :], ln[1:2, :])


def _merge(x, ya, yc, h_conv, e0, e1, h_sgu, conv_w, sgu_ln, wm, sgu_bias, wg, bgate, wb, wo, ln1, tm, seg):
    t = x.shape[0]
    prompt = seg >= tm
    tiles_per_seq = max(seg // tm, 1)
    z_rows = 8 if prompt else tm
    n_seq = t // seg if prompt else 0
    if prompt:
        cp_spec = pl.BlockSpec((8, 3 * MIX_W), lambda i: (jnp.maximum(i * (tm // 8) - 1, 0), 0))
        e_spec = _const_spec(e0.shape)
        z_spec = pl.BlockSpec((8, MIX_W), lambda i: (i // tiles_per_seq, 0))
        z_shape = jax.ShapeDtypeStruct((n_seq * 8, MIX_W), F32)
    else:
        cp_spec = pl.BlockSpec((8, 3 * MIX_W), lambda i: (0, 0))
        e_spec = pl.BlockSpec((tm, MIX_W), lambda i: (i, 0))
        z_spec = pl.BlockSpec((tm, MIX_W), lambda i: (i, 0))
        z_shape = jax.ShapeDtypeStruct((t, MIX_W), F32)
    row = lambda w: pl.BlockSpec((tm, w), lambda i: (i, 0))
    return pl.pallas_call(
        functools.partial(_merge_body, tm=tm, seg=seg, tiles_per_seq=tiles_per_seq),
        grid=(t // tm,),
        in_specs=[row(D_MODEL), row(MIX_W), row(MIX_W), row(3 * MIX_W), cp_spec, e_spec, e_spec, row(2 * MIX_W),
                  _const_spec(conv_w.shape), _const_spec(sgu_ln.shape), _const_spec(wm.shape),
                  _const_spec(sgu_bias.shape), _const_spec(wg.shape), _const_spec(bgate.shape),
                  _const_spec(wb.shape), _const_spec(wo.shape), _const_spec(ln1.shape)],
        out_specs=[row(D_MODEL), z_spec, z_spec],
        out_shape=[jax.ShapeDtypeStruct((t, D_MODEL), F32), z_shape, z_shape],
        compiler_params=_cparams("arbitrary"),
        name="merge",
    )(x, ya, yc, h_conv, h_conv, e0, e1, h_sgu, conv_w, sgu_ln, wm, sgu_bias, wg, bgate, wb, wo, ln1)


def _matmul2_body(x_ref, w1_ref, w2_ref, o1_ref, o2_ref):
    xb = x_ref[...].astype(BF16)
    o1_ref[...] = jnp.dot(xb, w1_ref[...], preferred_element_type=F32)
    o2_ref[...] = jnp.dot(xb, w2_ref[...], preferred_element_type=F32)


def _matmul2_layers(x, w1, w2, tm):
    t, kdim = x.shape
    n_layers, _, n = w1.shape
    w_spec = pl.BlockSpec((None, kdim, n), lambda l, i: (l, 0, 0))
    o_spec = pl.BlockSpec((None, tm, n), lambda l, i: (l, i, 0))
    o_shape = jax.ShapeDtypeStruct((n_layers, t, n), F32)
    return pl.pallas_call(
        _matmul2_body,
        grid=(n_layers, t // tm),
        in_specs=[pl.BlockSpec((tm, kdim), lambda l, i: (i, 0)), w_spec, w_spec],
        out_specs=[o_spec, o_spec],
        out_shape=[o_shape, o_shape],
        compiler_params=_cparams("parallel", "parallel"),
        name="matmul2",
    )(x, w1, w2)


def _matmul_body(x_ref, w_ref, o_ref):
    o_ref[...] = jnp.dot(x_ref[...].astype(BF16), w_ref[...], preferred_element_type=F32).astype(o_ref.dtype)


def _matmul(x, w, out_dtype, tm):
    t, kdim = x.shape
    n = w.shape[1]
    return pl.pallas_call(
        _matmul_body,
        grid=(t // tm,),
        in_specs=[pl.BlockSpec((tm, kdim), lambda i: (i, 0)), _const_spec(w.shape)],
        out_specs=pl.BlockSpec((tm, n), lambda i: (i, 0)),
        out_shape=jax.ShapeDtypeStruct((t, n), out_dtype),
        compiler_params=_cparams("parallel"),
        name="matmul",
    )(x, w)


def _softmax_rows(sc):
    m = jnp.max(sc, axis=-1, keepdims=True)
    e = jnp.exp(sc - m)
    return e / jnp.sum(e, axis=-1, keepdims=True)


def _xattn_rows_body(q_ref, k_ref, v_ref, o_ref, *, tq, slots):
    rows = X_HEADS * tq
    n_col = N_MEM * X_HEADS
    own = (_iota((rows, n_col), 1) & (X_HEADS - 1)) == (_iota((rows, n_col), 0) >> int(math.log2(tq)))
    for s in range(slots):
        q = q_ref[s * tq:(s + 1) * tq, :]
        qs = jnp.concatenate([q[:, h * X_HD:(h + 1) * X_HD] for h in range(X_HEADS)], axis=0).astype(BF16)
        k_all = k_ref[s].reshape(n_col, X_HD).astype(BF16)
        v_all = v_ref[s].reshape(n_col, X_HD).astype(BF16)
        sc = lax.dot_general(qs, k_all, (((1,), (1,)), ((), ())), preferred_element_type=F32) * (X_HD ** -0.5)
        pr = _softmax_rows(jnp.where(own, sc, -1e30))
        o = jnp.dot(pr.astype(BF16), v_all, preferred_element_type=F32)
        for h in range(X_HEADS):
            o_ref[s * tq:(s + 1) * tq, h * X_HD:(h + 1) * X_HD] = o[h * tq:(h + 1) * tq].astype(o_ref.dtype)


def _xattn_rows(q, mem_k, mem_v, layer, tq, slots):
    t = q.shape[0]
    rows = tq * slots
    mem_spec = pl.BlockSpec((None, slots, N_MEM, X_HEADS, X_HD), lambda i: (layer, i, 0, 0, 0))
    return pl.pallas_call(
        functools.partial(_xattn_rows_body, tq=tq, slots=slots),
        grid=(t // rows,),
        in_specs=[pl.BlockSpec((rows, D_MODEL), lambda i: (i, 0)), mem_spec, mem_spec],
        out_specs=pl.BlockSpec((rows, D_MODEL), lambda i: (i, 0)),
        out_shape=jax.ShapeDtypeStruct((t, D_MODEL), q.dtype),
        compiler_params=_cparams("parallel"),
        name="xattn_rows",
    )(q, mem_k, mem_v)


def _proj_ln_body(y_ref, w_ref, x_ref, ln_ref, o_ref):
    ln = ln_ref[...]
    acc = jnp.dot(y_ref[...].astype(BF16), w_ref[...], preferred_element_type=F32)
    o_ref[...] = _ln_rows(ALPHA * x_ref[...] + acc, ln[0:1, :], ln[1:2, :])


def _proj_ln(y, w, x, ln, tm):
    t = x.shape[0]
    return pl.pallas_call(
        _proj_ln_body,
        grid=(t // tm,),
        in_specs=[pl.BlockSpec((tm, D_MODEL), lambda i: (i, 0)), _const_spec(w.shape),
                  pl.BlockSpec((tm, D_MODEL), lambda i: (i, 0)), _const_spec(ln.shape)],
        out_specs=pl.BlockSpec((tm, D_MODEL), lambda i: (i, 0)),
        out_shape=jax.ShapeDtypeStruct((t, D_MODEL), F32),
        compiler_params=_cparams("parallel"),
        name="proj_ln",
    )(y, w, x, ln)


def _mlp_rows(x, wu_ref, wd_ref, ln):
    xb = x.astype(BF16)
    acc = jnp.zeros(x.shape, F32)
    for j in range(D_FF // D_MODEL):
        cols = slice(j * D_MODEL, (j + 1) * D_MODEL)
        hid = jnp.maximum(jnp.dot(xb, wu_ref[:, cols], preferred_element_type=F32), 0.0)
        acc = acc + jnp.dot((hid * hid).astype(BF16), wd_ref[cols, :], preferred_element_type=F32)
    return _ln_rows(ALPHA * x + acc, ln[0:1, :], ln[1:2, :])


def _mlp_body(x_ref, wu_ref, wd_ref, ln_ref, o_ref):
    o_ref[...] = _mlp_rows(x_ref[...], wu_ref, wd_ref, ln_ref[...])


def _mlp(x, wu, wd, ln, tm):
    t = x.shape[0]
    return pl.pallas_call(
        _mlp_body,
        grid=(t // tm,),
        in_specs=[pl.BlockSpec((tm, D_MODEL), lambda i: (i, 0)), _const_spec(wu.shape), _const_spec(wd.shape),
                  _const_spec(ln.shape)],
        out_specs=pl.BlockSpec((tm, D_MODEL), lambda i: (i, 0)),
        out_shape=jax.ShapeDtypeStruct((t, D_MODEL), F32),
        compiler_params=_cparams("parallel"),
        name="mlp",
    )(x, wu, wd, ln)


def _layer_params(l, w_in, sb_bias, w_gate, b_gate, w_branch, w_o, conv_w, rw_mu, rw_w0, rw_w2, rw_a0, rw_a2, rw_g2,
                  rw_kk, rw_ka, rw_rk, rw_gn_g, rw_gn_b, sgu_ln_g, sgu_ln_b, sgu_ws, sgu_b, w_mq, w_mo,
                  w_up, w_down, ln1_g, ln1_b, ln2_g, ln2_b, ln3_g, ln3_b):
    wi = w_in[l].astype(BF16)
    off_b, off_c, off_d = 3 * MIX_W, 6 * MIX_W, 6 * MIX_W + RW_COLS
    zpad = lambda w, r0: jnp.zeros((128, MIX_W), F32).at[r0:r0 + w.shape[0]].set(w).astype(BF16)
    par = jnp.zeros((8, MIX_W), F32)
    for i, vec in enumerate((rw_w0[l], rw_a0[l], rw_kk[l], rw_ka[l], rw_rk[l].reshape(MIX_W), rw_gn_g[l], rw_gn_b[l])):
        par = par.at[i].set(vec)
    return dict(
        w_in=(wi[:, :MIX_W], wi[:, MIX_W:2 * MIX_W], wi[:, 2 * MIX_W:off_b], wi[:, off_b:off_c], wi[:, off_c:off_d],
              wi[:, off_d:]),
        sb_bias=sb_bias[l],
        w_gate=w_gate[l].astype(BF16), b_gate=b_gate[l].reshape(1, -1), w_branch=w_branch[l].astype(BF16),
        w_o=w_o[l].astype(BF16), conv_w=jnp.zeros((8, MIX_W), F32).at[:3].set(conv_w[l]),
        rw_mu=rw_mu[l].reshape(1, RW_COLS), rw_par=par,
        rw_w2=zpad(rw_w2[l], 0), rw_a2=zpad(rw_a2[l], 32), rw_g2=zpad(rw_g2[l], 64),
        sgu_ln=jnp.stack([sgu_ln_g[l], sgu_ln_b[l]]), sgu_ws=sgu_ws[l], sgu_b=sgu_b[l],
        w_mq=w_mq[l].astype(BF16), w_mo=w_mo[l].astype(BF16),
        w_up=w_up[l].astype(BF16), w_down=w_down[l].astype(BF16),
        ln1=jnp.stack([ln1_g[l], ln1_b[l]]), ln2=jnp.stack([ln2_g[l], ln2_b[l]]), ln3=jnp.stack([ln3_g[l], ln3_b[l]]),
    )


def _sgu_tables(lp, seg):
    ws = lp['sgu_ws']
    sb = lp['sgu_b']
    if seg < SGU_CHUNK:
        reps = SGU_CHUNK // seg
        ws = jnp.tile(ws[:, :seg, :seg], (1, reps, reps))
        sb = jnp.tile(sb[:, :seg], (1, reps))
    wm = jnp.transpose(ws, (1, 0, 2)).reshape(SGU_CHUNK, N_HEADS * SGU_CHUNK)
    bias = jnp.repeat(sb.T, HEAD_W, axis=1)
    return wm, bias


def _sb_bias_rows(bias, tq, width):
    return jnp.broadcast_to(jnp.repeat(bias, tq)[:, None], (N_HEADS * tq, width)).astype(F32)


def _tail_fused_body(x_ref, wq_ref, k_ref, v_ref, wo_ref, ln2_ref, wu_ref, wd_ref, ln3_ref, o_ref):
    x = x_ref[...]
    q = jnp.dot(x.astype(BF16), wq_ref[...], preferred_element_type=F32).astype(BF16)
    kb = k_ref[...].astype(BF16)
    vb = v_ref[...].astype(BF16)
    ctx = []
    for h in range(X_HEADS):
        cols = slice(h * X_HD, (h + 1) * X_HD)
        sc = lax.dot_general(q[:, cols], kb[:, cols], (((1,), (1,)), ((), ())),
                             preferred_element_type=F32) * (X_HD ** -0.5)
        ctx.append(jnp.dot(_softmax_rows(sc).astype(BF16), vb[:, cols], preferred_element_type=F32).astype(BF16))
    att = jnp.concatenate(ctx, axis=1)
    ln2 = ln2_ref[...]
    x2 = _ln_rows(ALPHA * x + jnp.dot(att, wo_ref[...], preferred_element_type=F32), ln2[0:1, :], ln2[1:2, :])
    o_ref[...] = _mlp_rows(x2, wu_ref, wd_ref, ln3_ref[...])


def _tail_fused(x, lp, mem_k, mem_v, layer, tm):
    t = x.shape[0]
    tiles_per_mem = t // (mem_k.shape[1] // N_MEM) // tm
    mem_spec = pl.BlockSpec((None, N_MEM, D_MODEL), lambda i: (layer, i // tiles_per_mem, 0))
    weights = (lp['w_mq'], lp['w_mo'], lp['ln2'], lp['w_up'], lp['w_down'], lp['ln3'])
    wq, wo, ln2, wu, wd, ln3 = (_const_spec(w.shape) for w in weights)
    return pl.pallas_call(
        _tail_fused_body,
        grid=(t // tm,),
        in_specs=[pl.BlockSpec((tm, D_MODEL), lambda i: (i, 0)), wq, mem_spec, mem_spec, wo, ln2, wu, wd, ln3],
        out_specs=pl.BlockSpec((tm, D_MODEL), lambda i: (i, 0)),
        out_shape=jax.ShapeDtypeStruct((t, D_MODEL), F32),
        compiler_params=_cparams("parallel"),
        name="tail",
    )(x, lp['w_mq'], mem_k, mem_v, lp['w_mo'], lp['ln2'], lp['w_up'], lp['w_down'], lp['ln3'])


def _tail(x, lp, attend, q_dtype, tm):
    qm = _matmul(x, lp['w_mq'], q_dtype, tm)
    x = _proj_ln(attend(qm), lp['w_mo'], x, lp['ln2'], tm)
    return _mlp(x, lp['w_up'], lp['w_down'], lp['ln3'], tm)


def _layer_prompt(x, lp, mem_k, mem_v, layer, n, seq):
    t = n * seq
    q, k, v, h_conv, h_rw, h_sgu = _inproj(x, lp['w_in'], ROW_TILE)
    tq = SB_BLOCK
    ya = _sb_prompt(q, k, v, _sb_bias_rows(lp['sb_bias'], tq, tq), n, seq, tq)
    pe = jnp.zeros((n, 8, RW_COLS), F32)
    s0 = jnp.zeros((n, MIX_W, HEAD_W), F32)
    yc, s_fin = _rwkv(h_rw.reshape(n, seq, RW_COLS), pe, s0, lp['rw_mu'], lp['rw_par'], lp['rw_w2'], lp['rw_a2'],
                      lp['rw_g2'], n_prob=n, nb=1, c=RW_ROWS, chain=2)
    wm, sgu_bias = _sgu_tables(lp, SGU_CHUNK)
    zero_e = jnp.zeros((8, MIX_W), F32)
    x1, z_tail, _ = _merge(x, ya, yc.reshape(t, MIX_W), h_conv, zero_e, zero_e, h_sgu, lp['conv_w'], lp['sgu_ln'],
                           wm, sgu_bias, lp['w_gate'], lp['b_gate'], lp['w_branch'], lp['w_o'], lp['ln1'],
                           tm=ROW_TILE, seg=seq)
    x3 = _tail_fused(x1, lp, mem_k, mem_v, layer, ROW_TILE)
    k_new = k.reshape(n, seq, N_HEADS, HEAD_W)
    v_new = v.reshape(n, seq, N_HEADS, HEAD_W)
    conv_new = z_tail.reshape(n, 8, MIX_W)[:, 6:8]
    shift_new = h_rw.reshape(n, seq, RW_COLS)[:, -1]
    return x3, k_new, v_new, conv_new, shift_new, s_fin.reshape(n, N_HEADS, HEAD_W, HEAD_W)


def _layer_sample(x, lp, layer, mem_k, mem_v, cache_kt, cache_vt, page_table, state_conv, state_shift, state_wkv,
                  nb, seq):
    t = nb * seq
    q, k_new, v_new, h_conv, h_rw, h_sgu = _inproj(x, lp['w_in'], ROW_TILE)
    ya = _sb_sample(q, k_new, v_new, cache_kt, cache_vt, page_table, _sb_bias_rows(lp['sb_bias'], seq, PAGE_SIZE),
                    layer, nb, seq)
    per = RW_ROWS // seq
    pe = jnp.repeat(state_shift, seq, axis=0).reshape(nb // per, RW_ROWS, RW_COLS)
    yc, s_fin = _rwkv(h_rw.reshape(nb // per, RW_ROWS, RW_COLS), pe, state_wkv.reshape(nb, MIX_W, HEAD_W), lp['rw_mu'],
                      lp['rw_par'], lp['rw_w2'], lp['rw_a2'], lp['rw_g2'], n_prob=2, nb=per, c=seq, chain=1)
    wm, sgu_bias = _sgu_tables(lp, seq)
    e0 = jnp.repeat(state_conv[:, 0], seq, axis=0)
    e1 = jnp.repeat(state_conv[:, 1], seq, axis=0)
    x1, z_all, sgu_v = _merge(x, ya, yc.reshape(t, MIX_W), h_conv, e0, e1, h_sgu, lp['conv_w'], lp['sgu_ln'], wm,
                              sgu_bias, lp['w_gate'], lp['b_gate'], lp['w_branch'], lp['w_o'], lp['ln1'],
                              tm=ROW_TILE, seg=seq)
    x3 = _tail(x1, lp, lambda qm: _xattn_rows(qm, mem_k, mem_v, layer, seq, XATTN_SLOTS), F32, ROW_TILE)
    conv_new = z_all.reshape(nb, seq, MIX_W)[:, seq - 2:]
    shift_new = h_rw.reshape(nb, seq, RW_COLS)[:, -1]
    return (x3, k_new.reshape(nb, seq, N_HEADS, HEAD_W), v_new.reshape(nb, seq, N_HEADS, HEAD_W), conv_new,
            shift_new, s_fin.reshape(nb, N_HEADS, HEAD_W, HEAD_W), sgu_v.reshape(nb, seq, MIX_W))


def kernel(x_prompt, x_sample, mem_prompt, cache_k, cache_v, page_table, cache_mem_k, cache_mem_v, state_conv,
           state_wkv, state_shift, w_in, sb_bias, w_gate, b_gate, w_branch, w_o, conv_w, rw_mu, rw_w0, rw_w2, rw_a0,
           rw_a2, rw_g2, rw_kk, rw_ka, rw_rk, rw_gn_g, rw_gn_b, sgu_ln_g, sgu_ln_b, sgu_ws, sgu_b, w_mq, w_mk, w_mv,
           w_mo, w_up, w_down, ln1_g, ln1_b, ln2_g, ln2_b, ln3_g, ln3_b):
    n_p, seq_p, _ = x_prompt.shape
    n_s, seq_s, _ = x_sample.shape
    n_phys = cache_k.shape[1]
    xp = x_prompt.reshape(n_p * seq_p, D_MODEL)
    xs = x_sample.reshape(n_s * seq_s, D_MODEL)
    mem2d = mem_prompt.reshape(n_p * N_MEM, D_MODEL)
    cache_kt = jnp.transpose(cache_k, (0, 1, 3, 4, 2)).reshape(DEPTH, n_phys, MIX_W, PAGE_SIZE)
    cache_vt = jnp.transpose(cache_v, (0, 1, 3, 4, 2)).reshape(DEPTH, n_phys, MIX_W, PAGE_SIZE)
    mk, mv = _matmul2_layers(mem2d, w_mk.astype(BF16), w_mv.astype(BF16), ROW_TILE)
    outs = [[] for _ in range(11)]
    for l in range(DEPTH):
        lp = _layer_params(l, w_in, sb_bias, w_gate, b_gate, w_branch, w_o, conv_w, rw_mu, rw_w0, rw_w2, rw_a0,
                           rw_a2, rw_g2, rw_kk, rw_ka, rw_rk, rw_gn_g, rw_gn_b, sgu_ln_g, sgu_ln_b, sgu_ws, sgu_b,
                           w_mq, w_mo, w_up, w_down, ln1_g, ln1_b, ln2_g, ln2_b, ln3_g, ln3_b)
        xp, pk, pv, pc, psh, pst = _layer_prompt(xp, lp, mk, mv, l, n_p, seq_p)
        xs, sk, sv, sc, ssh, sst, scv = _layer_sample(
            xs, lp, l, cache_mem_k, cache_mem_v, cache_kt, cache_vt, page_table,
            state_conv[l], state_shift[l], state_wkv[l], n_s, seq_s)
        for lst, val in zip(outs, (pk, pv, pc, pst, psh, sk, sv, sc, sst, ssh, scv)):
            lst.append(val)
    p_k, p_v, p_conv, p_wkv, p_shift, s_k, s_v, s_conv, s_wkv, s_shift, s_chunk = (jnp.stack(o) for o in outs)
    mem_shape = (DEPTH, n_p, N_MEM, X_HEADS, X_HD)
    return (xp.reshape(n_p, seq_p, D_MODEL), xs.reshape(n_s, seq_s, D_MODEL), p_k, p_v, mk.reshape(mem_shape),
            mv.reshape(mem_shape), p_conv, p_wkv, p_shift, s_k, s_v, s_conv, s_wkv, s_shift, s_chunk)
```
